```python
import jax, jax.numpy as jnp
from jax import lax
import numpy as np

D_MODEL = 1024
BATCH = 2
SEQ = 8192
DEPTH = 1
DEC_BATCH = 16
DEC_SEQ = 32
PAST_LEN = 2048

CHUNK = 64
NORM_EPS = 1e-6
MLA_HEADS = 8
MLA_NOPE = 128
MLA_ROPE = 64
MLA_V = 128
Q_LORA = 512
KV_LORA = 256
ROPE_THETA = 10000.0
Q_BLOCK = 128
MLA_SCALE = (MLA_NOPE + MLA_ROPE) ** -0.5
NEG_INF = -1e30
GLA_HEADS = 4
GLA_DK = D_MODEL // 2
GLA_DV = D_MODEL
GLA_DK_HEAD = GLA_DK // GLA_HEADS
GLA_DV_HEAD = GLA_DV // GLA_HEADS
GLA_GATE_RANK = 16
GLA_GATE_TEMP = 16.0
N_EXPERTS = 32
TOP_K = 4
D_EXPERT = D_MODEL
SWIGLU_LIMIT = 7.0
SWIGLU_ALPHA = 1.702
MOE_BLOCK = 128
IN_SPLITS = (Q_LORA, KV_LORA, MLA_ROPE, GLA_DK, GLA_DK, GLA_DV, GLA_GATE_RANK, GLA_DV, 2 * D_MODEL)
D_IN = Q_LORA + KV_LORA + MLA_ROPE + 2 * GLA_DK + GLA_DV + GLA_GATE_RANK + GLA_DV + 2 * D_MODEL

kernel_name = 'hybrid_mla_gla_moe_streaming_step'


def rmsnorm(x, g):
    xf = x.astype(jnp.float32)
    y = xf * lax.rsqrt(jnp.mean(xf * xf, axis=-1, keepdims=True) + NORM_EPS)
    return (y * g.astype(jnp.float32)).astype(x.dtype)


def rope_tables(pos):
    half = MLA_ROPE // 2
    inv_freq = ROPE_THETA ** (-jnp.arange(half, dtype=jnp.float32) / half)
    ang = pos.astype(jnp.float32)[:, None] * inv_freq[None, :]
    return jnp.cos(ang), jnp.sin(ang)


def apply_rope(x, cos, sin):
    half = x.shape[-1] // 2
    x1, x2 = x[..., :half], x[..., half:]
    return jnp.concatenate([x1 * cos - x2 * sin, x2 * cos + x1 * sin], axis=-1).astype(x.dtype)


def mla_attend(q_nope, q_rope, k_nope, k_rope, v, q_pos, k_pos):
    s = jnp.einsum('bqhd,bkhd->bhqk', q_nope, k_nope) + jnp.einsum('bqhd,bkd->bhqk', q_rope, k_rope)
    s = s.astype(jnp.float32) * MLA_SCALE
    mask = (k_pos[None, :] // CHUNK) <= (q_pos[:, None] // CHUNK)
    s = jnp.where(mask[None, None], s, NEG_INF)
    p = jax.nn.softmax(s, axis=-1).astype(v.dtype)
    return jnp.einsum('bhqk,bkhd->bqhd', p, v)


def mla_attention(q_nope, q_rope, k_nope, k_rope, v, past_len):
    n_q = q_nope.shape[1]
    outs = []
    for i in range(0, n_q, Q_BLOCK):
        j = min(i + Q_BLOCK, n_q)
        kend = past_len + j
        outs.append(mla_attend(q_nope[:, i:j], q_rope[:, i:j], k_nope[:, :kend], k_rope[:, :kend], v[:, :kend],
                               past_len + jnp.arange(i, j), jnp.arange(kend)))
    return jnp.concatenate(outs, axis=1)


def gla_chunk(s0, q, k, v, la):
    c = q.shape[2]
    b = jnp.cumsum(la, axis=2)
    causal = jnp.tril(jnp.ones((c, c), dtype=bool))
    diff = b[:, :, :, None, :] - b[:, :, None, :, :]
    decay = jnp.exp(jnp.where(causal[None, None, :, :, None], diff, -jnp.inf))
    scores = jnp.einsum('bhtsd,bhsd->bhts', q[:, :, :, None, :] * decay, k)
    o = jnp.einsum('bhts,bhsv->bhtv', scores, v) + jnp.einsum('bhtd,bhdv->bhtv', q * jnp.exp(b), s0)
    b_end = b[:, :, -1:, :]
    s_new = jnp.exp(b_end[:, :, 0, :])[..., None] * s0 + jnp.einsum('bhsd,bhsv->bhdv', k * jnp.exp(b_end - b), v)
    return o.astype(v.dtype), s_new


def gla_scan(q, k, v, la, s0):
    bsz, n, h, _ = q.shape
    c = min(CHUNK, n)
    n_c = n // c

    def to_blocks(t):
        return t.reshape(bsz, n_c, c, h, t.shape[-1]).transpose(1, 0, 3, 2, 4)

    def step(s, inp):
        o, s_next = gla_chunk(s, *inp)
        return s_next, o

    s_fin, o = lax.scan(step, s0, (to_blocks(q), to_blocks(k), to_blocks(v), to_blocks(la)))
    return o.transpose(1, 0, 3, 2, 4).reshape(bsz, n, h, -1), s_fin


def moe_ffn(x2d, w_router, b_router, w1, b1, w2, b2):
    n_tok, d = x2d.shape
    logits = x2d.astype(jnp.float32) @ w_router.astype(jnp.float32) + b_router.astype(jnp.float32)
    top_val, top_idx = lax.top_k(logits, TOP_K)
    gate = jax.nn.softmax(top_val, axis=-1).astype(x2d.dtype)
    n_asg = n_tok * TOP_K
    e_flat = top_idx.reshape(n_asg)
    tok_flat = jnp.repeat(jnp.arange(n_tok, dtype=jnp.int32), TOP_K)
    order = jnp.argsort(e_flat)
    e_sorted = e_flat[order]
    counts = jnp.zeros((N_EXPERTS,), jnp.int32).at[e_flat].add(1)
    padded = (counts + MOE_BLOCK - 1) // MOE_BLOCK * MOE_BLOCK
    start = jnp.cumsum(counts) - counts
    pad_end = jnp.cumsum(padded)
    pad_start = pad_end - padded
    dest = pad_start[e_sorted] + jnp.arange(n_asg, dtype=jnp.int32) - start[e_sorted]
    n_slot = -(-n_asg // MOE_BLOCK) * MOE_BLOCK + N_EXPERTS * MOE_BLOCK
    n_blk = n_slot // MOE_BLOCK
    slot_tok = jnp.full((n_slot,), n_tok, jnp.int32).at[dest].set(tok_flat[order])
    slot_gate = jnp.zeros((n_slot,), x2d.dtype).at[dest].set(gate.reshape(n_asg)[order])
    blk_start = jnp.arange(n_blk, dtype=jnp.int32) * MOE_BLOCK
    blk_exp = jnp.minimum(jnp.sum(blk_start[:, None] >= pad_end[None, :], axis=1), N_EXPERTS - 1)
    x_pad = jnp.concatenate([x2d, jnp.zeros((1, d), x2d.dtype)], axis=0)
    xb = x_pad[slot_tok].reshape(n_blk, MOE_BLOCK, d)

    def expert_block(args):
        xe, e = args
        hcat = xe @ w1[e] + b1[e]
        x_glu = jnp.minimum(hcat[:, ::2], SWIGLU_LIMIT)
        x_lin = jnp.clip(hcat[:, 1::2], -SWIGLU_LIMIT, SWIGLU_LIMIT)
        act = x_glu * jax.nn.sigmoid(SWIGLU_ALPHA * x_glu) * (x_lin + 1.0)
        return act @ w2[e] + b2[e]

    yb = lax.map(expert_block, (xb, blk_exp))
    y = jax.ops.segment_sum(yb.reshape(n_slot, d) * slot_gate[:, None], slot_tok, num_segments=n_tok + 1)
    return y[:n_tok]


def layer_forward(x, past_lat, past_kr, s0, norm1_g, w_in, q_norm_g, kv_norm_g, w_uq, w_ukv, w_o_mla,
                  w_alpha2, b_alpha, gla_norm_g, w_o_gla, w_out, norm2_g, w_router, b_router, w1, b1, w2, b2):
    bsz, n, _ = x.shape
    past_len = past_lat.shape[1]
    cos, sin = rope_tables(past_len + jnp.arange(n))
    h = rmsnorm(x, norm1_g)
    offsets = [int(o) for o in np.cumsum(IN_SPLITS)[:-1]]
    c_q, c_kv, k_r, gq, gk, gv, ga, gg, mg = jnp.split(h @ w_in, offsets, axis=-1)

    q = (rmsnorm(c_q, q_norm_g) @ w_uq).reshape(bsz, n, MLA_HEADS, MLA_NOPE + MLA_ROPE)
    q_nope = q[..., :MLA_NOPE]
    q_rope = apply_rope(q[..., MLA_NOPE:], cos[:, None, :], sin[:, None, :])
    lat_new = rmsnorm(c_kv, kv_norm_g)
    kr_new = apply_rope(k_r, cos, sin)
    lat_all = jnp.concatenate([past_lat, lat_new], axis=1)
    kr_all = jnp.concatenate([past_kr, kr_new], axis=1)
    kv = (lat_all @ w_ukv).reshape(bsz, past_len + n, MLA_HEADS, MLA_NOPE + MLA_V)
    attn = mla_attention(q_nope, q_rope, kv[..., :MLA_NOPE], kr_all, kv[..., MLA_NOPE:], past_len)
    y_mla = attn.reshape(bsz, n, MLA_HEADS * MLA_V) @ w_o_mla

    gq = gq.reshape(bsz, n, GLA_HEADS, GLA_DK_HEAD) * (GLA_DK_HEAD ** -0.5)
    gk = gk.reshape(bsz, n, GLA_HEADS, GLA_DK_HEAD)
    gv = gv.reshape(bsz, n, GLA_HEADS, GLA_DV_HEAD)
    la = jax.nn.log_sigmoid((ga @ w_alpha2 + b_alpha).astype(jnp.float32)) / GLA_GATE_TEMP
    la = la.reshape(bsz, n, GLA_HEADS, GLA_DK_HEAD)
    o_gla, s_new = gla_scan(gq, gk, gv, la, s0)
    o_gla = rmsnorm(o_gla, gla_norm_g) * jax.nn.silu(gg.reshape(bsz, n, GLA_HEADS, GLA_DV_HEAD))
    y_gla = o_gla.reshape(bsz, n, GLA_DV) @ w_o_gla

    g_mla, g_gla = jnp.split(jax.nn.sigmoid(mg), 2, axis=-1)
    x = x + (g_mla * y_mla + g_gla * y_gla) @ w_out

    hm = rmsnorm(x, norm2_g).reshape(bsz * n, D_MODEL)
    x = x + moe_ffn(hm, w_router, b_router, w1, b1, w2, b2).reshape(bsz, n, D_MODEL)
    return x, lat_new, kr_new, s_new.astype(x.dtype)


def setup_inputs(seed: int = 0) -> dict:
    key = jax.random.key(seed)
    ks = jax.random.split(key, 25)

    def nrm(k, shape, scale):
        return scale * jax.random.normal(k, shape, jnp.float32)

    L = DEPTH
    return {
        'x_prompt': nrm(ks[0], (BATCH, SEQ, D_MODEL), 1.0),
        'x_sample': nrm(ks[1], (DEC_BATCH, DEC_SEQ, D_MODEL), 1.0),
        'cache_mla_latent': nrm(ks[2], (L, DEC_BATCH, PAST_LEN, KV_LORA), 1.0),
        'cache_mla_krope': nrm(ks[3], (L, DEC_BATCH, PAST_LEN, MLA_ROPE), 1.0),
        'state_gla': nrm(ks[4], (L, DEC_BATCH, GLA_HEADS, GLA_DK_HEAD, GLA_DV_HEAD), 0.5),
        'norm1_g': 1.0 + nrm(ks[5], (L, D_MODEL), 0.01),
        'w_in': nrm(ks[6], (L, D_MODEL, D_IN), D_MODEL ** -0.5),
        'q_norm_g': 1.0 + nrm(ks[7], (L, Q_LORA), 0.01),
        'kv_norm_g': 1.0 + nrm(ks[8], (L, KV_LORA), 0.01),
        'w_uq': nrm(ks[9], (L, Q_LORA, MLA_HEADS * (MLA_NOPE + MLA_ROPE)), Q_LORA ** -0.5),
        'w_ukv': nrm(ks[10], (L, KV_LORA, MLA_HEADS * (MLA_NOPE + MLA_V)), KV_LORA ** -0.5),
        'w_o_mla': nrm(ks[11], (L, MLA_HEADS * MLA_V, D_MODEL), (MLA_HEADS * MLA_V) ** -0.5),
        'w_alpha2': nrm(ks[12], (L, GLA_GATE_RANK, GLA_DK), GLA_GATE_RANK ** -0.5),
        'b_alpha': nrm(ks[13], (L, GLA_DK), 0.1),
        'gla_norm_g': 1.0 + nrm(ks[14], (L, GLA_DV_HEAD), 0.01),
        'w_o_gla': nrm(ks[15], (L, GLA_DV, D_MODEL), GLA_DV ** -0.5),
        'w_out': nrm(ks[16], (L, D_MODEL, D_MODEL), D_MODEL ** -0.5),
        'norm2_g': 1.0 + nrm(ks[17], (L, D_MODEL), 0.01),
        'w_router': nrm(ks[18], (L, D_MODEL, N_EXPERTS), D_MODEL ** -0.5),
        'b_router': nrm(ks[19], (L, N_EXPERTS), 0.01),
        'w1': nrm(ks[20], (L, N_EXPERTS, D_MODEL, 2 * D_EXPERT), D_MODEL ** -0.5),
        'b1': nrm(ks[21], (L, N_EXPERTS, 2 * D_EXPERT), 0.01),
        'w2': nrm(ks[22], (L, N_EXPERTS, D_EXPERT, D_MODEL), D_EXPERT ** -0.5),
        'b2': nrm(ks[23], (L, N_EXPERTS, D_MODEL), 0.01),
        'normf_g': 1.0 + nrm(ks[24], (D_MODEL,), 0.01),
    }


def reference(x_prompt, x_sample, cache_mla_latent, cache_mla_krope, state_gla, norm1_g, w_in, q_norm_g,
              kv_norm_g, w_uq, w_ukv, w_o_mla, w_alpha2, b_alpha, gla_norm_g, w_o_gla, w_out, norm2_g,
              w_router, b_router, w1, b1, w2, b2, normf_g):
    xp, xs = x_prompt, x_sample
    bp = xp.shape[0]
    lat_p, kr_p, st_p, lat_s, kr_s, st_s = [], [], [], [], [], []
    for l in range(DEPTH):
        params = (norm1_g[l], w_in[l], q_norm_g[l], kv_norm_g[l], w_uq[l], w_ukv[l], w_o_mla[l],
                  w_alpha2[l], b_alpha[l], gla_norm_g[l], w_o_gla[l], w_out[l], norm2_g[l],
                  w_router[l], b_router[l], w1[l], b1[l], w2[l], b2[l])
        xp, lp, kp, sp = layer_forward(
            xp, jnp.zeros((bp, 0, KV_LORA), xp.dtype), jnp.zeros((bp, 0, MLA_ROPE), xp.dtype),
            jnp.zeros((bp, GLA_HEADS, GLA_DK_HEAD, GLA_DV_HEAD), jnp.float32), *params)
        xs, ls, kss, ss = layer_forward(
            xs, cache_mla_latent[l], cache_mla_krope[l], state_gla[l].astype(jnp.float32), *params)
        lat_p.append(lp)
        kr_p.append(kp)
        st_p.append(sp)
        lat_s.append(ls)
        kr_s.append(kss)
        st_s.append(ss)
    y_prompt = rmsnorm(xp, normf_g)
    y_sample = rmsnorm(xs, normf_g)
    return (y_prompt, y_sample, jnp.stack(lat_p), jnp.stack(kr_p), jnp.stack(st_p),
            jnp.stack(lat_s), jnp.stack(kr_s), jnp.stack(st_s))
```

```python
import functools

import jax
import jax.numpy as jnp
import numpy as np
from jax import lax
from jax.experimental import pallas as pl
from jax.experimental.pallas import tpu as pltpu

F32 = jnp.float32
BF16 = jnp.bfloat16

D_MODEL = 1024
CHUNK = 64
NORM_EPS = 1e-6
MLA_HEADS = 8
MLA_NOPE = 128
MLA_ROPE = 64
MLA_V = 128
Q_LORA = 512
KV_LORA = 256
ROPE_THETA = 10000.0
MLA_SCALE = (MLA_NOPE + MLA_ROPE) ** -0.5
NEG_INF = -1e30
GLA_HEADS = 4
GLA_DK = D_MODEL // 2
GLA_DV = D_MODEL
GLA_DK_HEAD = GLA_DK // GLA_HEADS
GLA_DV_HEAD = GLA_DV // GLA_HEADS
GLA_GATE_RANK = 16
GLA_GATE_TEMP = 16.0
GLA_SUB = 16
N_EXPERTS = 32
TOP_K = 4
D_EXPERT = D_MODEL
SWIGLU_LIMIT = 7.0
SWIGLU_ALPHA = 1.702
MOE_ROWS = 256

LANES = 128
HEAD_PAD = 256

COL_CQ = 0
COL_CKV = 512
COL_KR = 768
COL_GA = 896
COL_GV = 1024
COL_GG = 2048
COL_GQ = 3072
COL_GK = 3584
COL_MG = 4096
D_IN_PAD = 6144

VMEM_LIMIT = 56 * 1024 * 1024


def _cparams(*sem):
    return pltpu.CompilerParams(dimension_semantics=sem, vmem_limit_bytes=VMEM_LIMIT)


def _tile(n, pref, mult=16):
    t = min(pref, n)
    t -= t % mult
    while n % t:
        t -= mult
    return t


def _rms(x, g):
    return x * lax.rsqrt(jnp.mean(x * x, axis=-1, keepdims=True) + NORM_EPS) * g


def _in_proj_kernel(x_ref, g_ref, w_ref, o_ref, h_scr):
    @pl.when(pl.program_id(1) == 0)
    def _():
        h_scr[...] = _rms(x_ref[...], g_ref[...]).astype(BF16)

    o_ref[...] = jnp.dot(h_scr[...], w_ref[...], preferred_element_type=F32)


def _in_proj(x2d, g, w_all):
    n = x2d.shape[0]
    tm = min(512, n)
    tn = 1536
    return pl.pallas_call(
        _in_proj_kernel,
        grid=(n // tm, D_IN_PAD // tn),
        in_specs=[
            pl.BlockSpec((tm, D_MODEL), lambda i, j: (i, 0)),
            pl.BlockSpec((1, D_MODEL), lambda i, j: (0, 0)),
            pl.BlockSpec((D_MODEL, tn), lambda i, j: (0, j)),
        ],
        out_specs=pl.BlockSpec((tm, tn), lambda i, j: (i, j)),
        out_shape=jax.ShapeDtypeStruct((n, D_IN_PAD), F32),
        scratch_shapes=[pltpu.VMEM((tm, D_MODEL), BF16)],
        compiler_params=_cparams("parallel", "arbitrary"),
        name="in_proj",
    )(x2d, g, w_all)


def _rope_pair(blk, cs):
    t = blk * cs
    return t + pltpu.roll(t, MLA_ROPE, axis=1)


def _mla_q_kernel(h_ref, cs_ref, qg_ref, kvg_ref, wuq_ref, q_ref, lat_ref, kr_ref):
    cs = cs_ref[...]
    qn = _rms(h_ref[:, COL_CQ:COL_CQ + Q_LORA], qg_ref[...]).astype(BF16)
    q = jnp.dot(qn, wuq_ref[...], preferred_element_type=F32)
    for hh in range(MLA_HEADS):
        c0 = hh * HEAD_PAD
        q_ref[:, c0:c0 + MLA_NOPE] = q[:, c0:c0 + MLA_NOPE].astype(BF16)
        q_ref[:, c0 + MLA_NOPE:c0 + HEAD_PAD] = _rope_pair(q[:, c0 + MLA_NOPE:c0 + HEAD_PAD], cs).astype(BF16)
    lat_ref[...] = _rms(h_ref[:, COL_CKV:COL_CKV + KV_LORA], kvg_ref[...])
    r = _rope_pair(h_ref[:, COL_KR:COL_KR + LANES], cs)
    lane = lax.broadcasted_iota(jnp.int32, r.shape, 1)
    kr_ref[...] = jnp.where(lane < MLA_ROPE, r, 0.0)


def _mla_q(hin, cs, qg, kvg, wuq):
    n = hin.shape[0]
    tm = min(512, n)
    n_cs = cs.shape[0] // tm
    return pl.pallas_call(
        _mla_q_kernel,
        grid=(n // tm,),
        in_specs=[
            pl.BlockSpec((tm, COL_GA), lambda i: (i, 0)),
            pl.BlockSpec((tm, LANES), lambda i: (i % n_cs, 0)),
            pl.BlockSpec((1, Q_LORA), lambda i: (0, 0)),
            pl.BlockSpec((1, KV_LORA), lambda i: (0, 0)),
            pl.BlockSpec((Q_LORA, MLA_HEADS * HEAD_PAD), lambda i: (0, 0)),
        ],
        out_specs=[
            pl.BlockSpec((tm, MLA_HEADS * HEAD_PAD), lambda i: (i, 0)),
            pl.BlockSpec((tm, KV_LORA), lambda i: (i, 0)),
            pl.BlockSpec((tm, LANES), lambda i: (i, 0)),
        ],
        out_shape=[
            jax.ShapeDtypeStruct((n, MLA_HEADS * HEAD_PAD), BF16),
            jax.ShapeDtypeStruct((n, KV_LORA), F32),
            jax.ShapeDtypeStruct((n, LANES), F32),
        ],
        compiler_params=_cparams("parallel"),
        name="mla_q",
    )(hin, cs, qg, kvg, wuq)


def _kv_up_kernel(lat_ref, kr_ref, w_ref, k_ref, v_ref):
    kv = jnp.dot(lat_ref[...].astype(BF16), w_ref[...], preferred_element_type=F32)
    krb = kr_ref[...].astype(BF16)
    for hh in range(MLA_HEADS):
        c0 = hh * HEAD_PAD
        k_ref[:, c0:c0 + MLA_NOPE] = kv[:, hh * MLA_NOPE:(hh + 1) * MLA_NOPE].astype(BF16)
        k_ref[:, c0 + MLA_NOPE:c0 + HEAD_PAD] = krb
    v_ref[...] = kv[:, MLA_HEADS * MLA_NOPE:].astype(BF16)


def _kv_up(lat, krp, wukv):
    n = lat.shape[0]
    tm = _tile(n, 512)
    return pl.pallas_call(
        _kv_up_kernel,
        grid=(n // tm,),
        in_specs=[
            pl.BlockSpec((tm, KV_LORA), lambda i: (i, 0)),
            pl.BlockSpec((tm, LANES), lambda i: (i, 0)),
            pl.BlockSpec((KV_LORA, MLA_HEADS * (MLA_NOPE + MLA_V)), lambda i: (0, 0)),
        ],
        out_specs=[
            pl.BlockSpec((tm, MLA_HEADS * HEAD_PAD), lambda i: (i, 0)),
            pl.BlockSpec((tm, MLA_HEADS * MLA_V), lambda i: (i, 0)),
        ],
        out_shape=[
            jax.ShapeDtypeStruct((n, MLA_HEADS * HEAD_PAD), BF16),
            jax.ShapeDtypeStruct((n, MLA_HEADS * MLA_V), BF16),
        ],
        compiler_params=_cparams("parallel"),
        name="kv_up",
    )(lat, krp, wukv)


def _flash_kernel(q_ref, k_ref, v_ref, o_ref, *, tq, tk, past_len):
    qi = pl.program_id(2)
    q = q_ref[0]
    q_pos0 = past_len + qi * tq
    q_chunk = (q_pos0 + lax.broadcasted_iota(jnp.int32, (tq, 1), 0)) // CHUNK
    n_blocks = (q_pos0 + tq + tk - 1) // tk
    n_full = ((q_pos0 // CHUNK) * CHUNK) // tk

    def step(j, carry, masked):
        m, l, acc = carry
        k0 = pl.multiple_of(j * tk, tk)
        k = k_ref[0, pl.ds(k0, tk), :]
        v = v_ref[0, pl.ds(k0, tk), :]
        s = lax.dot_general(q, k, (((1,), (1,)), ((), ())), preferred_element_type=F32) * MLA_SCALE
        if masked:
            k_chunk = (k0 + lax.broadcasted_iota(jnp.int32, (1, tk), 1)) // CHUNK
            s = jnp.where(k_chunk <= q_chunk, s, NEG_INF)
        m_new = jnp.maximum(m, jnp.max(s, axis=-1, keepdims=True))
        alpha = jnp.exp(m - m_new)
        p = jnp.exp(s - m_new)
        l = alpha * l + jnp.sum(p, axis=-1, keepdims=True)
        acc = alpha * acc + jnp.dot(p.astype(BF16), v, preferred_element_type=F32)
        return m_new, l, acc

    init = (jnp.full((tq, 1), NEG_INF, F32), jnp.zeros((tq, 1), F32), jnp.zeros((tq, MLA_V), F32))
    carry = lax.fori_loop(0, n_full, functools.partial(step, masked=False), init)
    m, l, acc = lax.fori_loop(n_full, n_blocks, functools.partial(step, masked=True), carry)
    o_ref[0] = (acc / l).astype(BF16)


def _flash(q, k, v, past_len, tq, tk):
    b, sq, _ = q.shape
    sk = k.shape[1]
    return pl.pallas_call(
        functools.partial(_flash_kernel, tq=tq, tk=tk, past_len=past_len),
        grid=(b, MLA_HEADS, sq // tq),
        in_specs=[
            pl.BlockSpec((1, tq, HEAD_PAD), lambda bi, h, i: (bi, i, h)),
            pl.BlockSpec((1, sk, HEAD_PAD), lambda bi, h, i: (bi, 0, h)),
            pl.BlockSpec((1, sk, MLA_V), lambda bi, h, i: (bi, 0, h)),
        ],
        out_specs=pl.BlockSpec((1, tq, MLA_V), lambda bi, h, i: (bi, i, h)),
        out_shape=jax.ShapeDtypeStruct((b, sq, MLA_HEADS * MLA_V), BF16),
        compiler_params=_cparams("parallel", "parallel", "arbitrary"),
        name="flash",
    )(q, k, v)


def _cumsum_rows(x):
    n = x.shape[0]
    row = lax.broadcasted_iota(jnp.int32, x.shape, 0)
    s = 1
    while s < n:
        x = x + jnp.where(row >= s, pltpu.roll(x, s, axis=0), 0.0)
        s *= 2
    return x


def _gla_kernel(gv_ref, gg_ref, gq_ref, gk_ref, ga_ref, wa_ref, ba_ref, ng_ref, s0_ref, og_ref, sn_ref, st_scr,
                *, csize, n_chunks):
    t = pl.program_id(2)

    @pl.when(t == 0)
    def _():
        st_scr[...] = s0_ref[0, 0].T

    wa = wa_ref[...]
    ba = ba_ref[...]
    ng = ng_ref[...]
    n_sub = csize // GLA_SUB
    col_id = lax.broadcasted_iota(jnp.int32, (GLA_SUB, csize), 1)
    row_id = lax.broadcasted_iota(jnp.int32, (GLA_SUB, GLA_DK_HEAD), 0)

    def chunk(c, carry):
        r0 = pl.multiple_of(c * csize, csize)
        rows = pl.ds(r0, csize)
        q = gq_ref[rows, :] * (GLA_DK_HEAD ** -0.5)
        k = gk_ref[rows, :]
        v = gv_ref[rows, :]
        x = jnp.dot(ga_ref[rows, :].astype(BF16), wa, preferred_element_type=F32) + ba
        la = jax.nn.log_sigmoid(x) / GLA_GATE_TEMP
        b = _cumsum_rows(la)

        a_rows = []
        for i in range(n_sub):
            lo = i * GLA_SUB
            bi = b[lo:lo + GLA_SUB]
            qi = q[lo:lo + GLA_SUB]
            ki = k[lo:lo + GLA_SUB]
            if i == 0:
                a_i = jnp.zeros((GLA_SUB, csize), F32)
            else:
                ri = b[lo - 1:lo]
                qs = qi * jnp.exp(bi - ri)
                ks = k * jnp.exp(jnp.minimum(ri - b, 0.0))
                a_i = lax.dot_general(qs.astype(BF16), ks.astype(BF16), (((1,), (1,)), ((), ())),
                                      preferred_element_type=F32)
                a_i = jnp.where(col_id < lo, a_i, 0.0)
            for s in range(GLA_SUB):
                e = jnp.where(row_id >= s, jnp.exp(bi - bi[s:s + 1]), 0.0)
                col = jnp.sum(qi * ki[s:s + 1] * e, axis=-1, keepdims=True)
                a_i = jnp.where(col_id == lo + s, col, a_i)
            a_rows.append(a_i)
        a = jnp.concatenate(a_rows, axis=0)

        st = st_scr[...]
        vb = v.astype(BF16)
        o = jnp.dot(a.astype(BF16), vb, preferred_element_type=F32)
        o = o + lax.dot_general((q * jnp.exp(b)).astype(BF16), st.astype(BF16), (((1,), (1,)), ((), ())),
                                preferred_element_type=F32)
        b_end = b[csize - 1:csize]
        kd = (k * jnp.exp(b_end - b)).astype(BF16)
        st_scr[...] = jnp.exp(b_end) * st + jnp.dot(v.T.astype(BF16), kd, preferred_element_type=F32)
        on = _rms(o, ng)
        og_ref[rows, :] = (on * jax.nn.silu(gg_ref[rows, :])).astype(BF16)
        return carry

    lax.fori_loop(0, n_chunks, chunk, 0)

    @pl.when(t == pl.num_programs(2) - 1)
    def _():
        sn_ref[0, 0] = st_scr[...].T


def _gla(hin, wa, ba, ng, s0, bsz, seq):
    csize = min(CHUNK, seq)
    tt = min(512, seq)
    n_t = seq // tt
    n = bsz * seq
    kern = functools.partial(_gla_kernel, csize=csize, n_chunks=tt // csize)
    vblk = GLA_DV_HEAD
    kblk = GLA_DK_HEAD
    return pl.pallas_call(
        kern,
        grid=(bsz, GLA_HEADS, n_t),
        in_specs=[
            pl.BlockSpec((tt, vblk), lambda b, h, t: (b * n_t + t, COL_GV // vblk + h)),
            pl.BlockSpec((tt, vblk), lambda b, h, t: (b * n_t + t, COL_GG // vblk + h)),
            pl.BlockSpec((tt, kblk), lambda b, h, t: (b * n_t + t, COL_GQ // kblk + h)),
            pl.BlockSpec((tt, kblk), lambda b, h, t: (b * n_t + t, COL_GK // kblk + h)),
            pl.BlockSpec((tt, LANES), lambda b, h, t: (b * n_t + t, COL_GA // LANES)),
            pl.BlockSpec((LANES, kblk), lambda b, h, t: (0, h)),
            pl.BlockSpec((1, kblk), lambda b, h, t: (0, h)),
            pl.BlockSpec((1, vblk), lambda b, h, t: (0, 0)),
            pl.BlockSpec((1, 1, kblk, vblk), lambda b, h, t: (b, h, 0, 0)),
        ],
        out_specs=[
            pl.BlockSpec((tt, vblk), lambda b, h, t: (b * n_t + t, h)),
            pl.BlockSpec((1, 1, kblk, vblk), lambda b, h, t: (b, h, 0, 0)),
        ],
        out_shape=[
            jax.ShapeDtypeStruct((n, GLA_DV), BF16),
            jax.ShapeDtypeStruct((bsz, GLA_HEADS, kblk, vblk), F32),
        ],
        scratch_shapes=[pltpu.VMEM((vblk, kblk), F32)],
        compiler_params=_cparams("parallel", "parallel", "arbitrary"),
        name="gla",
    )(hin, hin, hin, hin, hin, wa, ba, ng, s0)


def _post_mix_kernel(at_ref, og_ref, mg1_ref, mg2_ref, x_ref, wom_ref, wog_ref, wout_ref, n2_ref, wr_ref, br_ref,
                     x1_ref, hm_ref, lg_ref):
    y_mla = jnp.dot(at_ref[...], wom_ref[...], preferred_element_type=F32)
    y_gla = jnp.dot(og_ref[...], wog_ref[...], preferred_element_type=F32)
    m = jax.nn.sigmoid(mg1_ref[...]) * y_mla + jax.nn.sigmoid(mg2_ref[...]) * y_gla
    x1 = x_ref[...] + jnp.dot(m.astype(BF16), wout_ref[...], preferred_element_type=F32)
    x1_ref[...] = x1
    hm = _rms(x1, n2_ref[...])
    hmb = hm.astype(BF16)
    hm_ref[...] = hmb
    lg_ref[...] = jnp.dot(hmb, wr_ref[...], preferred_element_type=F32) + br_ref[...]


def _post_mix(attn, og, hin, x2d, wom, wog, wout, n2, wr, br):
    n = x2d.shape[0]
    tm = min(256, n)
    row = lambda i: (i, 0)
    const = lambda i: (0, 0)
    return pl.pallas_call(
        _post_mix_kernel,
        grid=(n // tm,),
        in_specs=[
            pl.BlockSpec((tm, D_MODEL), row),
            pl.BlockSpec((tm, D_MODEL), row),
            pl.BlockSpec((tm, D_MODEL), lambda i: (i, COL_MG // D_MODEL)),
            pl.BlockSpec((tm, D_MODEL), lambda i: (i, COL_MG // D_MODEL + 1)),
            pl.BlockSpec((tm, D_MODEL), row),
            pl.BlockSpec((D_MODEL, D_MODEL), const),
            pl.BlockSpec((D_MODEL, D_MODEL), const),
            pl.BlockSpec((D_MODEL, D_MODEL), const),
            pl.BlockSpec((1, D_MODEL), const),
            pl.BlockSpec((D_MODEL, LANES), const),
            pl.BlockSpec((1, LANES), const),
        ],
        out_specs=[
            pl.BlockSpec((tm, D_MODEL), row),
            pl.BlockSpec((tm, D_MODEL), row),
            pl.BlockSpec((tm, LANES), row),
        ],
        out_shape=[
            jax.ShapeDtypeStruct((n, D_MODEL), F32),
            jax.ShapeDtypeStruct((n, D_MODEL), BF16),
            jax.ShapeDtypeStruct((n, LANES), F32),
        ],
        compiler_params=_cparams("parallel"),
        name="post_mix",
    )(attn, og, hin, hin, x2d, wom, wog, wout, n2, wr, br)


def _expert_kernel(be_ref, x_ref, w1g_ref, w1l_ref, b1g_ref, b1l_ref, w2_ref, b2_ref, o_ref):
    del be_ref
    x = x_ref[...]
    hg = jnp.dot(x, w1g_ref[0], preferred_element_type=F32) + b1g_ref[0]
    hl = jnp.dot(x, w1l_ref[0], preferred_element_type=F32) + b1l_ref[0]
    glu = jnp.minimum(hg, SWIGLU_LIMIT)
    lin = jnp.clip(hl, -SWIGLU_LIMIT, SWIGLU_LIMIT)
    act = glu * jax.nn.sigmoid(SWIGLU_ALPHA * glu) * (lin + 1.0)
    o_ref[...] = jnp.dot(act.astype(BF16), w2_ref[0], preferred_element_type=F32) + b2_ref[0]


def _experts(blk_exp, xb, w1g, w1l, b1g, b1l, w2, b2):
    n_slot = xb.shape[0]
    n_blk = n_slot // MOE_ROWS
    row = lambda i, be: (i, 0)
    wsel = lambda i, be: (be[i], 0, 0)
    grid_spec = pltpu.PrefetchScalarGridSpec(
        num_scalar_prefetch=1,
        grid=(n_blk,),
        in_specs=[
            pl.BlockSpec((MOE_ROWS, D_MODEL), row),
            pl.BlockSpec((1, D_MODEL, D_EXPERT), wsel),
            pl.BlockSpec((1, D_MODEL, D_EXPERT), wsel),
            pl.BlockSpec((1, 1, D_EXPERT), wsel),
            pl.BlockSpec((1, 1, D_EXPERT), wsel),
            pl.BlockSpec((1, D_EXPERT, D_MODEL), wsel),
            pl.BlockSpec((1, 1, D_MODEL), wsel),
        ],
        out_specs=pl.BlockSpec((MOE_ROWS, D_MODEL), row),
    )
    return pl.pallas_call(
        _expert_kernel,
        grid_spec=grid_spec,
        out_shape=jax.ShapeDtypeStruct((n_slot, D_MODEL), F32),
        compiler_params=_cparams("arbitrary"),
        name="experts",
    )(blk_exp, xb, w1g, w1l, b1g, b1l, w2, b2)


def _final_kernel(x_ref, y_ref, g_ref, o_ref):
    o_ref[...] = _rms(x_ref[...] + y_ref[...], g_ref[...])


def _final(x1, ymoe, g):
    n = x1.shape[0]
    tm = min(512, n)
    row = lambda i: (i, 0)
    return pl.pallas_call(
        _final_kernel,
        grid=(n // tm,),
        in_specs=[pl.BlockSpec((tm, D_MODEL), row), pl.BlockSpec((tm, D_MODEL), row),
                  pl.BlockSpec((1, D_MODEL), lambda i: (0, 0))],
        out_specs=pl.BlockSpec((tm, D_MODEL), row),
        out_shape=jax.ShapeDtypeStruct((n, D_MODEL), F32),
        compiler_params=_cparams("parallel"),
        name="final_norm",
    )(x1, ymoe, g)


def _rope_table(pos):
    half = MLA_ROPE // 2
    inv_freq = ROPE_THETA ** (-jnp.arange(half, dtype=F32) / half)
    ang = pos.astype(F32)[:, None] * inv_freq[None, :]
    cos, sin = jnp.cos(ang), jnp.sin(ang)
    return jnp.concatenate([cos, cos, -sin, sin], axis=-1)


def _regroup_w_in(w_in):
    o = np.cumsum((Q_LORA, KV_LORA, MLA_ROPE, GLA_DK, GLA_DK, GLA_DV, GLA_GATE_RANK, GLA_DV, 2 * D_MODEL))
    c_q, c_kv, k_r = w_in[:, :o[0]], w_in[:, o[0]:o[1]], w_in[:, o[1]:o[2]]
    gq, gk, gv = w_in[:, o[2]:o[3]], w_in[:, o[3]:o[4]], w_in[:, o[4]:o[5]]
    ga, gg, mg = w_in[:, o[5]:o[6]], w_in[:, o[6]:o[7]], w_in[:, o[7]:o[8]]
    half = MLA_ROPE // 2
    k_r_sw = jnp.concatenate([k_r[:, half:], k_r[:, :half]], axis=1)
    ga_pad = jnp.zeros((D_MODEL, LANES - GLA_GATE_RANK), w_in.dtype)
    return jnp.concatenate([c_q, c_kv, k_r, k_r_sw, ga, ga_pad, gv, gg, gq, gk, mg], axis=1).astype(BF16)


def _regroup_w_uq(w_uq):
    w = w_uq.reshape(Q_LORA, MLA_HEADS, MLA_NOPE + MLA_ROPE)
    half = MLA_ROPE // 2
    nope, rope = w[..., :MLA_NOPE], w[..., MLA_NOPE:]
    rope_sw = jnp.concatenate([rope[..., half:], rope[..., :half]], axis=-1)
    return jnp.concatenate([nope, rope, rope_sw], axis=-1).reshape(Q_LORA, MLA_HEADS * HEAD_PAD).astype(BF16)


def _regroup_w_ukv(w_ukv):
    w = w_ukv.reshape(KV_LORA, MLA_HEADS, MLA_NOPE + MLA_V)
    k = w[..., :MLA_NOPE].reshape(KV_LORA, MLA_HEADS * MLA_NOPE)
    v = w[..., MLA_NOPE:].reshape(KV_LORA, MLA_HEADS * MLA_V)
    return jnp.concatenate([k, v], axis=1).astype(BF16)


def _route(logits, n_tok):
    top_val, top_idx = lax.top_k(logits, TOP_K)
    gate = jax.nn.softmax(top_val, axis=-1)
    n_asg = n_tok * TOP_K
    e_flat = top_idx.reshape(n_asg)
    tok_flat = jnp.repeat(jnp.arange(n_tok, dtype=jnp.int32), TOP_K)
    order = jnp.argsort(e_flat)
    e_sorted = e_flat[order]
    counts = jnp.zeros((N_EXPERTS,), jnp.int32).at[e_flat].add(1)
    padded = (counts + MOE_ROWS - 1) // MOE_ROWS * MOE_ROWS
    start = jnp.cumsum(counts) - counts
    pad_end = jnp.cumsum(padded)
    pad_start = pad_end - padded
    dest = pad_start[e_sorted] + jnp.arange(n_asg, dtype=jnp.int32) - start[e_sorted]
    n_slot = -(-n_asg // MOE_ROWS) * MOE_ROWS + N_EXPERTS * MOE_ROWS
    n_blk = n_slot // MOE_ROWS
    slot_tok = jnp.full((n_slot,), n_tok, jnp.int32).at[dest].set(tok_flat[order])
    slot_gate = jnp.zeros((n_slot,), F32).at[dest].set(gate.reshape(n_asg)[order])
    blk_start = jnp.arange(n_blk, dtype=jnp.int32) * MOE_ROWS
    blk_exp = jnp.minimum(jnp.sum(blk_start[:, None] >= pad_end[None, :], axis=1), N_EXPERTS - 1).astype(jnp.int32)
    return slot_tok, slot_gate, blk_exp


def _mixers(x, past_lat, past_kr, s0, past_len, p):
    bsz, seq, _ = x.shape
    n = bsz * seq
    x2d = x.reshape(n, D_MODEL)
    hin = _in_proj(x2d, p["norm1_g"], p["w_all"])
    tm = min(512, n)
    cs = _rope_table(past_len + jnp.arange(seq))
    if seq < tm:
        cs = jnp.tile(cs, (tm // seq, 1))
    q, lat_new, krp_new = _mla_q(hin, cs, p["q_norm_g"], p["kv_norm_g"], p["w_uq"])
    if past_len:
        lat_all = jnp.concatenate([past_lat, lat_new.reshape(bsz, seq, KV_LORA)], axis=1)
        kr_pad = jnp.pad(past_kr, ((0, 0), (0, 0), (0, LANES - MLA_ROPE)))
        krp_all = jnp.concatenate([kr_pad, krp_new.reshape(bsz, seq, LANES)], axis=1)
    else:
        lat_all, krp_all = lat_new, krp_new
    sk = past_len + seq
    kcat, v = _kv_up(lat_all.reshape(bsz * sk, KV_LORA), krp_all.reshape(bsz * sk, LANES), p["w_ukv"])
    tq = min(512, seq)
    tk = 512 if sk % 512 == 0 else sk
    attn = _flash(q.reshape(bsz, seq, -1), kcat.reshape(bsz, sk, -1), v.reshape(bsz, sk, -1), past_len, tq, tk)
    og, s_new = _gla(hin, p["w_alpha"], p["b_alpha"], p["gla_norm_g"], s0, bsz, seq)
    x1, hm, logits = _post_mix(attn.reshape(n, -1), og, hin, x2d, p["w_o_mla"], p["w_o_gla"], p["w_out"],
                               p["norm2_g"], p["w_router"], p["b_router"])
    return x1, hm, logits, lat_new.reshape(bsz, seq, KV_LORA), krp_new[:, :MLA_ROPE].reshape(bsz, seq, MLA_ROPE), s_new


def kernel(x_prompt, x_sample, cache_mla_latent, cache_mla_krope, state_gla, norm1_g, w_in, q_norm_g, kv_norm_g,
           w_uq, w_ukv, w_o_mla, w_alpha2, b_alpha, gla_norm_g, w_o_gla, w_out, norm2_g, w_router, b_router,
           w1, b1, w2, b2, normf_g):
    depth = w_in.shape[0]
    assert depth == 1
    l = 0
    bp, sp, _ = x_prompt.shape
    bs, ss, _ = x_sample.shape
    past_len = cache_mla_latent.shape[2]
    p = {
        "norm1_g": norm1_g[l][None, :],
        "w_all": _regroup_w_in(w_in[l]),
        "q_norm_g": q_norm_g[l][None, :],
        "kv_norm_g": kv_norm_g[l][None, :],
        "w_uq": _regroup_w_uq(w_uq[l]),
        "w_ukv": _regroup_w_ukv(w_ukv[l]),
        "w_o_mla": w_o_mla[l].astype(BF16),
        "w_alpha": jnp.pad(w_alpha2[l], ((0, LANES - GLA_GATE_RANK), (0, 0))).astype(BF16),
        "b_alpha": b_alpha[l][None, :],
        "gla_norm_g": gla_norm_g[l][None, :],
        "w_o_gla": w_o_gla[l].astype(BF16),
        "w_out": w_out[l].astype(BF16),
        "norm2_g": norm2_g[l][None, :],
        "w_router": jnp.pad(w_router[l], ((0, 0), (0, LANES - N_EXPERTS))).astype(BF16),
        "b_router": jnp.pad(b_router[l], (0, LANES - N_EXPERTS))[None, :],
    }
    zeros_state = jnp.zeros((bp, GLA_HEADS, GLA_DK_HEAD, GLA_DV_HEAD), F32)
    x1p, hmp, lgp, lat_p, kr_p, st_p = _mixers(x_prompt, None, None, zeros_state, 0, p)
    x1s, hms, lgs, lat_s, kr_s, st_s = _mixers(x_sample, cache_mla_latent[l], cache_mla_krope[l], state_gla[l],
                                               past_len, p)

    n_p, n_s = bp * sp, bs * ss
    n_tok = n_p + n_s
    hm = jnp.concatenate([hmp, hms], axis=0)
    logits = jnp.concatenate([lgp, lgs], axis=0)[:, :N_EXPERTS]
    slot_tok, slot_gate, blk_exp = _route(logits, n_tok)
    hm_pad = jnp.concatenate([hm, jnp.zeros((1, D_MODEL), BF16)], axis=0)
    xb = hm_pad[slot_tok]
    w1l_ = w1[l]
    w1g = w1l_[:, :, 0::2].astype(BF16)
    w1lin = w1l_[:, :, 1::2].astype(BF16)
    b1g = b1[l][:, None, 0::2]
    b1lin = b1[l][:, None, 1::2]
    yb = _experts(blk_exp, xb, w1g, w1lin, b1g, b1lin, w2[l].astype(BF16), b2[l][:, None, :])
    ymoe = jax.ops.segment_sum(yb * slot_gate[:, None], slot_tok, num_segments=n_tok + 1)[:n_tok]

    yp = _final(x1p, ymoe[:n_p], normf_g[None, :]).reshape(bp, sp, D_MODEL)
    ys = _final(x1s, ymoe[n_p:], normf_g[None, :]).reshape(bs, ss, D_MODEL)
    return (yp, ys, lat_p[None], kr_p[None], st_p[None], lat_s[None], kr_s[None], st_s[None])
```

```python
import functools

import jax
import jax.numpy as jnp
import numpy as np
from jax import lax
from jax.experimental import pallas as pl
from jax.experimental.pallas import tpu as pltpu

F32 = jnp.float32
BF16 = jnp.bfloat16

D_MODEL = 1024
CHUNK = 64
CHUNK_SHIFT = 6
LOG2_E = 1.4426950408889634
NORM_EPS = 1e-6
MLA_HEADS = 8
MLA_NOPE = 128
MLA_ROPE = 64
MLA_V = 128
Q_LORA = 512
KV_LORA = 256
ROPE_THETA = 10000.0
MLA_SCALE = (MLA_NOPE + MLA_ROPE) ** -0.5
NEG_INF = -1e30
GLA_HEADS = 4
GLA_DK = D_MODEL // 2
GLA_DV = D_MODEL
GLA_DK_HEAD = GLA_DK // GLA_HEADS
GLA_DV_HEAD = GLA_DV // GLA_HEADS
GLA_GATE_RANK = 16
GLA_GATE_TEMP = 16.0
GLA_SUB = 16
FLASH_HEADS = 2
N_EXPERTS = 32
TOP_K = 4
D_EXPERT = D_MODEL
SWIGLU_LIMIT = 7.0
SWIGLU_ALPHA = 1.702
MOE_ROWS = 256

LANES = 128
HEAD_PAD = 256

COL_CQ = 0
COL_CKV = 512
COL_KR = 768
COL_GA = 896
COL_GV = 1024
COL_GG = 2048
COL_GQ = 3072
COL_GK = 3584
COL_MG = 4096
D_IN_PAD = 6144

VMEM_LIMIT = 56 * 1024 * 1024


def _cparams(*sem):
    return pltpu.CompilerParams(dimension_semantics=sem, vmem_limit_bytes=VMEM_LIMIT)


def _tile(n, pref, mult=16):
    t = min(pref, n)
    t -= t % mult
    while n % t:
        t -= mult
    return t


def _rms(x, g):
    return x * lax.rsqrt(jnp.mean(x * x, axis=-1, keepdims=True) + NORM_EPS) * g


def _in_proj_kernel(x_ref, g_ref, w_ref, o_ref, h_scr):
    @pl.when(pl.program_id(1) == 0)
    def _():
        h_scr[...] = _rms(x_ref[...], g_ref[...]).astype(BF16)

    o_ref[...] = jnp.dot(h_scr[...], w_ref[...], preferred_element_type=F32)


def _in_proj(x2d, g, w_all):
    n = x2d.shape[0]
    tm = min(512, n)
    tn = 1536
    return pl.pallas_call(
        _in_proj_kernel,
        grid=(n // tm, D_IN_PAD // tn),
        in_specs=[
            pl.BlockSpec((tm, D_MODEL), lambda i, j: (i, 0)),
            pl.BlockSpec((1, D_MODEL), lambda i, j: (0, 0)),
            pl.BlockSpec((D_MODEL, tn), lambda i, j: (0, j)),
        ],
        out_specs=pl.BlockSpec((tm, tn), lambda i, j: (i, j)),
        out_shape=jax.ShapeDtypeStruct((n, D_IN_PAD), F32),
        scratch_shapes=[pltpu.VMEM((tm, D_MODEL), BF16)],
        compiler_params=_cparams("parallel", "arbitrary"),
        name="in_proj",
    )(x2d, g, w_all)


def _rope_pair(blk, cs):
    t = blk * cs
    return t + pltpu.roll(t, MLA_ROPE, axis=1)


def _mla_q_kernel(h_ref, cs_ref, qg_ref, kvg_ref, wuq_ref, q_ref, lat_ref, kr_ref):
    cs = cs_ref[...]
    qn = _rms(h_ref[:, COL_CQ:COL_CQ + Q_LORA], qg_ref[...]).astype(BF16)
    q = jnp.dot(qn, wuq_ref[...], preferred_element_type=F32)
    for hh in range(MLA_HEADS):
        c0 = hh * HEAD_PAD
        q_ref[:, c0:c0 + MLA_NOPE] = q[:, c0:c0 + MLA_NOPE].astype(BF16)
        q_ref[:, c0 + MLA_NOPE:c0 + HEAD_PAD] = _rope_pair(q[:, c0 + MLA_NOPE:c0 + HEAD_PAD], cs).astype(BF16)
    lat_ref[...] = _rms(h_ref[:, COL_CKV:COL_CKV + KV_LORA], kvg_ref[...])
    r = _rope_pair(h_ref[:, COL_KR:COL_KR + LANES], cs)
    lane = lax.broadcasted_iota(jnp.int32, r.shape, 1)
    kr_ref[...] = jnp.where(lane < MLA_ROPE, r, 0.0)


def _mla_q(hin, cs, qg, kvg, wuq):
    n = hin.shape[0]
    tm = min(512, n)
    n_cs = cs.shape[0] // tm
    return pl.pallas_call(
        _mla_q_kernel,
        grid=(n // tm,),
        in_specs=[
            pl.BlockSpec((tm, COL_GA), lambda i: (i, 0)),
            pl.BlockSpec((tm, LANES), lambda i: (i % n_cs, 0)),
            pl.BlockSpec((1, Q_LORA), lambda i: (0, 0)),
            pl.BlockSpec((1, KV_LORA), lambda i: (0, 0)),
            pl.BlockSpec((Q_LORA, MLA_HEADS * HEAD_PAD), lambda i: (0, 0)),
        ],
        out_specs=[
            pl.BlockSpec((tm, MLA_HEADS * HEAD_PAD), lambda i: (i, 0)),
            pl.BlockSpec((tm, KV_LORA), lambda i: (i, 0)),
            pl.BlockSpec((tm, LANES), lambda i: (i, 0)),
        ],
        out_shape=[
            jax.ShapeDtypeStruct((n, MLA_HEADS * HEAD_PAD), BF16),
            jax.ShapeDtypeStruct((n, KV_LORA), F32),
            jax.ShapeDtypeStruct((n, LANES), F32),
        ],
        compiler_params=_cparams("parallel"),
        name="mla_q",
    )(hin, cs, qg, kvg, wuq)


def _kv_up_kernel(lat_ref, kr_ref, w_ref, k_ref, v_ref):
    kv = jnp.dot(lat_ref[...].astype(BF16), w_ref[...], preferred_element_type=F32)
    krb = kr_ref[...].astype(BF16)
    for hh in range(MLA_HEADS):
        c0 = hh * HEAD_PAD
        k_ref[:, c0:c0 + MLA_NOPE] = kv[:, hh * MLA_NOPE:(hh + 1) * MLA_NOPE].astype(BF16)
        k_ref[:, c0 + MLA_NOPE:c0 + HEAD_PAD] = krb
    v_ref[...] = kv[:, MLA_HEADS * MLA_NOPE:].astype(BF16)


def _kv_up(lat, krp, wukv):
    n = lat.shape[0]
    tm = _tile(n, 512)
    return pl.pallas_call(
        _kv_up_kernel,
        grid=(n // tm,),
        in_specs=[
            pl.BlockSpec((tm, KV_LORA), lambda i: (i, 0)),
            pl.BlockSpec((tm, LANES), lambda i: (i, 0)),
            pl.BlockSpec((KV_LORA, MLA_HEADS * (MLA_NOPE + MLA_V)), lambda i: (0, 0)),
        ],
        out_specs=[
            pl.BlockSpec((tm, MLA_HEADS * HEAD_PAD), lambda i: (i, 0)),
            pl.BlockSpec((tm, MLA_HEADS * MLA_V), lambda i: (i, 0)),
        ],
        out_shape=[
            jax.ShapeDtypeStruct((n, MLA_HEADS * HEAD_PAD), BF16),
            jax.ShapeDtypeStruct((n, MLA_HEADS * MLA_V), BF16),
        ],
        compiler_params=_cparams("parallel"),
        name="kv_up",
    )(lat, krp, wukv)


def _flash_kernel(q_ref, k_ref, v_ref, o_ref, *, tq, tk, past_len):
    qi = pl.program_id(2)
    q_pos0 = past_len + qi * tq
    q_chunk = (q_pos0 + lax.broadcasted_iota(jnp.int32, (tq, 1), 0)) >> CHUNK_SHIFT
    n_blocks = (q_pos0 + tq + tk - 1) // tk
    n_full = ((q_pos0 >> CHUNK_SHIFT) << CHUNK_SHIFT) // tk
    qs = [q_ref[0, :, hh * HEAD_PAD:(hh + 1) * HEAD_PAD] for hh in range(FLASH_HEADS)]

    def step(j, carry, masked):
        k0 = pl.multiple_of(j * tk, tk)
        if masked:
            k_chunk = (k0 + lax.broadcasted_iota(jnp.int32, (1, tk), 1)) >> CHUNK_SHIFT
        out = []
        for hh in range(FLASH_HEADS):
            m, acc = carry[hh]
            k = k_ref[0, pl.ds(k0, tk), hh * HEAD_PAD:(hh + 1) * HEAD_PAD]
            v = v_ref[0, pl.ds(k0, tk), hh * MLA_V:(hh + 1) * MLA_V]
            s = lax.dot_general(qs[hh], k, (((1,), (1,)), ((), ())), preferred_element_type=F32)
            s = s * (MLA_SCALE * LOG2_E)
            if masked:
                s = jnp.where(k_chunk <= q_chunk, s, NEG_INF)
            m_new = jnp.maximum(m, jnp.max(s, axis=-1, keepdims=True))
            p = jnp.exp2(s - m_new).astype(BF16)
            v_ones = jnp.concatenate([v, jnp.ones_like(v)], axis=1)
            acc = jnp.exp2(m - m_new) * acc + jnp.dot(p, v_ones, preferred_element_type=F32)
            out.append((m_new, acc))
        return tuple(out)

    init = tuple((jnp.full((tq, 1), NEG_INF, F32), jnp.zeros((tq, 2 * MLA_V), F32)) for _ in range(FLASH_HEADS))
    carry = lax.fori_loop(0, n_full, functools.partial(step, masked=False), init)
    carry = lax.fori_loop(n_full, n_blocks, functools.partial(step, masked=True), carry)
    for hh in range(FLASH_HEADS):
        acc = carry[hh][1]
        o_ref[0, :, hh * MLA_V:(hh + 1) * MLA_V] = (acc[:, :MLA_V] / acc[:, MLA_V:]).astype(BF16)


def _flash(q, k, v, past_len, tq, tk):
    b, sq, _ = q.shape
    sk = k.shape[1]
    fh = FLASH_HEADS
    return pl.pallas_call(
        functools.partial(_flash_kernel, tq=tq, tk=tk, past_len=past_len),
        grid=(b, MLA_HEADS // fh, sq // tq),
        in_specs=[
            pl.BlockSpec((1, tq, fh * HEAD_PAD), lambda bi, h, i: (bi, i, h)),
            pl.BlockSpec((1, sk, fh * HEAD_PAD), lambda bi, h, i: (bi, 0, h)),
            pl.BlockSpec((1, sk, fh * MLA_V), lambda bi, h, i: (bi, 0, h)),
        ],
        out_specs=pl.BlockSpec((1, tq, fh * MLA_V), lambda bi, h, i: (bi, i, h)),
        out_shape=jax.ShapeDtypeStruct((b, sq, MLA_HEADS * MLA_V), BF16),
        compiler_params=_cparams("parallel", "parallel", "arbitrary"),
        name="flash",
    )(q, k, v)


def _cumsum_rows(x):
    n = x.shape[0]
    row = lax.broadcasted_iota(jnp.int32, x.shape, 0)
    s = 1
    while s < n:
        x = x + jnp.where(row >= s, pltpu.roll(x, s, axis=0), 0.0)
        s *= 2
    return x


def _gla_kernel(gv_ref, gg_ref, gq_ref, gk_ref, ga_ref, wa_ref, ba_ref, ng_ref, s0_ref, og_ref, sn_ref, st_scr,
                *, csize, n_chunks):
    t = pl.program_id(2)

    @pl.when(t == 0)
    def _():
        st_scr[...] = s0_ref[0, 0].T

    wa = wa_ref[...]
    ba = ba_ref[...]
    ng = ng_ref[...]
    n_sub = csize // GLA_SUB
    col_id = lax.broadcasted_iota(jnp.int32, (GLA_SUB, csize), 1)
    row_id = lax.broadcasted_iota(jnp.int32, (GLA_SUB, GLA_DK_HEAD), 0)

    def chunk(c, carry):
        r0 = pl.multiple_of(c * csize, csize)
        rows = pl.ds(r0, csize)
        q = gq_ref[rows, :] * (GLA_DK_HEAD ** -0.5)
        k = gk_ref[rows, :]
        v = gv_ref[rows, :]
        x = jnp.dot(ga_ref[rows, :].astype(BF16), wa, preferred_element_type=F32) + ba
        la = jax.nn.log_sigmoid(x) / GLA_GATE_TEMP
        b = _cumsum_rows(la)

        a_rows = []
        for i in range(n_sub):
            lo = i * GLA_SUB
            bi = b[lo:lo + GLA_SUB]
            qi = q[lo:lo + GLA_SUB]
            ki = k[lo:lo + GLA_SUB]
            if i == 0:
                a_i = jnp.zeros((GLA_SUB, csize), F32)
            else:
                ri = b[lo - 1:lo]
                qs = qi * jnp.exp(bi - ri)
                ks = k * jnp.exp(jnp.minimum(ri - b, 0.0))
                a_i = lax.dot_general(qs.astype(BF16), ks.astype(BF16), (((1,), (1,)), ((), ())),
                                      preferred_element_type=F32)
                a_i = jnp.where(col_id < lo, a_i, 0.0)
            for s in range(GLA_SUB):
                e = jnp.where(row_id >= s, jnp.exp(bi - bi[s:s + 1]), 0.0)
                col = jnp.sum(qi * ki[s:s + 1] * e, axis=-1, keepdims=True)
                a_i = jnp.where(col_id == lo + s, col, a_i)
            a_rows.append(a_i)
        a = jnp.concatenate(a_rows, axis=0)

        st = st_scr[...]
        vb = v.astype(BF16)
        o = jnp.dot(a.astype(BF16), vb, preferred_element_type=F32)
        o = o + lax.dot_general((q * jnp.exp(b)).astype(BF16), st.astype(BF16), (((1,), (1,)), ((), ())),
                                preferred_element_type=F32)
        b_end = b[csize - 1:csize]
        kd = (k * jnp.exp(b_end - b)).astype(BF16)
        st_scr[...] = jnp.exp(b_end) * st + jnp.dot(v.T.astype(BF16), kd, preferred_element_type=F32)
        on = _rms(o, ng)
        og_ref[rows, :] = (on * jax.nn.silu(gg_ref[rows, :])).astype(BF16)
        return carry

    lax.fori_loop(0, n_chunks, chunk, 0)

    @pl.when(t == pl.num_programs(2) - 1)
    def _():
        sn_ref[0, 0] = st_scr[...].T


def _gla(hin, wa, ba, ng, s0, bsz, seq):
    csize = min(CHUNK, seq)
    tt = min(512, seq)
    n_t = seq // tt
    n = bsz * seq
    kern = functools.partial(_gla_kernel, csize=csize, n_chunks=tt // csize)
    vblk = GLA_DV_HEAD
    kblk = GLA_DK_HEAD
    return pl.pallas_call(
        kern,
        grid=(bsz, GLA_HEADS, n_t),
        in_specs=[
            pl.BlockSpec((tt, vblk), lambda b, h, t: (b * n_t + t, COL_GV // vblk + h)),
            pl.BlockSpec((tt, vblk), lambda b, h, t: (b * n_t + t, COL_GG // vblk + h)),
            pl.BlockSpec((tt, kblk), lambda b, h, t: (b * n_t + t, COL_GQ // kblk + h)),
            pl.BlockSpec((tt, kblk), lambda b, h, t: (b * n_t + t, COL_GK // kblk + h)),
            pl.BlockSpec((tt, LANES), lambda b, h, t: (b * n_t + t, COL_GA // LANES)),
            pl.BlockSpec((LANES, kblk), lambda b, h, t: (0, h)),
            pl.BlockSpec((1, kblk), lambda b, h, t: (0, h)),
            pl.BlockSpec((1, vblk), lambda b, h, t: (0, 0)),
            pl.BlockSpec((1, 1, kblk, vblk), lambda b, h, t: (b, h, 0, 0)),
        ],
        out_specs=[
            pl.BlockSpec((tt, vblk), lambda b, h, t: (b * n_t + t, h)),
            pl.BlockSpec((1, 1, kblk, vblk), lambda b, h, t: (b, h, 0, 0)),
        ],
        out_shape=[
            jax.ShapeDtypeStruct((n, GLA_DV), BF16),
            jax.ShapeDtypeStruct((bsz, GLA_HEADS, kblk, vblk), F32),
        ],
        scratch_shapes=[pltpu.VMEM((vblk, kblk), F32)],
        compiler_params=_cparams("parallel", "parallel", "arbitrary"),
        name="gla",
    )(hin, hin, hin, hin, hin, wa, ba, ng, s0)


def _post_mix_kernel(at_ref, og_ref, mg1_ref, mg2_ref, x_ref, wom_ref, wog_ref, wout_ref, n2_ref, wr_ref, br_ref,
                     x1_ref, hm_ref, sel_ref, g4_ref, cnt_ref):
    y_mla = jnp.dot(at_ref[...], wom_ref[...], preferred_element_type=F32)
    y_gla = jnp.dot(og_ref[...], wog_ref[...], preferred_element_type=F32)
    m = jax.nn.sigmoid(mg1_ref[...]) * y_mla + jax.nn.sigmoid(mg2_ref[...]) * y_gla
    x1 = x_ref[...] + jnp.dot(m.astype(BF16), wout_ref[...], preferred_element_type=F32)
    x1_ref[...] = x1
    hm = _rms(x1, n2_ref[...])
    hm_ref[...] = hm
    logits = jnp.dot(hm.astype(BF16), wr_ref[...], preferred_element_type=F32) + br_ref[...]

    lane = lax.broadcasted_iota(jnp.int32, logits.shape, 1)
    work = jnp.where(lane < N_EXPERTS, logits, -jnp.inf)
    sel = jnp.zeros(logits.shape, F32)
    vals = []
    for k in range(TOP_K):
        mk = jnp.max(work, axis=-1, keepdims=True)
        ik = jnp.min(jnp.where(work == mk, lane, LANES), axis=-1, keepdims=True)
        hit = lane == ik
        sel = jnp.where(hit, float(k + 1), sel)
        work = jnp.where(hit, -jnp.inf, work)
        vals.append(mk)
    sel_ref[...] = sel
    ex = [jnp.exp(v - vals[0]) for v in vals]
    den = ex[0]
    for e in ex[1:]:
        den = den + e
    g4 = jnp.zeros(logits.shape, F32)
    for k in range(TOP_K):
        g4 = jnp.where(lane == k, ex[k] / den, g4)
    g4_ref[...] = g4[:, :TOP_K]

    @pl.when(pl.program_id(0) == 0)
    def _():
        cnt_ref[...] = jnp.zeros_like(cnt_ref)

    cnt_ref[...] += jnp.sum(jnp.where(sel > 0.0, 1.0, 0.0), axis=0, keepdims=True)


def _post_mix(attn, og, hin, x2d, wom, wog, wout, n2, wr, br):
    n = x2d.shape[0]
    tm = min(256, n)
    row = lambda i: (i, 0)
    const = lambda i: (0, 0)
    return pl.pallas_call(
        _post_mix_kernel,
        grid=(n // tm,),
        in_specs=[
            pl.BlockSpec((tm, D_MODEL), row),
            pl.BlockSpec((tm, D_MODEL), row),
            pl.BlockSpec((tm, D_MODEL), lambda i: (i, COL_MG // D_MODEL)),
            pl.BlockSpec((tm, D_MODEL), lambda i: (i, COL_MG // D_MODEL + 1)),
            pl.BlockSpec((tm, D_MODEL), row),
            pl.BlockSpec((D_MODEL, D_MODEL), const),
            pl.BlockSpec((D_MODEL, D_MODEL), const),
            pl.BlockSpec((D_MODEL, D_MODEL), const),
            pl.BlockSpec((1, D_MODEL), const),
            pl.BlockSpec((D_MODEL, LANES), const),
            pl.BlockSpec((1, LANES), const),
        ],
        out_specs=[
            pl.BlockSpec((tm, D_MODEL), row),
            pl.BlockSpec((tm, D_MODEL), row),
            pl.BlockSpec((tm, LANES), row),
            pl.BlockSpec((tm, TOP_K), row),
            pl.BlockSpec((1, LANES), const),
        ],
        out_shape=[
            jax.ShapeDtypeStruct((n, D_MODEL), F32),
            jax.ShapeDtypeStruct((n, D_MODEL), F32),
            jax.ShapeDtypeStruct((n, LANES), F32),
            jax.ShapeDtypeStruct((n, TOP_K), F32),
            jax.ShapeDtypeStruct((1, LANES), F32),
        ],
        compiler_params=_cparams("arbitrary"),
        name="post_mix",
    )(attn, og, hin, hin, x2d, wom, wog, wout, n2, wr, br)


def _dest_kernel(sel_ref, ps_ref, dest_ref, run_scr):
    @pl.when(pl.program_id(0) == 0)
    def _():
        run_scr[...] = ps_ref[...]

    sel = sel_ref[...]
    tt = sel.shape[0]
    oh = jnp.where(sel > 0.0, 1.0, 0.0)
    r = lax.broadcasted_iota(jnp.int32, (tt, tt), 0)
    c = lax.broadcasted_iota(jnp.int32, (tt, tt), 1)
    ltri = jnp.where(r > c, 1.0, 0.0).astype(BF16)
    slot = jnp.dot(ltri, oh.astype(BF16), preferred_element_type=F32) + run_scr[...]
    run_scr[...] += jnp.sum(oh, axis=0, keepdims=True)
    lane = lax.broadcasted_iota(jnp.int32, sel.shape, 1)
    d = jnp.zeros(sel.shape, F32)
    for k in range(TOP_K):
        col = jnp.sum(jnp.where(sel == float(k + 1), slot, 0.0), axis=-1, keepdims=True)
        d = jnp.where(lane == k, col, d)
    dest_ref[...] = d[:, :TOP_K].astype(jnp.int32)


def _dest(sel, pad_start):
    n = sel.shape[0]
    tt = _tile(n, 512)
    return pl.pallas_call(
        _dest_kernel,
        grid=(n // tt,),
        in_specs=[pl.BlockSpec((tt, LANES), lambda i: (i, 0)), pl.BlockSpec((1, LANES), lambda i: (0, 0))],
        out_specs=pl.BlockSpec((tt, TOP_K), lambda i: (i, 0)),
        out_shape=jax.ShapeDtypeStruct((n, TOP_K), jnp.int32),
        scratch_shapes=[pltpu.VMEM((1, LANES), F32)],
        compiler_params=_cparams("arbitrary"),
        name="route_dest",
    )(sel, pad_start)


def _row_copy(src, src_row, dst, dst_row, sem):
    return pltpu.make_async_copy(src.at[pl.ds(src_row, 1)], dst.at[pl.ds(dst_row, 1)], sem)


def _scatter_kernel(dest_ref, hm_ref, xs_in, xs_ref, sem):
    del xs_in
    tt = hm_ref.shape[0]

    def issue(t, carry):
        for k in range(TOP_K):
            _row_copy(hm_ref, t, xs_ref, dest_ref[t * TOP_K + k], sem).start()
        return carry

    lax.fori_loop(0, tt, issue, 0)

    def drain(t, carry):
        for k in range(TOP_K):
            _row_copy(hm_ref, t, xs_ref, dest_ref[t * TOP_K + k], sem).wait()
        return carry

    lax.fori_loop(0, tt, drain, 0)


def _scatter(dest_flat, hm, xs):
    n = hm.shape[0]
    tt = _tile(n, 256)
    return pl.pallas_call(
        _scatter_kernel,
        grid=(n // tt,),
        in_specs=[
            pl.BlockSpec((tt * TOP_K,), lambda i: (i,), memory_space=pltpu.SMEM),
            pl.BlockSpec((tt, D_MODEL), lambda i: (i, 0)),
            pl.BlockSpec(memory_space=pl.ANY),
        ],
        out_specs=pl.BlockSpec(memory_space=pl.ANY),
        out_shape=jax.ShapeDtypeStruct(xs.shape, xs.dtype),
        scratch_shapes=[pltpu.SemaphoreType.DMA(())],
        input_output_aliases={2: 0},
        compiler_params=_cparams("arbitrary"),
        name="moe_scatter",
    )(dest_flat, hm, xs)


def _w1_split_kernel(w_ref, g_ref, l_ref, t_scr):
    n_slab = w_ref.shape[1] // LANES
    for c in range(n_slab):
        t_scr[c] = w_ref[0, c * LANES:(c + 1) * LANES, :].T
    for c in range(n_slab):
        g_ref[0, :, c * LANES:(c + 1) * LANES] = t_scr[c, pl.ds(0, D_EXPERT, stride=2), :].astype(BF16)
        l_ref[0, :, c * LANES:(c + 1) * LANES] = t_scr[c, pl.ds(1, D_EXPERT, stride=2), :].astype(BF16)


def _w1_split(w1):
    ks = 512
    n_slab = ks // LANES
    out = jax.ShapeDtypeStruct((N_EXPERTS, D_EXPERT, D_MODEL), BF16)
    return pl.pallas_call(
        _w1_split_kernel,
        grid=(N_EXPERTS, D_MODEL // ks),
        in_specs=[pl.BlockSpec((1, ks, 2 * D_EXPERT), lambda e, j: (e, j, 0))],
        out_specs=[pl.BlockSpec((1, D_EXPERT, ks), lambda e, j: (e, 0, j)),
                   pl.BlockSpec((1, D_EXPERT, ks), lambda e, j: (e, 0, j))],
        out_shape=[out, out],
        scratch_shapes=[pltpu.VMEM((n_slab, 2 * D_EXPERT, LANES), F32)],
        compiler_params=_cparams("parallel", "parallel"),
        name="w1_split",
    )(w1)


def _expert_kernel(be_ref, nb_ref, x_ref, w1g_ref, w1l_ref, b1g_ref, b1l_ref, w2_ref, b2_ref, o_ref):
    del be_ref
    used = pl.program_id(0) < nb_ref[0]

    @pl.when(jnp.logical_not(used))
    def _():
        o_ref[...] = jnp.zeros_like(o_ref)

    @pl.when(used)
    def _():
        x = x_ref[...].astype(BF16)
        nt = (((1,), (1,)), ((), ()))
        hg = lax.dot_general(x, w1g_ref[0], nt, preferred_element_type=F32) + b1g_ref[0]
        hl = lax.dot_general(x, w1l_ref[0], nt, preferred_element_type=F32) + b1l_ref[0]
        glu = jnp.minimum(hg, SWIGLU_LIMIT)
        lin = jnp.clip(hl, -SWIGLU_LIMIT, SWIGLU_LIMIT)
        act = glu * jax.nn.sigmoid(SWIGLU_ALPHA * glu) * (lin + 1.0)
        o_ref[...] = jnp.dot(act.astype(BF16), w2_ref[0], preferred_element_type=F32) + b2_ref[0]


def _experts(blk_exp, n_used, xs, w1g_t, w1l_t, b1g, b1l, w2, b2):
    n_slot = xs.shape[0]
    n_blk = n_slot // MOE_ROWS
    row_in = lambda i, be, nb: (jnp.minimum(i, nb[0] - 1), 0)
    row = lambda i, be, nb: (i, 0)
    wsel = lambda i, be, nb: (be[i], 0, 0)
    grid_spec = pltpu.PrefetchScalarGridSpec(
        num_scalar_prefetch=2,
        grid=(n_blk,),
        in_specs=[
            pl.BlockSpec((MOE_ROWS, D_MODEL), row_in),
            pl.BlockSpec((1, D_EXPERT, D_MODEL), wsel),
            pl.BlockSpec((1, D_EXPERT, D_MODEL), wsel),
            pl.BlockSpec((1, 1, D_EXPERT), wsel),
            pl.BlockSpec((1, 1, D_EXPERT), wsel),
            pl.BlockSpec((1, D_EXPERT, D_MODEL), wsel),
            pl.BlockSpec((1, 1, D_MODEL), wsel),
        ],
        out_specs=pl.BlockSpec((MOE_ROWS, D_MODEL), row),
    )
    return pl.pallas_call(
        _expert_kernel,
        grid_spec=grid_spec,
        out_shape=jax.ShapeDtypeStruct((n_slot, D_MODEL), F32),
        compiler_params=_cparams("arbitrary"),
        name="experts",
    )(blk_exp, n_used, xs, w1g_t, w1l_t, b1g, b1l, w2, b2)


def _combine_kernel(dest_ref, x1_ref, g4_ref, nf_ref, ys_ref, o_ref, buf, sem):
    tt = x1_ref.shape[0]

    def issue(t, carry):
        for k in range(TOP_K):
            _row_copy(ys_ref, dest_ref[t * TOP_K + k], buf.at[k], t, sem).start()
        return carry

    lax.fori_loop(0, tt, issue, 0)

    def drain(t, carry):
        for k in range(TOP_K):
            _row_copy(ys_ref, dest_ref[t * TOP_K + k], buf.at[k], t, sem).wait()
        return carry

    lax.fori_loop(0, tt, drain, 0)
    g4 = g4_ref[...]
    moe = g4[:, 0:1] * buf[0]
    for k in range(1, TOP_K):
        moe = moe + g4[:, k:k + 1] * buf[k]
    o_ref[...] = _rms(x1_ref[...] + moe, nf_ref[...])


def _combine(dest_flat, x1, g4, nf, ys):
    n = x1.shape[0]
    tt = _tile(n, 256)
    row = lambda i: (i, 0)
    return pl.pallas_call(
        _combine_kernel,
        grid=(n // tt,),
        in_specs=[
            pl.BlockSpec((tt * TOP_K,), lambda i: (i,), memory_space=pltpu.SMEM),
            pl.BlockSpec((tt, D_MODEL), row),
            pl.BlockSpec((tt, TOP_K), row),
            pl.BlockSpec((1, D_MODEL), lambda i: (0, 0)),
            pl.BlockSpec(memory_space=pl.ANY),
        ],
        out_specs=pl.BlockSpec((tt, D_MODEL), row),
        out_shape=jax.ShapeDtypeStruct((n, D_MODEL), F32),
        scratch_shapes=[pltpu.VMEM((TOP_K, tt, D_MODEL), F32), pltpu.SemaphoreType.DMA(())],
        compiler_params=_cparams("arbitrary"),
        name="moe_combine",
    )(dest_flat, x1, g4, nf, ys)


def _rope_table(pos):
    half = MLA_ROPE // 2
    inv_freq = ROPE_THETA ** (-jnp.arange(half, dtype=F32) / half)
    ang = pos.astype(F32)[:, None] * inv_freq[None, :]
    cos, sin = jnp.cos(ang), jnp.sin(ang)
    return jnp.concatenate([cos, cos, -sin, sin], axis=-1)


def _regroup_w_in(w_in):
    o = np.cumsum((Q_LORA, KV_LORA, MLA_ROPE, GLA_DK, GLA_DK, GLA_DV, GLA_GATE_RANK, GLA_DV, 2 * D_MODEL))
    c_q, c_kv, k_r = w_in[:, :o[0]], w_in[:, o[0]:o[1]], w_in[:, o[1]:o[2]]
    gq, gk, gv = w_in[:, o[2]:o[3]], w_in[:, o[3]:o[4]], w_in[:, o[4]:o[5]]
    ga, gg, mg = w_in[:, o[5]:o[6]], w_in[:, o[6]:o[7]], w_in[:, o[7]:o[8]]
    half = MLA_ROPE // 2
    k_r_sw = jnp.concatenate([k_r[:, half:], k_r[:, :half]], axis=1)
    ga_pad = jnp.zeros((D_MODEL, LANES - GLA_GATE_RANK), w_in.dtype)
    return jnp.concatenate([c_q, c_kv, k_r, k_r_sw, ga, ga_pad, gv, gg, gq, gk, mg], axis=1).astype(BF16)


def _regroup_w_uq(w_uq):
    w = w_uq.reshape(Q_LORA, MLA_HEADS, MLA_NOPE + MLA_ROPE)
    half = MLA_ROPE // 2
    nope, rope = w[..., :MLA_NOPE], w[..., MLA_NOPE:]
    rope_sw = jnp.concatenate([rope[..., half:], rope[..., :half]], axis=-1)
    return jnp.concatenate([nope, rope, rope_sw], axis=-1).reshape(Q_LORA, MLA_HEADS * HEAD_PAD).astype(BF16)


def _regroup_w_ukv(w_ukv):
    w = w_ukv.reshape(KV_LORA, MLA_HEADS, MLA_NOPE + MLA_V)
    k = w[..., :MLA_NOPE].reshape(KV_LORA, MLA_HEADS * MLA_NOPE)
    v = w[..., MLA_NOPE:].reshape(KV_LORA, MLA_HEADS * MLA_V)
    return jnp.concatenate([k, v], axis=1).astype(BF16)


def _slot_plan(counts, n_blk):
    counts = counts.astype(jnp.int32)
    padded = (counts + MOE_ROWS - 1) // MOE_ROWS * MOE_ROWS
    pad_end = jnp.cumsum(padded)
    pad_start = pad_end - padded
    n_used = (pad_end[-1] // MOE_ROWS).reshape(1)
    blk_start = jnp.arange(n_blk, dtype=jnp.int32) * MOE_ROWS
    blk_exp = jnp.sum(blk_start[:, None] >= pad_end[None, :], axis=1).astype(jnp.int32)
    last_exp = blk_exp[jnp.maximum(n_used[0] - 1, 0)]
    blk_exp = jnp.where(jnp.arange(n_blk) < n_used[0], blk_exp, last_exp)
    ps_row = jnp.pad(pad_start.astype(F32), (0, LANES - N_EXPERTS))[None, :]
    return ps_row, blk_exp, n_used


def _mixers(x, past_lat, past_kr, s0, past_len, p):
    bsz, seq, _ = x.shape
    n = bsz * seq
    x2d = x.reshape(n, D_MODEL)
    hin = _in_proj(x2d, p["norm1_g"], p["w_all"])
    tm = min(512, n)
    cs = _rope_table(past_len + jnp.arange(seq))
    if seq < tm:
        cs = jnp.tile(cs, (tm // seq, 1))
    q, lat_new, krp_new = _mla_q(hin, cs, p["q_norm_g"], p["kv_norm_g"], p["w_uq"])
    if past_len:
        lat_all = jnp.concatenate([past_lat, lat_new.reshape(bsz, seq, KV_LORA)], axis=1)
        kr_pad = jnp.pad(past_kr, ((0, 0), (0, 0), (0, LANES - MLA_ROPE)))
        krp_all = jnp.concatenate([kr_pad, krp_new.reshape(bsz, seq, LANES)], axis=1)
    else:
        lat_all, krp_all = lat_new, krp_new
    sk = past_len + seq
    kcat, v = _kv_up(lat_all.reshape(bsz * sk, KV_LORA), krp_all.reshape(bsz * sk, LANES), p["w_ukv"])
    tq = min(512, seq)
    tk = 512 if sk % 512 == 0 else sk
    attn = _flash(q.reshape(bsz, seq, -1), kcat.reshape(bsz, sk, -1), v.reshape(bsz, sk, -1), past_len, tq, tk)
    og, s_new = _gla(hin, p["w_alpha"], p["b_alpha"], p["gla_norm_g"], s0, bsz, seq)
    x1, hm, sel, g4, cnt = _post_mix(attn.reshape(n, -1), og, hin, x2d, p["w_o_mla"], p["w_o_gla"], p["w_out"],
                                     p["norm2_g"], p["w_router"], p["b_router"])
    new_state = (lat_new.reshape(bsz, seq, KV_LORA), krp_new[:, :MLA_ROPE].reshape(bsz, seq, MLA_ROPE), s_new)
    return x1, hm, sel, g4, cnt, new_state


def kernel(x_prompt, x_sample, cache_mla_latent, cache_mla_krope, state_gla, norm1_g, w_in, q_norm_g, kv_norm_g,
           w_uq, w_ukv, w_o_mla, w_alpha2, b_alpha, gla_norm_g, w_o_gla, w_out, norm2_g, w_router, b_router,
           w1, b1, w2, b2, normf_g):
    depth = w_in.shape[0]
    assert depth == 1
    l = 0
    bp, sp, _ = x_prompt.shape
    bs, ss, _ = x_sample.shape
    past_len = cache_mla_latent.shape[2]
    p = {
        "norm1_g": norm1_g[l][None, :],
        "w_all": _regroup_w_in(w_in[l]),
        "q_norm_g": q_norm_g[l][None, :],
        "kv_norm_g": kv_norm_g[l][None, :],
        "w_uq": _regroup_w_uq(w_uq[l]),
        "w_ukv": _regroup_w_ukv(w_ukv[l]),
        "w_o_mla": w_o_mla[l].astype(BF16),
        "w_alpha": jnp.pad(w_alpha2[l], ((0, LANES - GLA_GATE_RANK), (0, 0))).astype(BF16),
        "b_alpha": b_alpha[l][None, :],
        "gla_norm_g": gla_norm_g[l][None, :],
        "w_o_gla": w_o_gla[l].astype(BF16),
        "w_out": w_out[l].astype(BF16),
        "norm2_g": norm2_g[l][None, :],
        "w_router": jnp.pad(w_router[l], ((0, 0), (0, LANES - N_EXPERTS))).astype(BF16),
        "b_router": jnp.pad(b_router[l], (0, LANES - N_EXPERTS))[None, :],
    }
    zeros_state = jnp.zeros((bp, GLA_HEADS, GLA_DK_HEAD, GLA_DV_HEAD), F32)
    x1p, hmp, selp, g4p, cntp, (lat_p, kr_p, st_p) = _mixers(x_prompt, None, None, zeros_state, 0, p)
    x1s, hms, sels, g4s, cnts, (lat_s, kr_s, st_s) = _mixers(x_sample, cache_mla_latent[l], cache_mla_krope[l],
                                                              state_gla[l], past_len, p)

    n_p, n_s = bp * sp, bs * ss
    n_asg = (n_p + n_s) * TOP_K
    n_slot = -(-n_asg // MOE_ROWS) * MOE_ROWS + N_EXPERTS * MOE_ROWS
    ps_row, blk_exp, n_used = _slot_plan((cntp + cnts)[0, :N_EXPERTS], n_slot // MOE_ROWS)
    dest = _dest(jnp.concatenate([selp, sels], axis=0), ps_row).reshape(n_asg)
    dest_p, dest_s = dest[:n_p * TOP_K], dest[n_p * TOP_K:]
    xs = jnp.zeros((n_slot, D_MODEL), F32)
    xs = _scatter(dest_p, hmp, xs)
    xs = _scatter(dest_s, hms, xs)
    w1g_t, w1l_t = _w1_split(w1[l])
    b1g = b1[l][:, None, 0::2]
    b1lin = b1[l][:, None, 1::2]
    ys_slots = _experts(blk_exp, n_used, xs, w1g_t, w1l_t, b1g, b1lin, w2[l].astype(BF16), b2[l][:, None, :])
    yp = _combine(dest_p, x1p, g4p, normf_g[None, :], ys_slots).reshape(bp, sp, D_MODEL)
    ys = _combine(dest_s, x1s, g4s, normf_g[None, :], ys_slots).reshape(bs, ss, D_MODEL)
    return (yp, ys, lat_p[None], kr_p[None], st_p[None], lat_s[None], kr_s[None], st_s[None])
```

```python
import functools

import jax
import jax.numpy as jnp
import numpy as np
from jax import lax
from jax.experimental import pallas as pl
from jax.experimental.pallas import tpu as pltpu

F32 = jnp.float32
BF16 = jnp.bfloat16

D_MODEL = 1024
CHUNK = 64
CHUNK_SHIFT = 6
LOG2_E = 1.4426950408889634
NORM_EPS = 1e-6
MLA_HEADS = 8
MLA_NOPE = 128
MLA_ROPE = 64
MLA_V = 128
Q_LORA = 512
KV_LORA = 256
ROPE_THETA = 10000.0
MLA_SCALE = (MLA_NOPE + MLA_ROPE) ** -0.5
NEG_INF = -1e30
GLA_HEADS = 4
GLA_DK = D_MODEL // 2
GLA_DV = D_MODEL
GLA_DK_HEAD = GLA_DK // GLA_HEADS
GLA_DV_HEAD = GLA_DV // GLA_HEADS
GLA_GATE_RANK = 16
GLA_GATE_TEMP = 16.0
GLA_SUB = 16
FLASH_HEADS = 2
N_EXPERTS = 32
TOP_K = 4
D_EXPERT = D_MODEL
SWIGLU_LIMIT = 7.0
SWIGLU_ALPHA = 1.702
MOE_ROWS = 512

LANES = 128
HEAD_PAD = 256

COL_CQ = 0
COL_CKV = 512
COL_KR = 768
COL_GA = 896
COL_GV = 1024
COL_GG = 2048
COL_GQ = 3072
COL_GK = 3584
COL_MG = 4096
D_IN_PAD = 6144

VMEM_LIMIT = 56 * 1024 * 1024


def _cparams(*sem, flags=None):
    return pltpu.CompilerParams(dimension_semantics=sem, vmem_limit_bytes=VMEM_LIMIT, flags=flags)


def _tile(n, pref, mult=16):
    t = min(pref, n)
    t -= t % mult
    while n % t:
        t -= mult
    return t


def _rms(x, g):
    return x * lax.rsqrt(jnp.mean(x * x, axis=-1, keepdims=True) + NORM_EPS) * g


def _in_proj_kernel(x_ref, g_ref, w_ref, o_ref, h_scr):
    @pl.when(pl.program_id(1) == 0)
    def _():
        h_scr[...] = _rms(x_ref[...], g_ref[...]).astype(BF16)

    o_ref[...] = jnp.dot(h_scr[...], w_ref[...], preferred_element_type=F32)


def _in_proj(x2d, g, w_all):
    n = x2d.shape[0]
    tm = min(512, n)
    tn = 1536
    return pl.pallas_call(
        _in_proj_kernel,
        grid=(n // tm, D_IN_PAD // tn),
        in_specs=[
            pl.BlockSpec((tm, D_MODEL), lambda i, j: (i, 0)),
            pl.BlockSpec((1, D_MODEL), lambda i, j: (0, 0)),
            pl.BlockSpec((D_MODEL, tn), lambda i, j: (0, j)),
        ],
        out_specs=pl.BlockSpec((tm, tn), lambda i, j: (i, j)),
        out_shape=jax.ShapeDtypeStruct((n, D_IN_PAD), F32),
        scratch_shapes=[pltpu.VMEM((tm, D_MODEL), BF16)],
        compiler_params=_cparams("parallel", "arbitrary"),
        name="in_proj",
    )(x2d, g, w_all)


def _rope_pair(blk, cs):
    t = blk * cs
    return t + pltpu.roll(t, MLA_ROPE, axis=1)


def _mla_q_kernel(h_ref, cs_ref, qg_ref, kvg_ref, wuq_ref, q_ref, lat_ref, kr_ref):
    cs = cs_ref[...]
    qn = _rms(h_ref[:, COL_CQ:COL_CQ + Q_LORA], qg_ref[...]).astype(BF16)
    q = jnp.dot(qn, wuq_ref[...], preferred_element_type=F32)
    for hh in range(MLA_HEADS):
        c0 = hh * HEAD_PAD
        q_ref[:, c0:c0 + MLA_NOPE] = q[:, c0:c0 + MLA_NOPE].astype(BF16)
        q_ref[:, c0 + MLA_NOPE:c0 + HEAD_PAD] = _rope_pair(q[:, c0 + MLA_NOPE:c0 + HEAD_PAD], cs).astype(BF16)
    lat_ref[...] = _rms(h_ref[:, COL_CKV:COL_CKV + KV_LORA], kvg_ref[...])
    r = _rope_pair(h_ref[:, COL_KR:COL_KR + LANES], cs)
    lane = lax.broadcasted_iota(jnp.int32, r.shape, 1)
    kr_ref[...] = jnp.where(lane < MLA_ROPE, r, 0.0)


def _mla_q(hin, cs, qg, kvg, wuq):
    n = hin.shape[0]
    tm = min(512, n)
    n_cs = cs.shape[0] // tm
    return pl.pallas_call(
        _mla_q_kernel,
        grid=(n // tm,),
        in_specs=[
            pl.BlockSpec((tm, COL_GA), lambda i: (i, 0)),
            pl.BlockSpec((tm, LANES), lambda i: (i % n_cs, 0)),
            pl.BlockSpec((1, Q_LORA), lambda i: (0, 0)),
            pl.BlockSpec((1, KV_LORA), lambda i: (0, 0)),
            pl.BlockSpec((Q_LORA, MLA_HEADS * HEAD_PAD), lambda i: (0, 0)),
        ],
        out_specs=[
            pl.BlockSpec((tm, MLA_HEADS * HEAD_PAD), lambda i: (i, 0)),
            pl.BlockSpec((tm, KV_LORA), lambda i: (i, 0)),
            pl.BlockSpec((tm, LANES), lambda i: (i, 0)),
        ],
        out_shape=[
            jax.ShapeDtypeStruct((n, MLA_HEADS * HEAD_PAD), BF16),
            jax.ShapeDtypeStruct((n, KV_LORA), F32),
            jax.ShapeDtypeStruct((n, LANES), F32),
        ],
        compiler_params=_cparams("parallel"),
        name="mla_q",
    )(hin, cs, qg, kvg, wuq)


def _kv_up_kernel(lat_ref, kr_ref, w_ref, k_ref, v_ref):
    kv = jnp.dot(lat_ref[...].astype(BF16), w_ref[...], preferred_element_type=F32)
    krb = kr_ref[...].astype(BF16)
    for hh in range(MLA_HEADS):
        c0 = hh * HEAD_PAD
        k_ref[:, c0:c0 + MLA_NOPE] = kv[:, hh * MLA_NOPE:(hh + 1) * MLA_NOPE].astype(BF16)
        k_ref[:, c0 + MLA_NOPE:c0 + HEAD_PAD] = krb
    v_ref[...] = kv[:, MLA_HEADS * MLA_NOPE:].astype(BF16)


def _kv_up(lat, krp, wukv):
    n = lat.shape[0]
    tm = _tile(n, 512)
    return pl.pallas_call(
        _kv_up_kernel,
        grid=(n // tm,),
        in_specs=[
            pl.BlockSpec((tm, KV_LORA), lambda i: (i, 0)),
            pl.BlockSpec((tm, LANES), lambda i: (i, 0)),
            pl.BlockSpec((KV_LORA, MLA_HEADS * (MLA_NOPE + MLA_V)), lambda i: (0, 0)),
        ],
        out_specs=[
            pl.BlockSpec((tm, MLA_HEADS * HEAD_PAD), lambda i: (i, 0)),
            pl.BlockSpec((tm, MLA_HEADS * MLA_V), lambda i: (i, 0)),
        ],
        out_shape=[
            jax.ShapeDtypeStruct((n, MLA_HEADS * HEAD_PAD), BF16),
            jax.ShapeDtypeStruct((n, MLA_HEADS * MLA_V), BF16),
        ],
        compiler_params=_cparams("parallel"),
        name="kv_up",
    )(lat, krp, wukv)


def _flash_kernel(q_ref, k_ref, v_ref, o_ref, *, tq, tk, past_len):
    qi = pl.program_id(2)
    q_pos0 = past_len + qi * tq
    q_chunk = (q_pos0 + lax.broadcasted_iota(jnp.int32, (tq, 1), 0)) >> CHUNK_SHIFT
    n_blocks = (q_pos0 + tq + tk - 1) // tk
    n_full = ((q_pos0 >> CHUNK_SHIFT) << CHUNK_SHIFT) // tk
    qs = [q_ref[0, :, hh * HEAD_PAD:(hh + 1) * HEAD_PAD] for hh in range(FLASH_HEADS)]

    def step(j, carry, masked):
        k0 = pl.multiple_of(j * tk, tk)
        if masked:
            k_chunk = (k0 + lax.broadcasted_iota(jnp.int32, (1, tk), 1)) >> CHUNK_SHIFT
        out = []
        for hh in range(FLASH_HEADS):
            m, l, acc = carry[hh]
            k = k_ref[0, pl.ds(k0, tk), hh * HEAD_PAD:(hh + 1) * HEAD_PAD]
            v = v_ref[0, pl.ds(k0, tk), hh * MLA_V:(hh + 1) * MLA_V]
            s = lax.dot_general(qs[hh], k, (((1,), (1,)), ((), ())), preferred_element_type=F32)
            s = s * (MLA_SCALE * LOG2_E)
            if masked:
                s = jnp.where(k_chunk <= q_chunk, s, NEG_INF)
            m_new = jnp.maximum(m, jnp.max(s, axis=-1, keepdims=True))
            alpha = jnp.exp2(m - m_new)
            p = jnp.exp2(s - m_new)
            l = alpha * l + jnp.sum(p, axis=-1, keepdims=True)
            acc = alpha * acc + jnp.dot(p.astype(BF16), v, preferred_element_type=F32)
            out.append((m_new, l, acc))
        return tuple(out)

    init = tuple((jnp.full((tq, 1), NEG_INF, F32), jnp.zeros((tq, 1), F32), jnp.zeros((tq, MLA_V), F32))
                 for _ in range(FLASH_HEADS))
    carry = lax.fori_loop(0, n_full, functools.partial(step, masked=False), init)
    carry = lax.fori_loop(n_full, n_blocks, functools.partial(step, masked=True), carry)
    for hh in range(FLASH_HEADS):
        _, l, acc = carry[hh]
        o_ref[0, :, hh * MLA_V:(hh + 1) * MLA_V] = (acc / l).astype(BF16)


def _flash(q, k, v, past_len, tq, tk):
    b, sq, _ = q.shape
    sk = k.shape[1]
    fh = FLASH_HEADS
    return pl.pallas_call(
        functools.partial(_flash_kernel, tq=tq, tk=tk, past_len=past_len),
        grid=(b, MLA_HEADS // fh, sq // tq),
        in_specs=[
            pl.BlockSpec((1, tq, fh * HEAD_PAD), lambda bi, h, i: (bi, i, h)),
            pl.BlockSpec((1, sk, fh * HEAD_PAD), lambda bi, h, i: (bi, 0, h)),
            pl.BlockSpec((1, sk, fh * MLA_V), lambda bi, h, i: (bi, 0, h)),
        ],
        out_specs=pl.BlockSpec((1, tq, fh * MLA_V), lambda bi, h, i: (bi, i, h)),
        out_shape=jax.ShapeDtypeStruct((b, sq, MLA_HEADS * MLA_V), BF16),
        compiler_params=_cparams("parallel", "parallel", "arbitrary"),
        name="flash",
    )(q, k, v)


def _cumsum_rows(x):
    n = x.shape[0]
    row = lax.broadcasted_iota(jnp.int32, x.shape, 0)
    s = 1
    while s < n:
        x = x + jnp.where(row >= s, pltpu.roll(x, s, axis=0), 0.0)
        s *= 2
    return x


def _gla_kernel(gv_ref, gg_ref, gq_ref, gk_ref, ga_ref, wa_ref, ba_ref, ng_ref, s0_ref, og_ref, sn_ref, st_scr,
                *, csize, n_chunks):
    t = pl.program_id(1)

    @pl.when(t == 0)
    def _():
        for hh in range(GLA_HEADS):
            st_scr[hh] = s0_ref[0, hh].T

    wa = wa_ref[...]
    ba = ba_ref[...]
    ng = ng_ref[...]
    n_sub = csize // GLA_SUB
    col_id = lax.broadcasted_iota(jnp.int32, (GLA_SUB, csize), 1)
    row_id = lax.broadcasted_iota(jnp.int32, (GLA_SUB, GLA_DK_HEAD), 0)

    def chunk(c, carry):
        r0 = pl.multiple_of(c * csize, csize)
        rows = pl.ds(r0, csize)
        x = jnp.dot(ga_ref[rows, :].astype(BF16), wa, preferred_element_type=F32) + ba
        la = jax.nn.log_sigmoid(x) / GLA_GATE_TEMP
        b_all = _cumsum_rows(la)
        for hh in range(GLA_HEADS):
            head(hh, rows, b_all[:, hh * GLA_DK_HEAD:(hh + 1) * GLA_DK_HEAD])
        return carry

    def head(hh, rows, b):
        kcols = slice(hh * GLA_DK_HEAD, (hh + 1) * GLA_DK_HEAD)
        vcols = slice(hh * GLA_DV_HEAD, (hh + 1) * GLA_DV_HEAD)
        q = gq_ref[rows, kcols] * (GLA_DK_HEAD ** -0.5)
        k = gk_ref[rows, kcols]
        v = gv_ref[rows, vcols]

        a_rows = []
        for i in range(n_sub):
            lo = i * GLA_SUB
            bi = b[lo:lo + GLA_SUB]
            qi = q[lo:lo + GLA_SUB]
            ki = k[lo:lo + GLA_SUB]
            if i == 0:
                a_i = jnp.zeros((GLA_SUB, csize), F32)
            else:
                ri = b[lo - 1:lo]
                qs = qi * jnp.exp(bi - ri)
                ks = k * jnp.exp(jnp.minimum(ri - b, 0.0))
                a_i = lax.dot_general(qs.astype(BF16), ks.astype(BF16), (((1,), (1,)), ((), ())),
                                      preferred_element_type=F32)
                a_i = jnp.where(col_id < lo, a_i, 0.0)
            for s in range(GLA_SUB):
                e = jnp.where(row_id >= s, jnp.exp(bi - bi[s:s + 1]), 0.0)
                col = jnp.sum(qi * ki[s:s + 1] * e, axis=-1, keepdims=True)
                a_i = jnp.where(col_id == lo + s, col, a_i)
            a_rows.append(a_i)
        a = jnp.concatenate(a_rows, axis=0)

        st = st_scr[hh]
        vb = v.astype(BF16)
        o = jnp.dot(a.astype(BF16), vb, preferred_element_type=F32)
        o = o + lax.dot_general((q * jnp.exp(b)).astype(BF16), st.astype(BF16), (((1,), (1,)), ((), ())),
                                preferred_element_type=F32)
        b_end = b[csize - 1:csize]
        kd = (k * jnp.exp(b_end - b)).astype(BF16)
        st_scr[hh] = jnp.exp(b_end) * st + jnp.dot(v.T.astype(BF16), kd, preferred_element_type=F32)
        on = _rms(o, ng)
        og_ref[rows, vcols] = (on * jax.nn.silu(gg_ref[rows, vcols])).astype(BF16)

    lax.fori_loop(0, n_chunks, chunk, 0)

    @pl.when(t == pl.num_programs(1) - 1)
    def _():
        for hh in range(GLA_HEADS):
            sn_ref[0, hh] = st_scr[hh].T


def _gla(hin, wa, ba, ng, s0, bsz, seq):
    csize = min(CHUNK, seq)
    tt = min(512, seq)
    n_t = seq // tt
    n = bsz * seq
    kern = functools.partial(_gla_kernel, csize=csize, n_chunks=tt // csize)
    state = pl.BlockSpec((1, GLA_HEADS, GLA_DK_HEAD, GLA_DV_HEAD), lambda b, t: (b, 0, 0, 0))
    return pl.pallas_call(
        kern,
        grid=(bsz, n_t),
        in_specs=[
            pl.BlockSpec((tt, GLA_DV), lambda b, t: (b * n_t + t, COL_GV // GLA_DV)),
            pl.BlockSpec((tt, GLA_DV), lambda b, t: (b * n_t + t, COL_GG // GLA_DV)),
            pl.BlockSpec((tt, GLA_DK), lambda b, t: (b * n_t + t, COL_GQ // GLA_DK)),
            pl.BlockSpec((tt, GLA_DK), lambda b, t: (b * n_t + t, COL_GK // GLA_DK)),
            pl.BlockSpec((tt, LANES), lambda b, t: (b * n_t + t, COL_GA // LANES)),
            pl.BlockSpec((LANES, GLA_DK), lambda b, t: (0, 0)),
            pl.BlockSpec((1, GLA_DK), lambda b, t: (0, 0)),
            pl.BlockSpec((1, GLA_DV_HEAD), lambda b, t: (0, 0)),
            state,
        ],
        out_specs=[pl.BlockSpec((tt, GLA_DV), lambda b, t: (b * n_t + t, 0)), state],
        out_shape=[
            jax.ShapeDtypeStruct((n, GLA_DV), BF16),
            jax.ShapeDtypeStruct((bsz, GLA_HEADS, GLA_DK_HEAD, GLA_DV_HEAD), F32),
        ],
        scratch_shapes=[pltpu.VMEM((GLA_HEADS, GLA_DV_HEAD, GLA_DK_HEAD), F32)],
        compiler_params=_cparams("parallel", "arbitrary"),
        name="gla",
    )(hin, hin, hin, hin, hin, wa, ba, ng, s0)


def _post_mix_kernel(at_ref, og_ref, mg1_ref, mg2_ref, x_ref, wom_ref, wog_ref, wout_ref, n2_ref, wr_ref, br_ref,
                     x1_ref, hm_ref, sel_ref, g4_ref, cnt_ref):
    y_mla = jnp.dot(at_ref[...], wom_ref[...], preferred_element_type=F32)
    y_gla = jnp.dot(og_ref[...], wog_ref[...], preferred_element_type=F32)
    m = jax.nn.sigmoid(mg1_ref[...]) * y_mla + jax.nn.sigmoid(mg2_ref[...]) * y_gla
    x1 = x_ref[...] + jnp.dot(m.astype(BF16), wout_ref[...], preferred_element_type=F32)
    x1_ref[...] = x1
    hm = _rms(x1, n2_ref[...])
    hm_ref[...] = hm
    logits = jnp.dot(hm.astype(BF16), wr_ref[...], preferred_element_type=F32) + br_ref[...]

    lane = lax.broadcasted_iota(jnp.int32, logits.shape, 1)
    work = jnp.where(lane < N_EXPERTS, logits, -jnp.inf)
    sel = jnp.zeros(logits.shape, F32)
    vals = []
    for k in range(TOP_K):
        mk = jnp.max(work, axis=-1, keepdims=True)
        ik = jnp.min(jnp.where(work == mk, lane, LANES), axis=-1, keepdims=True)
        hit = lane == ik
        sel = jnp.where(hit, float(k + 1), sel)
        work = jnp.where(hit, -jnp.inf, work)
        vals.append(mk)
    sel_ref[...] = sel
    ex = [jnp.exp(v - vals[0]) for v in vals]
    den = ex[0]
    for e in ex[1:]:
        den = den + e
    g4 = jnp.zeros(logits.shape, F32)
    for k in range(TOP_K):
        g4 = jnp.where(lane == k, ex[k] / den, g4)
    g4_ref[...] = g4[:, :TOP_K]

    @pl.when(pl.program_id(0) == 0)
    def _():
        cnt_ref[...] = jnp.zeros_like(cnt_ref)

    cnt_ref[...] += jnp.sum(jnp.where(sel > 0.0, 1.0, 0.0), axis=0, keepdims=True)


def _post_mix(attn, og, hin, x2d, wom, wog, wout, n2, wr, br):
    n = x2d.shape[0]
    tm = min(256, n)
    row = lambda i: (i, 0)
    const = lambda i: (0, 0)
    return pl.pallas_call(
        _post_mix_kernel,
        grid=(n // tm,),
        in_specs=[
            pl.BlockSpec((tm, D_MODEL), row),
            pl.BlockSpec((tm, D_MODEL), row),
            pl.BlockSpec((tm, D_MODEL), lambda i: (i, COL_MG // D_MODEL)),
            pl.BlockSpec((tm, D_MODEL), lambda i: (i, COL_MG // D_MODEL + 1)),
            pl.BlockSpec((tm, D_MODEL), row),
            pl.BlockSpec((D_MODEL, D_MODEL), const),
            pl.BlockSpec((D_MODEL, D_MODEL), const),
            pl.BlockSpec((D_MODEL, D_MODEL), const),
            pl.BlockSpec((1, D_MODEL), const),
            pl.BlockSpec((D_MODEL, LANES), const),
            pl.BlockSpec((1, LANES), const),
        ],
        out_specs=[
            pl.BlockSpec((tm, D_MODEL), row),
            pl.BlockSpec((tm, D_MODEL), row),
            pl.BlockSpec((tm, LANES), row),
            pl.BlockSpec((tm, TOP_K), row),
            pl.BlockSpec((1, LANES), const),
        ],
        out_shape=[
            jax.ShapeDtypeStruct((n, D_MODEL), F32),
            jax.ShapeDtypeStruct((n, D_MODEL), F32),
            jax.ShapeDtypeStruct((n, LANES), F32),
            jax.ShapeDtypeStruct((n, TOP_K), F32),
            jax.ShapeDtypeStruct((1, LANES), F32),
        ],
        compiler_params=_cparams("arbitrary"),
        name="post_mix",
    )(attn, og, hin, hin, x2d, wom, wog, wout, n2, wr, br)


def _dest_kernel(sel_ref, ps_ref, dest_ref, run_scr):
    @pl.when(pl.program_id(0) == 0)
    def _():
        run_scr[...] = ps_ref[...]

    sel = sel_ref[...]
    tt = sel.shape[0]
    oh = jnp.where(sel > 0.0, 1.0, 0.0)
    r = lax.broadcasted_iota(jnp.int32, (tt, tt), 0)
    c = lax.broadcasted_iota(jnp.int32, (tt, tt), 1)
    ltri = jnp.where(r > c, 1.0, 0.0).astype(BF16)
    slot = jnp.dot(ltri, oh.astype(BF16), preferred_element_type=F32) + run_scr[...]
    run_scr[...] += jnp.sum(oh, axis=0, keepdims=True)
    lane = lax.broadcasted_iota(jnp.int32, sel.shape, 1)
    d = jnp.zeros(sel.shape, F32)
    for k in range(TOP_K):
        col = jnp.sum(jnp.where(sel == float(k + 1), slot, 0.0), axis=-1, keepdims=True)
        d = jnp.where(lane == k, col, d)
    dest_ref[...] = d[:, :TOP_K].astype(jnp.int32)


def _dest(sel, pad_start):
    n = sel.shape[0]
    tt = _tile(n, 512)
    return pl.pallas_call(
        _dest_kernel,
        grid=(n // tt,),
        in_specs=[pl.BlockSpec((tt, LANES), lambda i: (i, 0)), pl.BlockSpec((1, LANES), lambda i: (0, 0))],
        out_specs=pl.BlockSpec((tt, TOP_K), lambda i: (i, 0)),
        out_shape=jax.ShapeDtypeStruct((n, TOP_K), jnp.int32),
        scratch_shapes=[pltpu.VMEM((1, LANES), F32)],
        compiler_params=_cparams("arbitrary"),
        name="route_dest",
    )(sel, pad_start)


def _row_copy(src, src_row, dst, dst_row, sem):
    return pltpu.make_async_copy(src.at[pl.ds(src_row, 1)], dst.at[pl.ds(dst_row, 1)], sem)


def _scatter_rows(dest_ref, hm_ref, xs_ref, sem):
    tt = hm_ref.shape[0]

    def issue(t, carry):
        for k in range(TOP_K):
            _row_copy(hm_ref, t, xs_ref, dest_ref[t * TOP_K + k], sem).start(priority=k % 2)
        return carry

    lax.fori_loop(0, tt, issue, 0)
    for k in range(TOP_K):
        pltpu.make_async_copy(hm_ref, xs_ref.at[pl.ds(0, tt)], sem).wait()


def _scatter_init_kernel(zf_ref, zl_ref, dest_ref, hm_ref, xs_ref, zero_scr, sem, zsem):
    @pl.when(pl.program_id(0) == 0)
    def _():
        zero_scr[...] = jnp.zeros_like(zero_scr)

        def zero_block(i):
            return pltpu.make_async_copy(zero_scr, xs_ref.at[pl.ds(pl.multiple_of(i * MOE_ROWS, MOE_ROWS), MOE_ROWS)],
                                         zsem)

        def start_range(e, carry):
            return lax.fori_loop(zf_ref[e], zl_ref[e], lambda i, c: (zero_block(i).start(), c)[1], carry)

        def wait_range(e, carry):
            return lax.fori_loop(zf_ref[e], zl_ref[e], lambda i, c: (zero_block(i).wait(), c)[1], carry)

        lax.fori_loop(0, N_EXPERTS + 1, start_range, 0)
        lax.fori_loop(0, N_EXPERTS + 1, wait_range, 0)

    _scatter_rows(dest_ref, hm_ref, xs_ref, sem)


def _scatter_more_kernel(dest_ref, hm_ref, xs_in, xs_ref, sem):
    del xs_in
    _scatter_rows(dest_ref, hm_ref, xs_ref, sem)


def _scatter_init(zero_first, zero_last, dest_flat, hm, n_slot):
    n = hm.shape[0]
    tt = _tile(n, 256)
    grid_spec = pltpu.PrefetchScalarGridSpec(
        num_scalar_prefetch=2,
        grid=(n // tt,),
        in_specs=[
            pl.BlockSpec((tt * TOP_K,), lambda i, pe, nb: (i,), memory_space=pltpu.SMEM),
            pl.BlockSpec((tt, D_MODEL), lambda i, pe, nb: (i, 0)),
        ],
        out_specs=pl.BlockSpec(memory_space=pl.ANY),
        scratch_shapes=[pltpu.VMEM((MOE_ROWS, D_MODEL), F32), pltpu.SemaphoreType.DMA(()),
                        pltpu.SemaphoreType.DMA(())],
    )
    return pl.pallas_call(
        _scatter_init_kernel,
        grid_spec=grid_spec,
        out_shape=jax.ShapeDtypeStruct((n_slot, D_MODEL), F32),
        compiler_params=_cparams("arbitrary"),
        name="moe_scatter_init",
    )(zero_first, zero_last, dest_flat, hm)


def _scatter_more(dest_flat, hm, xs):
    n = hm.shape[0]
    tt = _tile(n, 256)
    return pl.pallas_call(
        _scatter_more_kernel,
        grid=(n // tt,),
        in_specs=[
            pl.BlockSpec((tt * TOP_K,), lambda i: (i,), memory_space=pltpu.SMEM),
            pl.BlockSpec((tt, D_MODEL), lambda i: (i, 0)),
            pl.BlockSpec(memory_space=pl.ANY),
        ],
        out_specs=pl.BlockSpec(memory_space=pl.ANY),
        out_shape=jax.ShapeDtypeStruct(xs.shape, xs.dtype),
        scratch_shapes=[pltpu.SemaphoreType.DMA(())],
        input_output_aliases={2: 0},
        compiler_params=_cparams("arbitrary"),
        name="moe_scatter",
    )(dest_flat, hm, xs)


def _w1_split_kernel(w_ref, g_ref, l_ref, t_scr):
    n_slab = w_ref.shape[1] // LANES
    for c in range(n_slab):
        t_scr[c] = w_ref[0, c * LANES:(c + 1) * LANES, :].T
    for c in range(n_slab):
        g_ref[0, :, c * LANES:(c + 1) * LANES] = t_scr[c, pl.ds(0, D_EXPERT, stride=2), :].astype(BF16)
        l_ref[0, :, c * LANES:(c + 1) * LANES] = t_scr[c, pl.ds(1, D_EXPERT, stride=2), :].astype(BF16)


def _w1_split(w1):
    ks = 512
    n_slab = ks // LANES
    out = jax.ShapeDtypeStruct((N_EXPERTS, D_EXPERT, D_MODEL), BF16)
    return pl.pallas_call(
        _w1_split_kernel,
        grid=(N_EXPERTS, D_MODEL // ks),
        in_specs=[pl.BlockSpec((1, ks, 2 * D_EXPERT), lambda e, j: (e, j, 0))],
        out_specs=[pl.BlockSpec((1, D_EXPERT, ks), lambda e, j: (e, 0, j)),
                   pl.BlockSpec((1, D_EXPERT, ks), lambda e, j: (e, 0, j))],
        out_shape=[out, out],
        scratch_shapes=[pltpu.VMEM((n_slab, 2 * D_EXPERT, LANES), F32)],
        compiler_params=_cparams("parallel", "parallel"),
        name="w1_split",
    )(w1)


def _expert_kernel(be_ref, nb_ref, x_ref, w1g_ref, w1l_ref, b1g_ref, b1l_ref, w2_ref, b2_ref, o_ref):
    del be_ref
    used = pl.program_id(0) < nb_ref[0]

    @pl.when(jnp.logical_not(used))
    def _():
        o_ref[...] = jnp.zeros_like(o_ref)

    @pl.when(used)
    def _():
        x = x_ref[...].astype(BF16)
        nt = (((1,), (1,)), ((), ()))
        hg = lax.dot_general(x, w1g_ref[0], nt, preferred_element_type=F32) + b1g_ref[0]
        hl = lax.dot_general(x, w1l_ref[0], nt, preferred_element_type=F32) + b1l_ref[0]
        glu = jnp.minimum(hg, SWIGLU_LIMIT)
        lin = jnp.clip(hl, -SWIGLU_LIMIT, SWIGLU_LIMIT)
        act = glu * jax.nn.sigmoid(SWIGLU_ALPHA * glu) * (lin + 1.0)
        o_ref[...] = jnp.dot(act.astype(BF16), w2_ref[0], preferred_element_type=F32) + b2_ref[0]


def _experts(blk_exp, n_used, xs, w1g_t, w1l_t, b1g, b1l, w2, b2):
    n_slot = xs.shape[0]
    n_blk = n_slot // MOE_ROWS
    row_in = lambda i, be, nb: (jnp.minimum(i, nb[0] - 1), 0)
    row = lambda i, be, nb: (i, 0)
    wsel = lambda i, be, nb: (be[i], 0, 0)
    grid_spec = pltpu.PrefetchScalarGridSpec(
        num_scalar_prefetch=2,
        grid=(n_blk,),
        in_specs=[
            pl.BlockSpec((MOE_ROWS, D_MODEL), row_in),
            pl.BlockSpec((1, D_EXPERT, D_MODEL), wsel),
            pl.BlockSpec((1, D_EXPERT, D_MODEL), wsel),
            pl.BlockSpec((1, 1, D_EXPERT), wsel),
            pl.BlockSpec((1, 1, D_EXPERT), wsel),
            pl.BlockSpec((1, D_EXPERT, D_MODEL), wsel),
            pl.BlockSpec((1, 1, D_MODEL), wsel),
        ],
        out_specs=pl.BlockSpec((MOE_ROWS, D_MODEL), row),
    )
    return pl.pallas_call(
        _expert_kernel,
        grid_spec=grid_spec,
        out_shape=jax.ShapeDtypeStruct((n_slot, D_MODEL), F32),
        compiler_params=_cparams("arbitrary"),
        name="experts",
    )(blk_exp, n_used, xs, w1g_t, w1l_t, b1g, b1l, w2, b2)


def _combine_kernel(dest_ref, x1_ref, g4_ref, nf_ref, ys_ref, o_ref, buf, sem):
    tt = x1_ref.shape[0]

    def issue(t, carry):
        for k in range(TOP_K):
            _row_copy(ys_ref, dest_ref[t * TOP_K + k], buf.at[k], t, sem).start(priority=k % 2)
        return carry

    lax.fori_loop(0, tt, issue, 0)
    for k in range(TOP_K):
        pltpu.make_async_copy(ys_ref.at[pl.ds(0, tt)], buf.at[k], sem).wait()
    g4 = g4_ref[...]
    moe = g4[:, 0:1] * buf[0]
    for k in range(1, TOP_K):
        moe = moe + g4[:, k:k + 1] * buf[k]
    o_ref[...] = _rms(x1_ref[...] + moe, nf_ref[...])


def _combine(dest_flat, x1, g4, nf, ys):
    n = x1.shape[0]
    tt = _tile(n, 256)
    row = lambda i: (i, 0)
    return pl.pallas_call(
        _combine_kernel,
        grid=(n // tt,),
        in_specs=[
            pl.BlockSpec((tt * TOP_K,), lambda i: (i,), memory_space=pltpu.SMEM),
            pl.BlockSpec((tt, D_MODEL), row),
            pl.BlockSpec((tt, TOP_K), row),
            pl.BlockSpec((1, D_MODEL), lambda i: (0, 0)),
            pl.BlockSpec(memory_space=pl.ANY),
        ],
        out_specs=pl.BlockSpec((tt, D_MODEL), row),
        out_shape=jax.ShapeDtypeStruct((n, D_MODEL), F32),
        scratch_shapes=[pltpu.VMEM((TOP_K, tt, D_MODEL), F32), pltpu.SemaphoreType.DMA(())],
        compiler_params=_cparams("arbitrary"),
        name="moe_combine",
    )(dest_flat, x1, g4, nf, ys)


def _rope_table(pos):
    half = MLA_ROPE // 2
    inv_freq = ROPE_THETA ** (-jnp.arange(half, dtype=F32) / half)
    ang = pos.astype(F32)[:, None] * inv_freq[None, :]
    cos, sin = jnp.cos(ang), jnp.sin(ang)
    return jnp.concatenate([cos, cos, -sin, sin], axis=-1)


def _regroup_w_in(w_in):
    o = np.cumsum((Q_LORA, KV_LORA, MLA_ROPE, GLA_DK, GLA_DK, GLA_DV, GLA_GATE_RANK, GLA_DV, 2 * D_MODEL))
    c_q, c_kv, k_r = w_in[:, :o[0]], w_in[:, o[0]:o[1]], w_in[:, o[1]:o[2]]
    gq, gk, gv = w_in[:, o[2]:o[3]], w_in[:, o[3]:o[4]], w_in[:, o[4]:o[5]]
    ga, gg, mg = w_in[:, o[5]:o[6]], w_in[:, o[6]:o[7]], w_in[:, o[7]:o[8]]
    half = MLA_ROPE // 2
    k_r_sw = jnp.concatenate([k_r[:, half:], k_r[:, :half]], axis=1)
    ga_pad = jnp.zeros((D_MODEL, LANES - GLA_GATE_RANK), w_in.dtype)
    return jnp.concatenate([c_q, c_kv, k_r, k_r_sw, ga, ga_pad, gv, gg, gq, gk, mg], axis=1).astype(BF16)


def _regroup_w_uq(w_uq):
    w = w_uq.reshape(Q_LORA, MLA_HEADS, MLA_NOPE + MLA_ROPE)
    half = MLA_ROPE // 2
    nope, rope = w[..., :MLA_NOPE], w[..., MLA_NOPE:]
    rope_sw = jnp.concatenate([rope[..., half:], rope[..., :half]], axis=-1)
    return jnp.concatenate([nope, rope, rope_sw], axis=-1).reshape(Q_LORA, MLA_HEADS * HEAD_PAD).astype(BF16)


def _regroup_w_ukv(w_ukv):
    w = w_ukv.reshape(KV_LORA, MLA_HEADS, MLA_NOPE + MLA_V)
    k = w[..., :MLA_NOPE].reshape(KV_LORA, MLA_HEADS * MLA_NOPE)
    v = w[..., MLA_NOPE:].reshape(KV_LORA, MLA_HEADS * MLA_V)
    return jnp.concatenate([k, v], axis=1).astype(BF16)


def _slot_plan(counts_first, counts_rest, n_blk):
    counts_first = counts_first.astype(jnp.int32)
    counts = counts_first + counts_rest.astype(jnp.int32)
    padded = (counts + MOE_ROWS - 1) // MOE_ROWS * MOE_ROWS
    pad_end = jnp.cumsum(padded)
    pad_start = pad_end - padded
    n_used = (pad_end[-1] // MOE_ROWS).reshape(1)
    blk_start = jnp.arange(n_blk, dtype=jnp.int32) * MOE_ROWS
    blk_exp = jnp.sum(blk_start[:, None] >= pad_end[None, :], axis=1).astype(jnp.int32)
    last_exp = blk_exp[jnp.maximum(n_used[0] - 1, 0)]
    blk_exp = jnp.where(jnp.arange(n_blk) < n_used[0], blk_exp, last_exp)
    ps_row = jnp.pad(pad_start.astype(F32), (0, LANES - N_EXPERTS))[None, :]
    zero_first = jnp.concatenate([(pad_start + counts_first) // MOE_ROWS, n_used]).astype(jnp.int32)
    zero_last = jnp.concatenate([pad_end // MOE_ROWS, jnp.full((1,), n_blk)]).astype(jnp.int32)
    return ps_row, zero_first, zero_last, blk_exp, n_used


def _mixers(x, past_lat, past_kr, s0, past_len, p):
    bsz, seq, _ = x.shape
    n = bsz * seq
    x2d = x.reshape(n, D_MODEL)
    hin = _in_proj(x2d, p["norm1_g"], p["w_all"])
    tm = min(512, n)
    cs = _rope_table(past_len + jnp.arange(seq))
    if seq < tm:
        cs = jnp.tile(cs, (tm // seq, 1))
    q, lat_new, krp_new = _mla_q(hin, cs, p["q_norm_g"], p["kv_norm_g"], p["w_uq"])
    if past_len:
        lat_all = jnp.concatenate([past_lat, lat_new.reshape(bsz, seq, KV_LORA)], axis=1)
        kr_pad = jnp.pad(past_kr, ((0, 0), (0, 0), (0, LANES - MLA_ROPE)))
        krp_all = jnp.concatenate([kr_pad, krp_new.reshape(bsz, seq, LANES)], axis=1)
    else:
        lat_all, krp_all = lat_new, krp_new
    sk = past_len + seq
    kcat, v = _kv_up(lat_all.reshape(bsz * sk, KV_LORA), krp_all.reshape(bsz * sk, LANES), p["w_ukv"])
    tq = min(1024, seq)
    tk = 1024 if sk % 1024 == 0 else sk
    attn = _flash(q.reshape(bsz, seq, -1), kcat.reshape(bsz, sk, -1), v.reshape(bsz, sk, -1), past_len, tq, tk)
    og, s_new = _gla(hin, p["w_alpha"], p["b_alpha"], p["gla_norm_g"], s0, bsz, seq)
    x1, hm, sel, g4, cnt = _post_mix(attn.reshape(n, -1), og, hin, x2d, p["w_o_mla"], p["w_o_gla"], p["w_out"],
                                     p["norm2_g"], p["w_router"], p["b_router"])
    new_state = (lat_new.reshape(bsz, seq, KV_LORA), krp_new[:, :MLA_ROPE].reshape(bsz, seq, MLA_ROPE), s_new)
    return x1, hm, sel, g4, cnt, new_state


def kernel(x_prompt, x_sample, cache_mla_latent, cache_mla_krope, state_gla, norm1_g, w_in, q_norm_g, kv_norm_g,
           w_uq, w_ukv, w_o_mla, w_alpha2, b_alpha, gla_norm_g, w_o_gla, w_out, norm2_g, w_router, b_router,
           w1, b1, w2, b2, normf_g):
    depth = w_in.shape[0]
    assert depth == 1
    l = 0
    bp, sp, _ = x_prompt.shape
    bs, ss, _ = x_sample.shape
    past_len = cache_mla_latent.shape[2]
    p = {
        "norm1_g": norm1_g[l][None, :],
        "w_all": _regroup_w_in(w_in[l]),
        "q_norm_g": q_norm_g[l][None, :],
        "kv_norm_g": kv_norm_g[l][None, :],
        "w_uq": _regroup_w_uq(w_uq[l]),
        "w_ukv": _regroup_w_ukv(w_ukv[l]),
        "w_o_mla": w_o_mla[l].astype(BF16),
        "w_alpha": jnp.pad(w_alpha2[l], ((0, LANES - GLA_GATE_RANK), (0, 0))).astype(BF16),
        "b_alpha": b_alpha[l][None, :],
        "gla_norm_g": gla_norm_g[l][None, :],
        "w_o_gla": w_o_gla[l].astype(BF16),
        "w_out": w_out[l].astype(BF16),
        "norm2_g": norm2_g[l][None, :],
        "w_router": jnp.pad(w_router[l], ((0, 0), (0, LANES - N_EXPERTS))).astype(BF16),
        "b_router": jnp.pad(b_router[l], (0, LANES - N_EXPERTS))[None, :],
    }
    zeros_state = jnp.zeros((bp, GLA_HEADS, GLA_DK_HEAD, GLA_DV_HEAD), F32)
    x1p, hmp, selp, g4p, cntp, (lat_p, kr_p, st_p) = _mixers(x_prompt, None, None, zeros_state, 0, p)
    x1s, hms, sels, g4s, cnts, (lat_s, kr_s, st_s) = _mixers(x_sample, cache_mla_latent[l], cache_mla_krope[l],
                                                              state_gla[l], past_len, p)

    n_p, n_s = bp * sp, bs * ss
    n_asg = (n_p + n_s) * TOP_K
    n_slot = -(-n_asg // MOE_ROWS) * MOE_ROWS + N_EXPERTS * MOE_ROWS
    ps_row, zero_first, zero_last, blk_exp, n_used = _slot_plan(cntp[0, :N_EXPERTS], cnts[0, :N_EXPERTS],
                                                                n_slot // MOE_ROWS)
    dest = _dest(jnp.concatenate([selp, sels], axis=0), ps_row).reshape(n_asg)
    dest_p, dest_s = dest[:n_p * TOP_K], dest[n_p * TOP_K:]
    xs = _scatter_init(zero_first, zero_last, dest_p, hmp, n_slot)
    xs = _scatter_more(dest_s, hms, xs)
    w1g_t, w1l_t = _w1_split(w1[l])
    b1g = b1[l][:, None, 0::2]
    b1lin = b1[l][:, None, 1::2]
    ys_slots = _experts(blk_exp, n_used, xs, w1g_t, w1l_t, b1g, b1lin, w2[l].astype(BF16), b2[l][:, None, :])
    yp = _combine(dest_p, x1p, g4p, normf_g[None, :], ys_slots).reshape(bp, sp, D_MODEL)
    ys = _combine(dest_s, x1s, g4s, normf_g[None, :], ys_slots).reshape(bs, ss, D_MODEL)
    return (yp, ys, lat_p[None], kr_p[None], st_p[None], lat_s[None], kr_s[None], st_s[None])
```

```python
import functools

import jax
import jax.numpy as jnp
import numpy as np
from jax import lax
from jax.experimental import pallas as pl
from jax.experimental.pallas import tpu as pltpu

F32 = jnp.float32
BF16 = jnp.bfloat16

D_MODEL = 1024
CHUNK = 64
CHUNK_SHIFT = 6
LOG2_E = 1.4426950408889634
NORM_EPS = 1e-6
MLA_HEADS = 8
MLA_NOPE = 128
MLA_ROPE = 64
MLA_V = 128
Q_LORA = 512
KV_LORA = 256
ROPE_THETA = 10000.0
MLA_SCALE = (MLA_NOPE + MLA_ROPE) ** -0.5
NEG_INF = -1e30
GLA_HEADS = 4
GLA_DK = D_MODEL // 2
GLA_DV = D_MODEL
GLA_DK_HEAD = GLA_DK // GLA_HEADS
GLA_DV_HEAD = GLA_DV // GLA_HEADS
GLA_GATE_RANK = 16
GLA_GATE_TEMP = 16.0
GLA_SUB = 16
FLASH_HEADS = 2
POST_MIX_PARTS = 2
N_EXPERTS = 32
TOP_K = 4
D_EXPERT = D_MODEL
SWIGLU_LIMIT = 7.0
SWIGLU_ALPHA = 1.702
MOE_ROWS = 512

LANES = 128
HEAD_PAD = 256

COL_CQ = 0
COL_CKV = 512
COL_KR = 768
COL_GA = 896
COL_GV = 1024
COL_GG = 2048
COL_GQ = 3072
COL_GK = 3584
COL_MG = 4096
D_IN_PAD = 6144

VMEM_LIMIT = 56 * 1024 * 1024


def _cparams(*sem, flags=None):
    return pltpu.CompilerParams(dimension_semantics=sem, vmem_limit_bytes=VMEM_LIMIT, flags=flags)


def _tile(n, pref, mult=16):
    t = min(pref, n)
    t -= t % mult
    while n % t:
        t -= mult
    return t


def _rms(x, g):
    return x * lax.rsqrt(jnp.mean(x * x, axis=-1, keepdims=True) + NORM_EPS) * g


def _in_proj_kernel(x_ref, g_ref, w_ref, o_ref):
    h = _rms(x_ref[...], g_ref[...]).astype(BF16)
    tn = 1536
    for j in range(D_IN_PAD // tn):
        o_ref[:, j * tn:(j + 1) * tn] = jnp.dot(h, w_ref[:, j * tn:(j + 1) * tn], preferred_element_type=F32)


def _in_proj(x2d, g, w_all):
    n = x2d.shape[0]
    tm = min(256, n)
    return pl.pallas_call(
        _in_proj_kernel,
        grid=(n // tm,),
        in_specs=[
            pl.BlockSpec((tm, D_MODEL), lambda i: (i, 0)),
            pl.BlockSpec((1, D_MODEL), lambda i: (0, 0)),
            pl.BlockSpec((D_MODEL, D_IN_PAD), lambda i: (0, 0)),
        ],
        out_specs=pl.BlockSpec((tm, D_IN_PAD), lambda i: (i, 0)),
        out_shape=jax.ShapeDtypeStruct((n, D_IN_PAD), F32),
        compiler_params=_cparams("parallel"),
        name="in_proj",
    )(x2d, g, w_all)


def _rope_pair(blk, cs):
    t = blk * cs
    return t + pltpu.roll(t, MLA_ROPE, axis=1)


def _mla_q_kernel(h_ref, cs_ref, qg_ref, kvg_ref, wuq_ref, q_ref, lat_ref, kr_ref):
    cs = cs_ref[...]
    qn = _rms(h_ref[:, COL_CQ:COL_CQ + Q_LORA], qg_ref[...]).astype(BF16)
    q = jnp.dot(qn, wuq_ref[...], preferred_element_type=F32)
    for hh in range(MLA_HEADS):
        c0 = hh * HEAD_PAD
        q_ref[:, c0:c0 + MLA_NOPE] = q[:, c0:c0 + MLA_NOPE].astype(BF16)
        q_ref[:, c0 + MLA_NOPE:c0 + HEAD_PAD] = _rope_pair(q[:, c0 + MLA_NOPE:c0 + HEAD_PAD], cs).astype(BF16)
    lat_ref[...] = _rms(h_ref[:, COL_CKV:COL_CKV + KV_LORA], kvg_ref[...])
    r = _rope_pair(h_ref[:, COL_KR:COL_KR + LANES], cs)
    lane = lax.broadcasted_iota(jnp.int32, r.shape, 1)
    kr_ref[...] = jnp.where(lane < MLA_ROPE, r, 0.0)


def _mla_q(hin, cs, qg, kvg, wuq):
    n = hin.shape[0]
    tm = min(512, n)
    n_cs = cs.shape[0] // tm
    return pl.pallas_call(
        _mla_q_kernel,
        grid=(n // tm,),
        in_specs=[
            pl.BlockSpec((tm, COL_GA), lambda i: (i, 0)),
            pl.BlockSpec((tm, LANES), lambda i: (i % n_cs, 0)),
            pl.BlockSpec((1, Q_LORA), lambda i: (0, 0)),
            pl.BlockSpec((1, KV_LORA), lambda i: (0, 0)),
            pl.BlockSpec((Q_LORA, MLA_HEADS * HEAD_PAD), lambda i: (0, 0)),
        ],
        out_specs=[
            pl.BlockSpec((tm, MLA_HEADS * HEAD_PAD), lambda i: (i, 0)),
            pl.BlockSpec((tm, KV_LORA), lambda i: (i, 0)),
            pl.BlockSpec((tm, LANES), lambda i: (i, 0)),
        ],
        out_shape=[
            jax.ShapeDtypeStruct((n, MLA_HEADS * HEAD_PAD), BF16),
            jax.ShapeDtypeStruct((n, KV_LORA), F32),
            jax.ShapeDtypeStruct((n, LANES), F32),
        ],
        compiler_params=_cparams("parallel"),
        name="mla_q",
    )(hin, cs, qg, kvg, wuq)


def _kv_up_kernel(lat_ref, kr_ref, w_ref, k_ref, v_ref):
    kv = jnp.dot(lat_ref[...].astype(BF16), w_ref[...], preferred_element_type=F32)
    krb = kr_ref[...].astype(BF16)
    for hh in range(MLA_HEADS):
        c0 = hh * HEAD_PAD
        k_ref[:, c0:c0 + MLA_NOPE] = kv[:, hh * MLA_NOPE:(hh + 1) * MLA_NOPE].astype(BF16)
        k_ref[:, c0 + MLA_NOPE:c0 + HEAD_PAD] = krb
    v_ref[...] = kv[:, MLA_HEADS * MLA_NOPE:].astype(BF16)


def _kv_up(lat, krp, wukv):
    n = lat.shape[0]
    tm = _tile(n, 512)
    return pl.pallas_call(
        _kv_up_kernel,
        grid=(n // tm,),
        in_specs=[
            pl.BlockSpec((tm, KV_LORA), lambda i: (i, 0)),
            pl.BlockSpec((tm, LANES), lambda i: (i, 0)),
            pl.BlockSpec((KV_LORA, MLA_HEADS * (MLA_NOPE + MLA_V)), lambda i: (0, 0)),
        ],
        out_specs=[
            pl.BlockSpec((tm, MLA_HEADS * HEAD_PAD), lambda i: (i, 0)),
            pl.BlockSpec((tm, MLA_HEADS * MLA_V), lambda i: (i, 0)),
        ],
        out_shape=[
            jax.ShapeDtypeStruct((n, MLA_HEADS * HEAD_PAD), BF16),
            jax.ShapeDtypeStruct((n, MLA_HEADS * MLA_V), BF16),
        ],
        compiler_params=_cparams("parallel"),
        name="kv_up",
    )(lat, krp, wukv)


def _flash_kernel(q_ref, k_ref, v_ref, o_ref, *, tq, tk, past_len):
    qi = pl.program_id(2)
    q_pos0 = past_len + qi * tq
    q_chunk = (q_pos0 + lax.broadcasted_iota(jnp.int32, (tq, 1), 0)) >> CHUNK_SHIFT
    n_blocks = (q_pos0 + tq + tk - 1) // tk
    n_full = ((q_pos0 >> CHUNK_SHIFT) << CHUNK_SHIFT) // tk
    qs = [q_ref[0, :, hh * HEAD_PAD:(hh + 1) * HEAD_PAD] for hh in range(FLASH_HEADS)]

    def step(j, carry, masked):
        k0 = pl.multiple_of(j * tk, tk)
        if masked:
            k_chunk = (k0 + lax.broadcasted_iota(jnp.int32, (1, tk), 1)) >> CHUNK_SHIFT
        out = []
        for hh in range(FLASH_HEADS):
            m, l, acc = carry[hh]
            k = k_ref[0, pl.ds(k0, tk), hh * HEAD_PAD:(hh + 1) * HEAD_PAD]
            v = v_ref[0, pl.ds(k0, tk), hh * MLA_V:(hh + 1) * MLA_V]
            s = lax.dot_general(qs[hh], k, (((1,), (1,)), ((), ())), preferred_element_type=F32)
            s = s * (MLA_SCALE * LOG2_E)
            if masked:
                s = jnp.where(k_chunk <= q_chunk, s, NEG_INF)
            m_new = jnp.maximum(m, jnp.max(s, axis=-1, keepdims=True))
            alpha = jnp.exp2(m - m_new)
            p = jnp.exp2(s - m_new)
            l = alpha * l + jnp.sum(p, axis=-1, keepdims=True)
            acc = alpha * acc + jnp.dot(p.astype(BF16), v, preferred_element_type=F32)
            out.append((m_new, l, acc))
        return tuple(out)

    init = tuple((jnp.full((tq, 1), NEG_INF, F32), jnp.zeros((tq, 1), F32), jnp.zeros((tq, MLA_V), F32))
                 for _ in range(FLASH_HEADS))
    carry = lax.fori_loop(0, n_full, functools.partial(step, masked=False), init)
    carry = lax.fori_loop(n_full, n_blocks, functools.partial(step, masked=True), carry)
    for hh in range(FLASH_HEADS):
        _, l, acc = carry[hh]
        o_ref[0, :, hh * MLA_V:(hh + 1) * MLA_V] = (acc / l).astype(BF16)


def _flash(q, k, v, past_len, tq, tk):
    b, sq, _ = q.shape
    sk = k.shape[1]
    fh = FLASH_HEADS
    return pl.pallas_call(
        functools.partial(_flash_kernel, tq=tq, tk=tk, past_len=past_len),
        grid=(b, MLA_HEADS // fh, sq // tq),
        in_specs=[
            pl.BlockSpec((1, tq, fh * HEAD_PAD), lambda bi, h, i: (bi, i, h)),
            pl.BlockSpec((1, sk, fh * HEAD_PAD), lambda bi, h, i: (bi, 0, h)),
            pl.BlockSpec((1, sk, fh * MLA_V), lambda bi, h, i: (bi, 0, h)),
        ],
        out_specs=pl.BlockSpec((1, tq, fh * MLA_V), lambda bi, h, i: (bi, i, h)),
        out_shape=jax.ShapeDtypeStruct((b, sq, MLA_HEADS * MLA_V), BF16),
        compiler_params=_cparams("parallel", "parallel", "arbitrary"),
        name="flash",
    )(q, k, v)


def _cumsum_rows(x):
    n = x.shape[0]
    row = lax.broadcasted_iota(jnp.int32, x.shape, 0)
    s = 1
    while s < n:
        x = x + jnp.where(row >= s, pltpu.roll(x, s, axis=0), 0.0)
        s *= 2
    return x


def _gla_kernel(gv_ref, gg_ref, gq_ref, gk_ref, ga_ref, wa_ref, ba_ref, ng_ref, s0_ref, og_ref, sn_ref, st_scr,
                *, csize, n_chunks):
    t = pl.program_id(1)

    @pl.when(t == 0)
    def _():
        for hh in range(GLA_HEADS):
            st_scr[hh] = s0_ref[0, hh].T

    wa = wa_ref[...]
    ba = ba_ref[...]
    ng = ng_ref[...]
    n_sub = csize // GLA_SUB
    col_id = lax.broadcasted_iota(jnp.int32, (GLA_SUB, csize), 1)
    row_id = lax.broadcasted_iota(jnp.int32, (GLA_SUB, GLA_DK_HEAD), 0)

    def chunk(c, carry):
        r0 = pl.multiple_of(c * csize, csize)
        rows = pl.ds(r0, csize)
        x = jnp.dot(ga_ref[rows, :].astype(BF16), wa, preferred_element_type=F32) + ba
        la = jax.nn.log_sigmoid(x) / GLA_GATE_TEMP
        b_all = _cumsum_rows(la)
        for hh in range(GLA_HEADS):
            head(hh, rows, b_all[:, hh * GLA_DK_HEAD:(hh + 1) * GLA_DK_HEAD])
        return carry

    def head(hh, rows, b):
        kcols = slice(hh * GLA_DK_HEAD, (hh + 1) * GLA_DK_HEAD)
        vcols = slice(hh * GLA_DV_HEAD, (hh + 1) * GLA_DV_HEAD)
        q = gq_ref[rows, kcols] * (GLA_DK_HEAD ** -0.5)
        k = gk_ref[rows, kcols]
        v = gv_ref[rows, vcols]

        a_rows = []
        for i in range(n_sub):
            lo = i * GLA_SUB
            bi = b[lo:lo + GLA_SUB]
            qi = q[lo:lo + GLA_SUB]
            ki = k[lo:lo + GLA_SUB]
            if i == 0:
                a_i = jnp.zeros((GLA_SUB, csize), F32)
            else:
                ri = b[lo - 1:lo]
                qs = qi * jnp.exp(bi - ri)
                ks = k * jnp.exp(jnp.minimum(ri - b, 0.0))
                a_i = lax.dot_general(qs.astype(BF16), ks.astype(BF16), (((1,), (1,)), ((), ())),
                                      preferred_element_type=F32)
                a_i = jnp.where(col_id < lo, a_i, 0.0)
            for s in range(GLA_SUB):
                e = jnp.where(row_id >= s, jnp.exp(bi - bi[s:s + 1]), 0.0)
                col = jnp.sum(qi * ki[s:s + 1] * e, axis=-1, keepdims=True)
                a_i = jnp.where(col_id == lo + s, col, a_i)
            a_rows.append(a_i)
        a = jnp.concatenate(a_rows, axis=0)

        st = st_scr[hh]
        vb = v.astype(BF16)
        o = jnp.dot(a.astype(BF16), vb, preferred_element_type=F32)
        o = o + lax.dot_general((q * jnp.exp(b)).astype(BF16), st.astype(BF16), (((1,), (1,)), ((), ())),
                                preferred_element_type=F32)
        b_end = b[csize - 1:csize]
        kd = (k * jnp.exp(b_end - b)).astype(BF16)
        st_scr[hh] = jnp.exp(b_end) * st + jnp.dot(v.T.astype(BF16), kd, preferred_element_type=F32)
        on = _rms(o, ng)
        og_ref[rows, vcols] = (on * jax.nn.silu(gg_ref[rows, vcols])).astype(BF16)

    lax.fori_loop(0, n_chunks, chunk, 0)

    @pl.when(t == pl.num_programs(1) - 1)
    def _():
        for hh in range(GLA_HEADS):
            sn_ref[0, hh] = st_scr[hh].T


def _gla(hin, wa, ba, ng, s0, bsz, seq):
    csize = min(CHUNK, seq)
    tt = min(512, seq)
    n_t = seq // tt
    n = bsz * seq
    kern = functools.partial(_gla_kernel, csize=csize, n_chunks=tt // csize)
    state = pl.BlockSpec((1, GLA_HEADS, GLA_DK_HEAD, GLA_DV_HEAD), lambda b, t: (b, 0, 0, 0))
    return pl.pallas_call(
        kern,
        grid=(bsz, n_t),
        in_specs=[
            pl.BlockSpec((tt, GLA_DV), lambda b, t: (b * n_t + t, COL_GV // GLA_DV)),
            pl.BlockSpec((tt, GLA_DV), lambda b, t: (b * n_t + t, COL_GG // GLA_DV)),
            pl.BlockSpec((tt, GLA_DK), lambda b, t: (b * n_t + t, COL_GQ // GLA_DK)),
            pl.BlockSpec((tt, GLA_DK), lambda b, t: (b * n_t + t, COL_GK // GLA_DK)),
            pl.BlockSpec((tt, LANES), lambda b, t: (b * n_t + t, COL_GA // LANES)),
            pl.BlockSpec((LANES, GLA_DK), lambda b, t: (0, 0)),
            pl.BlockSpec((1, GLA_DK), lambda b, t: (0, 0)),
            pl.BlockSpec((1, GLA_DV_HEAD), lambda b, t: (0, 0)),
            state,
        ],
        out_specs=[pl.BlockSpec((tt, GLA_DV), lambda b, t: (b * n_t + t, 0)), state],
        out_shape=[
            jax.ShapeDtypeStruct((n, GLA_DV), BF16),
            jax.ShapeDtypeStruct((bsz, GLA_HEADS, GLA_DK_HEAD, GLA_DV_HEAD), F32),
        ],
        scratch_shapes=[pltpu.VMEM((GLA_HEADS, GLA_DV_HEAD, GLA_DK_HEAD), F32)],
        compiler_params=_cparams("parallel", "arbitrary"),
        name="gla",
    )(hin, hin, hin, hin, hin, wa, ba, ng, s0)


def _post_mix_kernel(at_ref, og_ref, mg1_ref, mg2_ref, x_ref, wom_ref, wog_ref, wout_ref, n2_ref, wr_ref, br_ref,
                     x1_ref, hm_ref, sel_ref, g4_ref, cnt_ref):
    @pl.when(pl.program_id(0) == 0)
    def _():
        cnt_ref[...] = jnp.zeros_like(cnt_ref)

    tm = x_ref.shape[0]
    half = tm // POST_MIX_PARTS
    for part in range(POST_MIX_PARTS):
        rows = slice(part * half, (part + 1) * half)
        y_mla = jnp.dot(at_ref[rows, :], wom_ref[...], preferred_element_type=F32)
        y_gla = jnp.dot(og_ref[rows, :], wog_ref[...], preferred_element_type=F32)
        m = jax.nn.sigmoid(mg1_ref[rows, :]) * y_mla + jax.nn.sigmoid(mg2_ref[rows, :]) * y_gla
        x1 = x_ref[rows, :] + jnp.dot(m.astype(BF16), wout_ref[...], preferred_element_type=F32)
        x1_ref[rows, :] = x1
        hm = _rms(x1, n2_ref[...])
        hm_ref[rows, :] = hm
        logits = jnp.dot(hm.astype(BF16), wr_ref[...], preferred_element_type=F32) + br_ref[...]

        lane = lax.broadcasted_iota(jnp.int32, logits.shape, 1)
        work = jnp.where(lane < N_EXPERTS, logits, -jnp.inf)
        sel = jnp.zeros(logits.shape, F32)
        vals = []
        for k in range(TOP_K):
            mk = jnp.max(work, axis=-1, keepdims=True)
            ik = jnp.min(jnp.where(work == mk, lane, LANES), axis=-1, keepdims=True)
            hit = lane == ik
            sel = jnp.where(hit, float(k + 1), sel)
            work = jnp.where(hit, -jnp.inf, work)
            vals.append(mk)
        sel_ref[rows, :] = sel
        ex = [jnp.exp(v - vals[0]) for v in vals]
        den = ex[0]
        for e in ex[1:]:
            den = den + e
        g4 = jnp.zeros(logits.shape, F32)
        for k in range(TOP_K):
            g4 = jnp.where(lane == k, ex[k] / den, g4)
        g4_ref[rows, :] = g4[:, :TOP_K]
        cnt_ref[...] += jnp.sum(jnp.where(sel > 0.0, 1.0, 0.0), axis=0, keepdims=True)


def _post_mix(attn, og, hin, x2d, wom, wog, wout, n2, wr, br):
    n = x2d.shape[0]
    tm = min(256 * POST_MIX_PARTS, n)
    row = lambda i: (i, 0)
    const = lambda i: (0, 0)
    return pl.pallas_call(
        _post_mix_kernel,
        grid=(n // tm,),
        in_specs=[
            pl.BlockSpec((tm, D_MODEL), row),
            pl.BlockSpec((tm, D_MODEL), row),
            pl.BlockSpec((tm, D_MODEL), lambda i: (i, COL_MG // D_MODEL)),
            pl.BlockSpec((tm, D_MODEL), lambda i: (i, COL_MG // D_MODEL + 1)),
            pl.BlockSpec((tm, D_MODEL), row),
            pl.BlockSpec((D_MODEL, D_MODEL), const),
            pl.BlockSpec((D_MODEL, D_MODEL), const),
            pl.BlockSpec((D_MODEL, D_MODEL), const),
            pl.BlockSpec((1, D_MODEL), const),
            pl.BlockSpec((D_MODEL, LANES), const),
            pl.BlockSpec((1, LANES), const),
        ],
        out_specs=[
            pl.BlockSpec((tm, D_MODEL), row),
            pl.BlockSpec((tm, D_MODEL), row),
            pl.BlockSpec((tm, LANES), row),
            pl.BlockSpec((tm, TOP_K), row),
            pl.BlockSpec((1, LANES), const),
        ],
        out_shape=[
            jax.ShapeDtypeStruct((n, D_MODEL), F32),
            jax.ShapeDtypeStruct((n, D_MODEL), F32),
            jax.ShapeDtypeStruct((n, LANES), F32),
            jax.ShapeDtypeStruct((n, TOP_K), F32),
            jax.ShapeDtypeStruct((1, LANES), F32),
        ],
        compiler_params=_cparams("arbitrary"),
        name="post_mix",
    )(attn, og, hin, hin, x2d, wom, wog, wout, n2, wr, br)


def _dest_kernel(sel_ref, ps_ref, dest_ref, run_scr):
    @pl.when(pl.program_id(0) == 0)
    def _():
        run_scr[...] = ps_ref[...]

    sel = sel_ref[...]
    tt = sel.shape[0]
    oh = jnp.where(sel > 0.0, 1.0, 0.0)
    r = lax.broadcasted_iota(jnp.int32, (tt, tt), 0)
    c = lax.broadcasted_iota(jnp.int32, (tt, tt), 1)
    ltri = jnp.where(r > c, 1.0, 0.0).astype(BF16)
    slot = jnp.dot(ltri, oh.astype(BF16), preferred_element_type=F32) + run_scr[...]
    run_scr[...] += jnp.sum(oh, axis=0, keepdims=True)
    lane = lax.broadcasted_iota(jnp.int32, sel.shape, 1)
    d = jnp.zeros(sel.shape, F32)
    for k in range(TOP_K):
        col = jnp.sum(jnp.where(sel == float(k + 1), slot, 0.0), axis=-1, keepdims=True)
        d = jnp.where(lane == k, col, d)
    dest_ref[...] = d[:, :TOP_K].astype(jnp.int32)


def _dest(sel, pad_start):
    n = sel.shape[0]
    tt = _tile(n, 512)
    return pl.pallas_call(
        _dest_kernel,
        grid=(n // tt,),
        in_specs=[pl.BlockSpec((tt, LANES), lambda i: (i, 0)), pl.BlockSpec((1, LANES), lambda i: (0, 0))],
        out_specs=pl.BlockSpec((tt, TOP_K), lambda i: (i, 0)),
        out_shape=jax.ShapeDtypeStruct((n, TOP_K), jnp.int32),
        scratch_shapes=[pltpu.VMEM((1, LANES), F32)],
        compiler_params=_cparams("arbitrary"),
        name="route_dest",
    )(sel, pad_start)


def _row_copy(src, src_row, dst, dst_row, sem):
    return pltpu.make_async_copy(src.at[pl.ds(src_row, 1)], dst.at[pl.ds(dst_row, 1)], sem)


def _scatter_rows(dest_ref, hm_ref, xs_ref, sem):
    tt = hm_ref.shape[0]

    def issue(t, carry):
        for k in range(TOP_K):
            _row_copy(hm_ref, t, xs_ref, dest_ref[t * TOP_K + k], sem).start()
        return carry

    lax.fori_loop(0, tt, issue, 0)
    for k in range(TOP_K):
        pltpu.make_async_copy(hm_ref, xs_ref.at[pl.ds(0, tt)], sem).wait()


def _scatter_init_kernel(zf_ref, zl_ref, dest_ref, hm_ref, xs_ref, zero_scr, sem, zsem):
    @pl.when(pl.program_id(0) == 0)
    def _():
        zero_scr[...] = jnp.zeros_like(zero_scr)

        def zero_block(i):
            return pltpu.make_async_copy(zero_scr, xs_ref.at[pl.ds(pl.multiple_of(i * MOE_ROWS, MOE_ROWS), MOE_ROWS)],
                                         zsem)

        def start_range(e, carry):
            return lax.fori_loop(zf_ref[e], zl_ref[e], lambda i, c: (zero_block(i).start(), c)[1], carry)

        def wait_range(e, carry):
            return lax.fori_loop(zf_ref[e], zl_ref[e], lambda i, c: (zero_block(i).wait(), c)[1], carry)

        lax.fori_loop(0, N_EXPERTS + 1, start_range, 0)
        lax.fori_loop(0, N_EXPERTS + 1, wait_range, 0)

    _scatter_rows(dest_ref, hm_ref, xs_ref, sem)


def _scatter_more_kernel(dest_ref, hm_ref, xs_in, xs_ref, sem):
    del xs_in
    _scatter_rows(dest_ref, hm_ref, xs_ref, sem)


def _scatter_init(zero_first, zero_last, dest_flat, hm, n_slot):
    n = hm.shape[0]
    tt = _tile(n, 256)
    grid_spec = pltpu.PrefetchScalarGridSpec(
        num_scalar_prefetch=2,
        grid=(n // tt,),
        in_specs=[
            pl.BlockSpec((tt * TOP_K,), lambda i, pe, nb: (i,), memory_space=pltpu.SMEM),
            pl.BlockSpec((tt, D_MODEL), lambda i, pe, nb: (i, 0)),
        ],
        out_specs=pl.BlockSpec(memory_space=pl.ANY),
        scratch_shapes=[pltpu.VMEM((MOE_ROWS, D_MODEL), F32), pltpu.SemaphoreType.DMA(()),
                        pltpu.SemaphoreType.DMA(())],
    )
    return pl.pallas_call(
        _scatter_init_kernel,
        grid_spec=grid_spec,
        out_shape=jax.ShapeDtypeStruct((n_slot, D_MODEL), F32),
        compiler_params=_cparams("arbitrary"),
        name="moe_scatter_init",
    )(zero_first, zero_last, dest_flat, hm)


def _scatter_more(dest_flat, hm, xs):
    n = hm.shape[0]
    tt = _tile(n, 256)
    return pl.pallas_call(
        _scatter_more_kernel,
        grid=(n // tt,),
        in_specs=[
            pl.BlockSpec((tt * TOP_K,), lambda i: (i,), memory_space=pltpu.SMEM),
            pl.BlockSpec((tt, D_MODEL), lambda i: (i, 0)),
            pl.BlockSpec(memory_space=pl.ANY),
        ],
        out_specs=pl.BlockSpec(memory_space=pl.ANY),
        out_shape=jax.ShapeDtypeStruct(xs.shape, xs.dtype),
        scratch_shapes=[pltpu.SemaphoreType.DMA(())],
        input_output_aliases={2: 0},
        compiler_params=_cparams("arbitrary"),
        name="moe_scatter",
    )(dest_flat, hm, xs)


def _w1_split_kernel(w_ref, g_ref, l_ref, t_scr):
    n_slab = w_ref.shape[1] // LANES
    for c in range(n_slab):
        t_scr[c] = w_ref[0, c * LANES:(c + 1) * LANES, :].T
    for c in range(n_slab):
        g_ref[0, :, c * LANES:(c + 1) * LANES] = t_scr[c, pl.ds(0, D_EXPERT, stride=2), :].astype(BF16)
        l_ref[0, :, c * LANES:(c + 1) * LANES] = t_scr[c, pl.ds(1, D_EXPERT, stride=2), :].astype(BF16)


def _w1_split(w1):
    ks = 512
    n_slab = ks // LANES
    out = jax.ShapeDtypeStruct((N_EXPERTS, D_EXPERT, D_MODEL), BF16)
    return pl.pallas_call(
        _w1_split_kernel,
        grid=(N_EXPERTS, D_MODEL // ks),
        in_specs=[pl.BlockSpec((1, ks, 2 * D_EXPERT), lambda e, j: (e, j, 0))],
        out_specs=[pl.BlockSpec((1, D_EXPERT, ks), lambda e, j: (e, 0, j)),
                   pl.BlockSpec((1, D_EXPERT, ks), lambda e, j: (e, 0, j))],
        out_shape=[out, out],
        scratch_shapes=[pltpu.VMEM((n_slab, 2 * D_EXPERT, LANES), F32)],
        compiler_params=_cparams("parallel", "parallel"),
        name="w1_split",
    )(w1)


def _expert_kernel(be_ref, nb_ref, x_ref, w1g_ref, w1l_ref, b1g_ref, b1l_ref, w2_ref, b2_ref, o_ref):
    del be_ref
    used = pl.program_id(0) < nb_ref[0]

    @pl.when(jnp.logical_not(used))
    def _():
        o_ref[...] = jnp.zeros_like(o_ref)

    @pl.when(used)
    def _():
        x = x_ref[...].astype(BF16)
        nt = (((1,), (1,)), ((), ()))
        hg = lax.dot_general(x, w1g_ref[0], nt, preferred_element_type=F32) + b1g_ref[0]
        hl = lax.dot_general(x, w1l_ref[0], nt, preferred_element_type=F32) + b1l_ref[0]
        glu = jnp.minimum(hg, SWIGLU_LIMIT)
        lin = jnp.clip(hl, -SWIGLU_LIMIT, SWIGLU_LIMIT)
        act = glu * jax.nn.sigmoid(SWIGLU_ALPHA * glu) * (lin + 1.0)
        o_ref[...] = jnp.dot(act.astype(BF16), w2_ref[0].astype(BF16), preferred_element_type=F32) + b2_ref[0]


def _experts(blk_exp, n_used, xs, w1g_t, w1l_t, b1g, b1l, w2, b2):
    n_slot = xs.shape[0]
    n_blk = n_slot // MOE_ROWS
    row_in = lambda i, be, nb: (jnp.minimum(i, nb[0] - 1), 0)
    row = lambda i, be, nb: (i, 0)
    wsel = lambda i, be, nb: (be[i], 0, 0)
    grid_spec = pltpu.PrefetchScalarGridSpec(
        num_scalar_prefetch=2,
        grid=(n_blk,),
        in_specs=[
            pl.BlockSpec((MOE_ROWS, D_MODEL), row_in),
            pl.BlockSpec((1, D_EXPERT, D_MODEL), wsel),
            pl.BlockSpec((1, D_EXPERT, D_MODEL), wsel),
            pl.BlockSpec((1, 1, D_EXPERT), wsel),
            pl.BlockSpec((1, 1, D_EXPERT), wsel),
            pl.BlockSpec((1, D_EXPERT, D_MODEL), wsel),
            pl.BlockSpec((1, 1, D_MODEL), wsel),
        ],
        out_specs=pl.BlockSpec((MOE_ROWS, D_MODEL), row),
    )
    return pl.pallas_call(
        _expert_kernel,
        grid_spec=grid_spec,
        out_shape=jax.ShapeDtypeStruct((n_slot, D_MODEL), F32),
        compiler_params=_cparams("arbitrary"),
        name="experts",
    )(blk_exp, n_used, xs, w1g_t, w1l_t, b1g, b1l, w2, b2)


def _combine_kernel(dest_ref, dnext_ref, x1_ref, g4_ref, nf_ref, ys_ref, o_ref, buf, sems, *, n):
    tt = x1_ref.shape[0]
    i = pl.program_id(0)

    def issue(d_ref, slot):
        def body(t, carry):
            for k in range(TOP_K):
                _row_copy(ys_ref, d_ref[t * TOP_K + k], buf.at[slot, k], t, sems.at[slot]).start()
            return carry

        lax.fori_loop(0, tt, body, 0, unroll=8)

    def finish(slot):
        for k in range(TOP_K):
            pltpu.make_async_copy(ys_ref.at[pl.ds(0, tt)], buf.at[slot, k], sems.at[slot]).wait()
        g4 = g4_ref[...]
        moe = g4[:, 0:1] * buf[slot, 0]
        for k in range(1, TOP_K):
            moe = moe + g4[:, k:k + 1] * buf[slot, k]
        o_ref[...] = _rms(x1_ref[...] + moe, nf_ref[...])

    @pl.when(i == 0)
    def _():
        issue(dest_ref, 0)

    for parity in range(2):
        @pl.when(i % 2 == parity)
        def _():
            @pl.when(i + 1 < n)
            def _():
                issue(dnext_ref, 1 - parity)

            finish(parity)


def _combine(dest_flat, x1, g4, nf, ys):
    n = x1.shape[0]
    tt = _tile(n, 256)
    n_t = n // tt
    row = lambda i: (i, 0)
    return pl.pallas_call(
        functools.partial(_combine_kernel, n=n_t),
        grid=(n_t,),
        in_specs=[
            pl.BlockSpec((tt * TOP_K,), lambda i: (i,), memory_space=pltpu.SMEM),
            pl.BlockSpec((tt * TOP_K,), lambda i: (jnp.minimum(i + 1, n_t - 1),), memory_space=pltpu.SMEM),
            pl.BlockSpec((tt, D_MODEL), row),
            pl.BlockSpec((tt, TOP_K), row),
            pl.BlockSpec((1, D_MODEL), lambda i: (0, 0)),
            pl.BlockSpec(memory_space=pl.ANY),
        ],
        out_specs=pl.BlockSpec((tt, D_MODEL), row),
        out_shape=jax.ShapeDtypeStruct((n, D_MODEL), F32),
        scratch_shapes=[pltpu.VMEM((2, TOP_K, tt, D_MODEL), F32), pltpu.SemaphoreType.DMA((2,))],
        compiler_params=_cparams("arbitrary"),
        name="moe_combine",
    )(dest_flat, dest_flat, x1, g4, nf, ys)


def _rope_table(pos):
    half = MLA_ROPE // 2
    inv_freq = ROPE_THETA ** (-jnp.arange(half, dtype=F32) / half)
    ang = pos.astype(F32)[:, None] * inv_freq[None, :]
    cos, sin = jnp.cos(ang), jnp.sin(ang)
    return jnp.concatenate([cos, cos, -sin, sin], axis=-1)


def _regroup_w_in(w_in):
    o = np.cumsum((Q_LORA, KV_LORA, MLA_ROPE, GLA_DK, GLA_DK, GLA_DV, GLA_GATE_RANK, GLA_DV, 2 * D_MODEL))
    c_q, c_kv, k_r = w_in[:, :o[0]], w_in[:, o[0]:o[1]], w_in[:, o[1]:o[2]]
    gq, gk, gv = w_in[:, o[2]:o[3]], w_in[:, o[3]:o[4]], w_in[:, o[4]:o[5]]
    ga, gg, mg = w_in[:, o[5]:o[6]], w_in[:, o[6]:o[7]], w_in[:, o[7]:o[8]]
    half = MLA_ROPE // 2
    k_r_sw = jnp.concatenate([k_r[:, half:], k_r[:, :half]], axis=1)
    ga_pad = jnp.zeros((D_MODEL, LANES - GLA_GATE_RANK), w_in.dtype)
    return jnp.concatenate([c_q, c_kv, k_r, k_r_sw, ga, ga_pad, gv, gg, gq, gk, mg], axis=1).astype(BF16)


def _regroup_w_uq(w_uq):
    w = w_uq.reshape(Q_LORA, MLA_HEADS, MLA_NOPE + MLA_ROPE)
    half = MLA_ROPE // 2
    nope, rope = w[..., :MLA_NOPE], w[..., MLA_NOPE:]
    rope_sw = jnp.concatenate([rope[..., half:], rope[..., :half]], axis=-1)
    return jnp.concatenate([nope, rope, rope_sw], axis=-1).reshape(Q_LORA, MLA_HEADS * HEAD_PAD).astype(BF16)


def _regroup_w_ukv(w_ukv):
    w = w_ukv.reshape(KV_LORA, MLA_HEADS, MLA_NOPE + MLA_V)
    k = w[..., :MLA_NOPE].reshape(KV_LORA, MLA_HEADS * MLA_NOPE)
    v = w[..., MLA_NOPE:].reshape(KV_LORA, MLA_HEADS * MLA_V)
    return jnp.concatenate([k, v], axis=1).astype(BF16)


def _slot_plan(counts_first, counts_rest, n_blk):
    counts_first = counts_first.astype(jnp.int32)
    counts = counts_first + counts_rest.astype(jnp.int32)
    padded = (counts + MOE_ROWS - 1) // MOE_ROWS * MOE_ROWS
    pad_end = jnp.cumsum(padded)
    pad_start = pad_end - padded
    n_used = (pad_end[-1] // MOE_ROWS).reshape(1)
    blk_start = jnp.arange(n_blk, dtype=jnp.int32) * MOE_ROWS
    blk_exp = jnp.sum(blk_start[:, None] >= pad_end[None, :], axis=1).astype(jnp.int32)
    last_exp = blk_exp[jnp.maximum(n_used[0] - 1, 0)]
    blk_exp = jnp.where(jnp.arange(n_blk) < n_used[0], blk_exp, last_exp)
    ps_row = jnp.pad(pad_start.astype(F32), (0, LANES - N_EXPERTS))[None, :]
    zero_first = jnp.concatenate([(pad_start + counts_first) // MOE_ROWS, n_used]).astype(jnp.int32)
    zero_last = jnp.concatenate([pad_end // MOE_ROWS, jnp.full((1,), n_blk)]).astype(jnp.int32)
    return ps_row, zero_first, zero_last, blk_exp, n_used


def _mixers(x, past_lat, past_kr, s0, past_len, p):
    bsz, seq, _ = x.shape
    n = bsz * seq
    x2d = x.reshape(n, D_MODEL)
    hin = _in_proj(x2d, p["norm1_g"], p["w_all"])
    tm = min(512, n)
    cs = _rope_table(past_len + jnp.arange(seq))
    if seq < tm:
        cs = jnp.tile(cs, (tm // seq, 1))
    q, lat_new, krp_new = _mla_q(hin, cs, p["q_norm_g"], p["kv_norm_g"], p["w_uq"])
    if past_len:
        lat_all = jnp.concatenate([past_lat, lat_new.reshape(bsz, seq, KV_LORA)], axis=1)
        kr_pad = jnp.pad(past_kr, ((0, 0), (0, 0), (0, LANES - MLA_ROPE)))
        krp_all = jnp.concatenate([kr_pad, krp_new.reshape(bsz, seq, LANES)], axis=1)
    else:
        lat_all, krp_all = lat_new, krp_new
    sk = past_len + seq
    kcat, v = _kv_up(lat_all.reshape(bsz * sk, KV_LORA), krp_all.reshape(bsz * sk, LANES), p["w_ukv"])
    tq = min(1024, seq)
    tk = 1024 if sk % 1024 == 0 else sk
    attn = _flash(q.reshape(bsz, seq, -1), kcat.reshape(bsz, sk, -1), v.reshape(bsz, sk, -1), past_len, tq, tk)
    og, s_new = _gla(hin, p["w_alpha"], p["b_alpha"], p["gla_norm_g"], s0, bsz, seq)
    x1, hm, sel, g4, cnt = _post_mix(attn.reshape(n, -1), og, hin, x2d, p["w_o_mla"], p["w_o_gla"], p["w_out"],
                                     p["norm2_g"], p["w_router"], p["b_router"])
    new_state = (lat_new.reshape(bsz, seq, KV_LORA), krp_new[:, :MLA_ROPE].reshape(bsz, seq, MLA_ROPE), s_new)
    return x1, hm, sel, g4, cnt, new_state


def kernel(x_prompt, x_sample, cache_mla_latent, cache_mla_krope, state_gla, norm1_g, w_in, q_norm_g, kv_norm_g,
           w_uq, w_ukv, w_o_mla, w_alpha2, b_alpha, gla_norm_g, w_o_gla, w_out, norm2_g, w_router, b_router,
           w1, b1, w2, b2, normf_g):
    depth = w_in.shape[0]
    assert depth == 1
    l = 0
    bp, sp, _ = x_prompt.shape
    bs, ss, _ = x_sample.shape
    past_len = cache_mla_latent.shape[2]
    p = {
        "norm1_g": norm1_g[l][None, :],
        "w_all": _regroup_w_in(w_in[l]),
        "q_norm_g": q_norm_g[l][None, :],
        "kv_norm_g": kv_norm_g[l][None, :],
        "w_uq": _regroup_w_uq(w_uq[l]),
        "w_ukv": _regroup_w_ukv(w_ukv[l]),
        "w_o_mla": w_o_mla[l].astype(BF16),
        "w_alpha": jnp.pad(w_alpha2[l], ((0, LANES - GLA_GATE_RANK), (0, 0))).astype(BF16),
        "b_alpha": b_alpha[l][None, :],
        "gla_norm_g": gla_norm_g[l][None, :],
        "w_o_gla": w_o_gla[l].astype(BF16),
        "w_out": w_out[l].astype(BF16),
        "norm2_g": norm2_g[l][None, :],
        "w_router": jnp.pad(w_router[l], ((0, 0), (0, LANES - N_EXPERTS))).astype(BF16),
        "b_router": jnp.pad(b_router[l], (0, LANES - N_EXPERTS))[None, :],
    }
    zeros_state = jnp.zeros((bp, GLA_HEADS, GLA_DK_HEAD, GLA_DV_HEAD), F32)
    x1p, hmp, selp, g4p, cntp, (lat_p, kr_p, st_p) = _mixers(x_prompt, None, None, zeros_state, 0, p)
    x1s, hms, sels, g4s, cnts, (lat_s, kr_s, st_s) = _mixers(x_sample, cache_mla_latent[l], cache_mla_krope[l],
                                                              state_gla[l], past_len, p)

    n_p, n_s = bp * sp, bs * ss
    n_asg = (n_p + n_s) * TOP_K
    n_slot = -(-n_asg // MOE_ROWS) * MOE_ROWS + N_EXPERTS * MOE_ROWS
    ps_row, zero_first, zero_last, blk_exp, n_used = _slot_plan(cntp[0, :N_EXPERTS], cnts[0, :N_EXPERTS],
                                                                n_slot // MOE_ROWS)
    dest = _dest(jnp.concatenate([selp, sels], axis=0), ps_row).reshape(n_asg)
    dest_p, dest_s = dest[:n_p * TOP_K], dest[n_p * TOP_K:]
    xs = _scatter_init(zero_first, zero_last, dest_p, hmp, n_slot)
    xs = _scatter_more(dest_s, hms, xs)
    w1g_t, w1l_t = _w1_split(w1[l])
    b1g = b1[l][:, None, 0::2]
    b1lin = b1[l][:, None, 1::2]
    ys_slots = _experts(blk_exp, n_used, xs, w1g_t, w1l_t, b1g, b1lin, w2[l], b2[l][:, None, :])
    yp = _combine(dest_p, x1p, g4p, normf_g[None, :], ys_slots).reshape(bp, sp, D_MODEL)
    ys = _combine(dest_s, x1s, g4s, normf_g[None, :], ys_slots).reshape(bs, ss, D_MODEL)
    return (yp, ys, lat_p[None], kr_p[None], st_p[None], lat_s[None], kr_s[None], st_s[None])
```

```python
import functools

import jax
import jax.numpy as jnp
import numpy as np
from jax import lax
from jax.experimental import pallas as pl
from jax.experimental.pallas import tpu as pltpu

F32 = jnp.float32
BF16 = jnp.bfloat16

D_MODEL = 1024
CHUNK = 64
CHUNK_SHIFT = 6
LOG2_E = 1.4426950408889634
NORM_EPS = 1e-6
MLA_HEADS = 8
MLA_NOPE = 128
MLA_ROPE = 64
MLA_V = 128
Q_LORA = 512
KV_LORA = 256
ROPE_THETA = 10000.0
MLA_SCALE = (MLA_NOPE + MLA_ROPE) ** -0.5
NEG_INF = -1e30
GLA_HEADS = 4
GLA_DK = D_MODEL // 2
GLA_DV = D_MODEL
GLA_DK_HEAD = GLA_DK // GLA_HEADS
GLA_DV_HEAD = GLA_DV // GLA_HEADS
GLA_GATE_RANK = 16
GLA_GATE_TEMP = 16.0
GLA_SUB = 16
FLASH_HEADS = 2
POST_MIX_PARTS = 2
N_EXPERTS = 32
TOP_K = 4
D_EXPERT = D_MODEL
SWIGLU_LIMIT = 7.0
SWIGLU_ALPHA = 1.702
MOE_ROWS = 512

LANES = 128
HEAD_PAD = 256

COL_CQ = 0
COL_CKV = 512
COL_KR = 768
COL_GA = 896
COL_GV = 1024
COL_GG = 2048
COL_GQ = 3072
COL_GK = 3584
COL_MG = 4096
D_IN_PAD = 6144

VMEM_LIMIT = 56 * 1024 * 1024


def _cparams(*sem, flags=None):
    return pltpu.CompilerParams(dimension_semantics=sem, vmem_limit_bytes=VMEM_LIMIT, flags=flags)


def _tile(n, pref, mult=16):
    t = min(pref, n)
    t -= t % mult
    while n % t:
        t -= mult
    return t


def _rms(x, g):
    return x * lax.rsqrt(jnp.mean(x * x, axis=-1, keepdims=True) + NORM_EPS) * g


def _in_proj_kernel(x_ref, g_ref, w_ref, o_ref):
    h = _rms(x_ref[...], g_ref[...]).astype(BF16)
    tn = 1536
    for j in range(D_IN_PAD // tn):
        o_ref[:, j * tn:(j + 1) * tn] = jnp.dot(h, w_ref[:, j * tn:(j + 1) * tn], preferred_element_type=F32)


def _in_proj(x2d, g, w_all):
    n = x2d.shape[0]
    tm = min(256, n)
    return pl.pallas_call(
        _in_proj_kernel,
        grid=(n // tm,),
        in_specs=[
            pl.BlockSpec((tm, D_MODEL), lambda i: (i, 0)),
            pl.BlockSpec((1, D_MODEL), lambda i: (0, 0)),
            pl.BlockSpec((D_MODEL, D_IN_PAD), lambda i: (0, 0)),
        ],
        out_specs=pl.BlockSpec((tm, D_IN_PAD), lambda i: (i, 0)),
        out_shape=jax.ShapeDtypeStruct((n, D_IN_PAD), F32),
        compiler_params=_cparams("parallel"),
        name="in_proj",
    )(x2d, g, w_all)


def _rope_pair(blk, cs):
    t = blk * cs
    return t + pltpu.roll(t, MLA_ROPE, axis=1)


def _mla_q_kernel(h_ref, cs_ref, qg_ref, kvg_ref, wuq_ref, q_ref, lat_ref, kr_ref):
    cs = cs_ref[...]
    qn = _rms(h_ref[:, COL_CQ:COL_CQ + Q_LORA], qg_ref[...]).astype(BF16)
    q = jnp.dot(qn, wuq_ref[...], preferred_element_type=F32)
    for hh in range(MLA_HEADS):
        c0 = hh * HEAD_PAD
        q_ref[:, c0:c0 + MLA_NOPE] = q[:, c0:c0 + MLA_NOPE].astype(BF16)
        q_ref[:, c0 + MLA_NOPE:c0 + HEAD_PAD] = _rope_pair(q[:, c0 + MLA_NOPE:c0 + HEAD_PAD], cs).astype(BF16)
    lat_ref[...] = _rms(h_ref[:, COL_CKV:COL_CKV + KV_LORA], kvg_ref[...])
    r = _rope_pair(h_ref[:, COL_KR:COL_KR + LANES], cs)
    lane = lax.broadcasted_iota(jnp.int32, r.shape, 1)
    kr_ref[...] = jnp.where(lane < MLA_ROPE, r, 0.0)


def _mla_q(hin, cs, qg, kvg, wuq):
    n = hin.shape[0]
    tm = min(512, n)
    n_cs = cs.shape[0] // tm
    return pl.pallas_call(
        _mla_q_kernel,
        grid=(n // tm,),
        in_specs=[
            pl.BlockSpec((tm, COL_GA), lambda i: (i, 0)),
            pl.BlockSpec((tm, LANES), lambda i: (i % n_cs, 0)),
            pl.BlockSpec((1, Q_LORA), lambda i: (0, 0)),
            pl.BlockSpec((1, KV_LORA), lambda i: (0, 0)),
            pl.BlockSpec((Q_LORA, MLA_HEADS * HEAD_PAD), lambda i: (0, 0)),
        ],
        out_specs=[
            pl.BlockSpec((tm, MLA_HEADS * HEAD_PAD), lambda i: (i, 0)),
            pl.BlockSpec((tm, KV_LORA), lambda i: (i, 0)),
            pl.BlockSpec((tm, LANES), lambda i: (i, 0)),
        ],
        out_shape=[
            jax.ShapeDtypeStruct((n, MLA_HEADS * HEAD_PAD), BF16),
            jax.ShapeDtypeStruct((n, KV_LORA), F32),
            jax.ShapeDtypeStruct((n, LANES), F32),
        ],
        compiler_params=_cparams("parallel"),
        name="mla_q",
    )(hin, cs, qg, kvg, wuq)


def _kv_up_kernel(lat_ref, kr_ref, w_ref, k_ref, v_ref):
    kv = jnp.dot(lat_ref[...].astype(BF16), w_ref[...], preferred_element_type=F32)
    krb = kr_ref[...].astype(BF16)
    for hh in range(MLA_HEADS):
        c0 = hh * HEAD_PAD
        k_ref[:, c0:c0 + MLA_NOPE] = kv[:, hh * MLA_NOPE:(hh + 1) * MLA_NOPE].astype(BF16)
        k_ref[:, c0 + MLA_NOPE:c0 + HEAD_PAD] = krb
    v_ref[...] = kv[:, MLA_HEADS * MLA_NOPE:].astype(BF16)


def _kv_up(lat, krp, wukv):
    n = lat.shape[0]
    tm = _tile(n, 512)
    return pl.pallas_call(
        _kv_up_kernel,
        grid=(n // tm,),
        in_specs=[
            pl.BlockSpec((tm, KV_LORA), lambda i: (i, 0)),
            pl.BlockSpec((tm, LANES), lambda i: (i, 0)),
            pl.BlockSpec((KV_LORA, MLA_HEADS * (MLA_NOPE + MLA_V)), lambda i: (0, 0)),
        ],
        out_specs=[
            pl.BlockSpec((tm, MLA_HEADS * HEAD_PAD), lambda i: (i, 0)),
            pl.BlockSpec((tm, MLA_HEADS * MLA_V), lambda i: (i, 0)),
        ],
        out_shape=[
            jax.ShapeDtypeStruct((n, MLA_HEADS * HEAD_PAD), BF16),
            jax.ShapeDtypeStruct((n, MLA_HEADS * MLA_V), BF16),
        ],
        compiler_params=_cparams("parallel"),
        name="kv_up",
    )(lat, krp, wukv)


def _flash_kernel(q_ref, k_ref, v_ref, o_ref, *, tq, tk, past_len):
    qi = pl.program_id(2)
    q_pos0 = past_len + qi * tq
    q_chunk = (q_pos0 + lax.broadcasted_iota(jnp.int32, (tq, 1), 0)) >> CHUNK_SHIFT
    n_blocks = (q_pos0 + tq + tk - 1) // tk
    n_full = ((q_pos0 >> CHUNK_SHIFT) << CHUNK_SHIFT) // tk
    qs = [q_ref[0, :, hh * HEAD_PAD:(hh + 1) * HEAD_PAD] for hh in range(FLASH_HEADS)]

    def step(j, carry, masked):
        k0 = pl.multiple_of(j * tk, tk)
        if masked:
            k_chunk = (k0 + lax.broadcasted_iota(jnp.int32, (1, tk), 1)) >> CHUNK_SHIFT
        out = []
        for hh in range(FLASH_HEADS):
            m, l, acc = carry[hh]
            k = k_ref[0, pl.ds(k0, tk), hh * HEAD_PAD:(hh + 1) * HEAD_PAD]
            v = v_ref[0, pl.ds(k0, tk), hh * MLA_V:(hh + 1) * MLA_V]
            s = lax.dot_general(qs[hh], k, (((1,), (1,)), ((), ())), preferred_element_type=F32)
            s = s * (MLA_SCALE * LOG2_E)
            if masked:
                s = jnp.where(k_chunk <= q_chunk, s, NEG_INF)
            m_new = jnp.maximum(m, jnp.max(s, axis=-1, keepdims=True))
            alpha = jnp.exp2(m - m_new)
            p = jnp.exp2(s - m_new)
            l = alpha * l + jnp.sum(p, axis=-1, keepdims=True)
            acc = alpha * acc + jnp.dot(p.astype(BF16), v, preferred_element_type=F32)
            out.append((m_new, l, acc))
        return tuple(out)

    init = tuple((jnp.full((tq, 1), NEG_INF, F32), jnp.zeros((tq, 1), F32), jnp.zeros((tq, MLA_V), F32))
                 for _ in range(FLASH_HEADS))
    carry = lax.fori_loop(0, n_full, functools.partial(step, masked=False), init)
    carry = lax.fori_loop(n_full, n_blocks, functools.partial(step, masked=True), carry)
    for hh in range(FLASH_HEADS):
        _, l, acc = carry[hh]
        o_ref[0, :, hh * MLA_V:(hh + 1) * MLA_V] = (acc / l).astype(BF16)


def _flash(q, k, v, past_len, tq, tk):
    b, sq, _ = q.shape
    sk = k.shape[1]
    fh = FLASH_HEADS
    return pl.pallas_call(
        functools.partial(_flash_kernel, tq=tq, tk=tk, past_len=past_len),
        grid=(b, MLA_HEADS // fh, sq // tq),
        in_specs=[
            pl.BlockSpec((1, tq, fh * HEAD_PAD), lambda bi, h, i: (bi, i, h)),
            pl.BlockSpec((1, sk, fh * HEAD_PAD), lambda bi, h, i: (bi, 0, h)),
            pl.BlockSpec((1, sk, fh * MLA_V), lambda bi, h, i: (bi, 0, h)),
        ],
        out_specs=pl.BlockSpec((1, tq, fh * MLA_V), lambda bi, h, i: (bi, i, h)),
        out_shape=jax.ShapeDtypeStruct((b, sq, MLA_HEADS * MLA_V), BF16),
        compiler_params=_cparams("parallel", "parallel", "arbitrary"),
        name="flash",
    )(q, k, v)


def _mla_cached_kernel(q_ref, clat_ref, ckr_ref, nlat_ref, nkr_ref, wuk_ref, wuv_ref, o_ref, *, past_len):
    seq = q_ref.shape[0]
    nt = (((1,), (1,)), ((), ()))
    q_abs = jnp.concatenate(
        [jnp.dot(q_ref[:, hh * HEAD_PAD:hh * HEAD_PAD + MLA_NOPE], wuk_ref[hh], preferred_element_type=F32)
         for hh in range(MLA_HEADS)], axis=0).astype(BF16)
    q_rot = jnp.concatenate([q_ref[:, hh * HEAD_PAD + MLA_NOPE:(hh + 1) * HEAD_PAD] for hh in range(MLA_HEADS)],
                            axis=0)
    pos_q = past_len + lax.broadcasted_iota(jnp.int32, (seq, 1), 0)
    q_chunk = jnp.concatenate([pos_q] * MLA_HEADS, axis=0) >> CHUNK_SHIFT

    def scores(lat, kr, k_pos0):
        n_k = lat.shape[0]
        kr_pad = jnp.concatenate([kr, jnp.zeros_like(kr)], axis=1)
        s = (lax.dot_general(q_abs, lat, nt, preferred_element_type=F32)
             + lax.dot_general(q_rot, kr_pad, nt, preferred_element_type=F32)) * (MLA_SCALE * LOG2_E)
        k_chunk = (k_pos0 + lax.broadcasted_iota(jnp.int32, (1, n_k), 1)) >> CHUNK_SHIFT
        return jnp.where(k_chunk <= q_chunk, s, NEG_INF)

    lat_c = clat_ref[0].astype(BF16)
    lat_n = nlat_ref[...].astype(BF16)
    s_c = scores(lat_c, ckr_ref[0].astype(BF16), 0)
    s_n = scores(lat_n, nkr_ref[:, :MLA_ROPE].astype(BF16), past_len)
    m = jnp.maximum(jnp.max(s_c, axis=-1, keepdims=True), jnp.max(s_n, axis=-1, keepdims=True))
    p_c = jnp.exp2(s_c - m)
    p_n = jnp.exp2(s_n - m)
    l = jnp.sum(p_c, axis=-1, keepdims=True) + jnp.sum(p_n, axis=-1, keepdims=True)
    o_lat = (jnp.dot(p_c.astype(BF16), lat_c, preferred_element_type=F32)
             + jnp.dot(p_n.astype(BF16), lat_n, preferred_element_type=F32)) / l
    o_lat = o_lat.astype(BF16)
    for hh in range(MLA_HEADS):
        o_ref[:, hh * MLA_V:(hh + 1) * MLA_V] = jnp.dot(o_lat[hh * seq:(hh + 1) * seq], wuv_ref[hh],
                                                         preferred_element_type=F32).astype(BF16)


def _mla_cached(q, cache_lat, cache_kr, lat_new, krp_new, w_uk, w_uv, bsz, seq):
    past_len = cache_lat.shape[1]
    return pl.pallas_call(
        functools.partial(_mla_cached_kernel, past_len=past_len),
        grid=(bsz,),
        in_specs=[
            pl.BlockSpec((seq, MLA_HEADS * HEAD_PAD), lambda b: (b, 0)),
            pl.BlockSpec((1, past_len, KV_LORA), lambda b: (b, 0, 0)),
            pl.BlockSpec((1, past_len, MLA_ROPE), lambda b: (b, 0, 0)),
            pl.BlockSpec((seq, KV_LORA), lambda b: (b, 0)),
            pl.BlockSpec((seq, LANES), lambda b: (b, 0)),
            pl.BlockSpec((MLA_HEADS, MLA_NOPE, KV_LORA), lambda b: (0, 0, 0)),
            pl.BlockSpec((MLA_HEADS, KV_LORA, MLA_V), lambda b: (0, 0, 0)),
        ],
        out_specs=pl.BlockSpec((seq, MLA_HEADS * MLA_V), lambda b: (b, 0)),
        out_shape=jax.ShapeDtypeStruct((bsz * seq, MLA_HEADS * MLA_V), BF16),
        compiler_params=_cparams("parallel"),
        name="mla_cached",
    )(q, cache_lat, cache_kr, lat_new, krp_new, w_uk, w_uv)


def _cumsum_rows(x):
    n = x.shape[0]
    row = lax.broadcasted_iota(jnp.int32, x.shape, 0)
    s = 1
    while s < n:
        x = x + jnp.where(row >= s, pltpu.roll(x, s, axis=0), 0.0)
        s *= 2
    return x


def _gla_kernel(gv_ref, gg_ref, gq_ref, gk_ref, ga_ref, wa_ref, ba_ref, ng_ref, s0_ref, og_ref, sn_ref, st_scr,
                *, csize, n_chunks):
    t = pl.program_id(1)

    @pl.when(t == 0)
    def _():
        for hh in range(GLA_HEADS):
            st_scr[hh] = s0_ref[0, hh].T

    wa = wa_ref[...]
    ba = ba_ref[...]
    ng = ng_ref[...]
    n_sub = csize // GLA_SUB
    col_id = lax.broadcasted_iota(jnp.int32, (GLA_SUB, csize), 1)
    row_id = lax.broadcasted_iota(jnp.int32, (GLA_SUB, GLA_DK_HEAD), 0)

    def chunk(c, carry):
        r0 = pl.multiple_of(c * csize, csize)
        rows = pl.ds(r0, csize)
        x = jnp.dot(ga_ref[rows, :].astype(BF16), wa, preferred_element_type=F32) + ba
        la = jax.nn.log_sigmoid(x) / GLA_GATE_TEMP
        b_all = _cumsum_rows(la)
        for hh in range(GLA_HEADS):
            head(hh, rows, b_all[:, hh * GLA_DK_HEAD:(hh + 1) * GLA_DK_HEAD])
        return carry

    def head(hh, rows, b):
        kcols = slice(hh * GLA_DK_HEAD, (hh + 1) * GLA_DK_HEAD)
        vcols = slice(hh * GLA_DV_HEAD, (hh + 1) * GLA_DV_HEAD)
        q = gq_ref[rows, kcols] * (GLA_DK_HEAD ** -0.5)
        k = gk_ref[rows, kcols]
        v = gv_ref[rows, vcols]

        a_rows = []
        for i in range(n_sub):
            lo = i * GLA_SUB
            bi = b[lo:lo + GLA_SUB]
            qi = q[lo:lo + GLA_SUB]
            ki = k[lo:lo + GLA_SUB]
            if i == 0:
                a_i = jnp.zeros((GLA_SUB, csize), F32)
            else:
                ri = b[lo - 1:lo]
                qs = qi * jnp.exp(bi - ri)
                ks = k * jnp.exp(jnp.minimum(ri - b, 0.0))
                a_i = lax.dot_general(qs.astype(BF16), ks.astype(BF16), (((1,), (1,)), ((), ())),
                                      preferred_element_type=F32)
                a_i = jnp.where(col_id < lo, a_i, 0.0)
            for s in range(GLA_SUB):
                e = jnp.where(row_id >= s, jnp.exp(bi - bi[s:s + 1]), 0.0)
                col = jnp.sum(qi * ki[s:s + 1] * e, axis=-1, keepdims=True)
                a_i = jnp.where(col_id == lo + s, col, a_i)
            a_rows.append(a_i)
        a = jnp.concatenate(a_rows, axis=0)

        st = st_scr[hh]
        vb = v.astype(BF16)
        o = jnp.dot(a.astype(BF16), vb, preferred_element_type=F32)
        o = o + lax.dot_general((q * jnp.exp(b)).astype(BF16), st.astype(BF16), (((1,), (1,)), ((), ())),
                                preferred_element_type=F32)
        b_end = b[csize - 1:csize]
        kd = (k * jnp.exp(b_end - b)).astype(BF16)
        st_scr[hh] = jnp.exp(b_end) * st + jnp.dot(v.T.astype(BF16), kd, preferred_element_type=F32)
        on = _rms(o, ng)
        og_ref[rows, vcols] = (on * jax.nn.silu(gg_ref[rows, vcols])).astype(BF16)

    lax.fori_loop(0, n_chunks, chunk, 0)

    @pl.when(t == pl.num_programs(1) - 1)
    def _():
        for hh in range(GLA_HEADS):
            sn_ref[0, hh] = st_scr[hh].T


def _gla(hin, wa, ba, ng, s0, bsz, seq):
    csize = min(CHUNK, seq)
    tt = min(512, seq)
    n_t = seq // tt
    n = bsz * seq
    kern = functools.partial(_gla_kernel, csize=csize, n_chunks=tt // csize)
    state = pl.BlockSpec((1, GLA_HEADS, GLA_DK_HEAD, GLA_DV_HEAD), lambda b, t: (b, 0, 0, 0))
    return pl.pallas_call(
        kern,
        grid=(bsz, n_t),
        in_specs=[
            pl.BlockSpec((tt, GLA_DV), lambda b, t: (b * n_t + t, COL_GV // GLA_DV)),
            pl.BlockSpec((tt, GLA_DV), lambda b, t: (b * n_t + t, COL_GG // GLA_DV)),
            pl.BlockSpec((tt, GLA_DK), lambda b, t: (b * n_t + t, COL_GQ // GLA_DK)),
            pl.BlockSpec((tt, GLA_DK), lambda b, t: (b * n_t + t, COL_GK // GLA_DK)),
            pl.BlockSpec((tt, LANES), lambda b, t: (b * n_t + t, COL_GA // LANES)),
            pl.BlockSpec((LANES, GLA_DK), lambda b, t: (0, 0)),
            pl.BlockSpec((1, GLA_DK), lambda b, t: (0, 0)),
            pl.BlockSpec((1, GLA_DV_HEAD), lambda b, t: (0, 0)),
            state,
        ],
        out_specs=[pl.BlockSpec((tt, GLA_DV), lambda b, t: (b * n_t + t, 0)), state],
        out_shape=[
            jax.ShapeDtypeStruct((n, GLA_DV), BF16),
            jax.ShapeDtypeStruct((bsz, GLA_HEADS, GLA_DK_HEAD, GLA_DV_HEAD), F32),
        ],
        scratch_shapes=[pltpu.VMEM((GLA_HEADS, GLA_DV_HEAD, GLA_DK_HEAD), F32)],
        compiler_params=_cparams("parallel", "arbitrary"),
        name="gla",
    )(hin, hin, hin, hin, hin, wa, ba, ng, s0)


def _post_mix_kernel(at_ref, og_ref, mg1_ref, mg2_ref, x_ref, wom_ref, wog_ref, wout_ref, n2_ref, wr_ref, br_ref,
                     x1_ref, hm_ref, sel_ref, g4_ref, cnt_ref):
    @pl.when(pl.program_id(0) == 0)
    def _():
        cnt_ref[...] = jnp.zeros_like(cnt_ref)

    tm = x_ref.shape[0]
    half = tm // POST_MIX_PARTS
    for part in range(POST_MIX_PARTS):
        rows = slice(part * half, (part + 1) * half)
        y_mla = jnp.dot(at_ref[rows, :], wom_ref[...], preferred_element_type=F32)
        y_gla = jnp.dot(og_ref[rows, :], wog_ref[...], preferred_element_type=F32)
        m = jax.nn.sigmoid(mg1_ref[rows, :]) * y_mla + jax.nn.sigmoid(mg2_ref[rows, :]) * y_gla
        x1 = x_ref[rows, :] + jnp.dot(m.astype(BF16), wout_ref[...], preferred_element_type=F32)
        x1_ref[rows, :] = x1
        hm = _rms(x1, n2_ref[...])
        hm_ref[rows, :] = hm
        logits = jnp.dot(hm.astype(BF16), wr_ref[...], preferred_element_type=F32) + br_ref[...]

        lane = lax.broadcasted_iota(jnp.int32, logits.shape, 1)
        work = jnp.where(lane < N_EXPERTS, logits, -jnp.inf)
        sel = jnp.zeros(logits.shape, F32)
        vals = []
        for k in range(TOP_K):
            mk = jnp.max(work, axis=-1, keepdims=True)
            ik = jnp.min(jnp.where(work == mk, lane, LANES), axis=-1, keepdims=True)
            hit = lane == ik
            sel = jnp.where(hit, float(k + 1), sel)
            work = jnp.where(hit, -jnp.inf, work)
            vals.append(mk)
        sel_ref[rows, :] = sel
        ex = [jnp.exp(v - vals[0]) for v in vals]
        den = ex[0]
        for e in ex[1:]:
            den = den + e
        g4 = jnp.zeros(logits.shape, F32)
        for k in range(TOP_K):
            g4 = jnp.where(lane == k, ex[k] / den, g4)
        g4_ref[rows, :] = g4[:, :TOP_K]
        cnt_ref[...] += jnp.sum(jnp.where(sel > 0.0, 1.0, 0.0), axis=0, keepdims=True)


def _post_mix(attn, og, hin, x2d, wom, wog, wout, n2, wr, br):
    n = x2d.shape[0]
    tm = min(256 * POST_MIX_PARTS, n)
    row = lambda i: (i, 0)
    const = lambda i: (0, 0)
    return pl.pallas_call(
        _post_mix_kernel,
        grid=(n // tm,),
        in_specs=[
            pl.BlockSpec((tm, D_MODEL), row),
            pl.BlockSpec((tm, D_MODEL), row),
            pl.BlockSpec((tm, D_MODEL), lambda i: (i, COL_MG // D_MODEL)),
            pl.BlockSpec((tm, D_MODEL), lambda i: (i, COL_MG // D_MODEL + 1)),
            pl.BlockSpec((tm, D_MODEL), row),
            pl.BlockSpec((D_MODEL, D_MODEL), const),
            pl.BlockSpec((D_MODEL, D_MODEL), const),
            pl.BlockSpec((D_MODEL, D_MODEL), const),
            pl.BlockSpec((1, D_MODEL), const),
            pl.BlockSpec((D_MODEL, LANES), const),
            pl.BlockSpec((1, LANES), const),
        ],
        out_specs=[
            pl.BlockSpec((tm, D_MODEL), row),
            pl.BlockSpec((tm, D_MODEL), row),
            pl.BlockSpec((tm, LANES), row),
            pl.BlockSpec((tm, TOP_K), row),
            pl.BlockSpec((1, LANES), const),
        ],
        out_shape=[
            jax.ShapeDtypeStruct((n, D_MODEL), F32),
            jax.ShapeDtypeStruct((n, D_MODEL), F32),
            jax.ShapeDtypeStruct((n, LANES), F32),
            jax.ShapeDtypeStruct((n, TOP_K), F32),
            jax.ShapeDtypeStruct((1, LANES), F32),
        ],
        compiler_params=_cparams("arbitrary"),
        name="post_mix",
    )(attn, og, hin, hin, x2d, wom, wog, wout, n2, wr, br)


def _dest_kernel(sel_ref, ps_ref, dest_ref, run_scr):
    @pl.when(pl.program_id(0) == 0)
    def _():
        run_scr[...] = ps_ref[...]

    sel = sel_ref[...]
    tt = sel.shape[0]
    oh = jnp.where(sel > 0.0, 1.0, 0.0)
    r = lax.broadcasted_iota(jnp.int32, (tt, tt), 0)
    c = lax.broadcasted_iota(jnp.int32, (tt, tt), 1)
    ltri = jnp.where(r > c, 1.0, 0.0).astype(BF16)
    slot = jnp.dot(ltri, oh.astype(BF16), preferred_element_type=F32) + run_scr[...]
    run_scr[...] += jnp.sum(oh, axis=0, keepdims=True)
    lane = lax.broadcasted_iota(jnp.int32, sel.shape, 1)
    d = jnp.zeros(sel.shape, F32)
    for k in range(TOP_K):
        col = jnp.sum(jnp.where(sel == float(k + 1), slot, 0.0), axis=-1, keepdims=True)
        d = jnp.where(lane == k, col, d)
    dest_ref[...] = d[:, :TOP_K].astype(jnp.int32)


def _dest(sel, pad_start):
    n = sel.shape[0]
    tt = _tile(n, 512)
    return pl.pallas_call(
        _dest_kernel,
        grid=(n // tt,),
        in_specs=[pl.BlockSpec((tt, LANES), lambda i: (i, 0)), pl.BlockSpec((1, LANES), lambda i: (0, 0))],
        out_specs=pl.BlockSpec((tt, TOP_K), lambda i: (i, 0)),
        out_shape=jax.ShapeDtypeStruct((n, TOP_K), jnp.int32),
        scratch_shapes=[pltpu.VMEM((1, LANES), F32)],
        compiler_params=_cparams("arbitrary"),
        name="route_dest",
    )(sel, pad_start)


def _row_copy(src, src_row, dst, dst_row, sem):
    return pltpu.make_async_copy(src.at[pl.ds(src_row, 1)], dst.at[pl.ds(dst_row, 1)], sem)


def _scatter_rows(dest_ref, hm_ref, xs_ref, sem):
    tt = hm_ref.shape[0]

    def issue(t, carry):
        for k in range(TOP_K):
            _row_copy(hm_ref, t, xs_ref, dest_ref[t * TOP_K + k], sem).start()
        return carry

    lax.fori_loop(0, tt, issue, 0)
    for k in range(TOP_K):
        pltpu.make_async_copy(hm_ref, xs_ref.at[pl.ds(0, tt)], sem).wait()


def _scatter_init_kernel(zf_ref, zl_ref, dest_ref, hm_ref, xs_ref, zero_scr, sem, zsem):
    @pl.when(pl.program_id(0) == 0)
    def _():
        zero_scr[...] = jnp.zeros_like(zero_scr)

        def zero_block(i):
            return pltpu.make_async_copy(zero_scr, xs_ref.at[pl.ds(pl.multiple_of(i * MOE_ROWS, MOE_ROWS), MOE_ROWS)],
                                         zsem)

        def start_range(e, carry):
            return lax.fori_loop(zf_ref[e], zl_ref[e], lambda i, c: (zero_block(i).start(), c)[1], carry)

        def wait_range(e, carry):
            return lax.fori_loop(zf_ref[e], zl_ref[e], lambda i, c: (zero_block(i).wait(), c)[1], carry)

        lax.fori_loop(0, N_EXPERTS + 1, start_range, 0)
        lax.fori_loop(0, N_EXPERTS + 1, wait_range, 0)

    _scatter_rows(dest_ref, hm_ref, xs_ref, sem)


def _scatter_more_kernel(dest_ref, hm_ref, xs_in, xs_ref, sem):
    del xs_in
    _scatter_rows(dest_ref, hm_ref, xs_ref, sem)


def _scatter_init(zero_first, zero_last, dest_flat, hm, n_slot):
    n = hm.shape[0]
    tt = _tile(n, 512)
    grid_spec = pltpu.PrefetchScalarGridSpec(
        num_scalar_prefetch=2,
        grid=(n // tt,),
        in_specs=[
            pl.BlockSpec((tt * TOP_K,), lambda i, pe, nb: (i,), memory_space=pltpu.SMEM),
            pl.BlockSpec((tt, D_MODEL), lambda i, pe, nb: (i, 0)),
        ],
        out_specs=pl.BlockSpec(memory_space=pl.ANY),
        scratch_shapes=[pltpu.VMEM((MOE_ROWS, D_MODEL), F32), pltpu.SemaphoreType.DMA(()),
                        pltpu.SemaphoreType.DMA(())],
    )
    return pl.pallas_call(
        _scatter_init_kernel,
        grid_spec=grid_spec,
        out_shape=jax.ShapeDtypeStruct((n_slot, D_MODEL), F32),
        compiler_params=_cparams("arbitrary"),
        name="moe_scatter_init",
    )(zero_first, zero_last, dest_flat, hm)


def _scatter_more(dest_flat, hm, xs):
    n = hm.shape[0]
    tt = _tile(n, 256)
    return pl.pallas_call(
        _scatter_more_kernel,
        grid=(n // tt,),
        in_specs=[
            pl.BlockSpec((tt * TOP_K,), lambda i: (i,), memory_space=pltpu.SMEM),
            pl.BlockSpec((tt, D_MODEL), lambda i: (i, 0)),
            pl.BlockSpec(memory_space=pl.ANY),
        ],
        out_specs=pl.BlockSpec(memory_space=pl.ANY),
        out_shape=jax.ShapeDtypeStruct(xs.shape, xs.dtype),
        scratch_shapes=[pltpu.SemaphoreType.DMA(())],
        input_output_aliases={2: 0},
        compiler_params=_cparams("arbitrary"),
        name="moe_scatter",
    )(dest_flat, hm, xs)


def _w1_split_kernel(w_ref, g_ref, l_ref, t_scr):
    n_slab = w_ref.shape[1] // LANES
    for c in range(n_slab):
        t_scr[c] = w_ref[0, c * LANES:(c + 1) * LANES, :].T
    for c in range(n_slab):
        g_ref[0, :, c * LANES:(c + 1) * LANES] = t_scr[c, pl.ds(0, D_EXPERT, stride=2), :].astype(BF16)
        l_ref[0, :, c * LANES:(c + 1) * LANES] = t_scr[c, pl.ds(1, D_EXPERT, stride=2), :].astype(BF16)


def _w1_split(w1):
    ks = 512
    n_slab = ks // LANES
    out = jax.ShapeDtypeStruct((N_EXPERTS, D_EXPERT, D_MODEL), BF16)
    return pl.pallas_call(
        _w1_split_kernel,
        grid=(N_EXPERTS, D_MODEL // ks),
        in_specs=[pl.BlockSpec((1, ks, 2 * D_EXPERT), lambda e, j: (e, j, 0))],
        out_specs=[pl.BlockSpec((1, D_EXPERT, ks), lambda e, j: (e, 0, j)),
                   pl.BlockSpec((1, D_EXPERT, ks), lambda e, j: (e, 0, j))],
        out_shape=[out, out],
        scratch_shapes=[pltpu.VMEM((n_slab, 2 * D_EXPERT, LANES), F32)],
        compiler_params=_cparams("parallel", "parallel"),
        name="w1_split",
    )(w1)


def _expert_kernel(be_ref, nb_ref, x_ref, w1g_ref, w1l_ref, b1g_ref, b1l_ref, w2_ref, b2_ref, o_ref):
    del be_ref
    used = pl.program_id(0) < nb_ref[0]

    @pl.when(jnp.logical_not(used))
    def _():
        o_ref[...] = jnp.zeros_like(o_ref)

    @pl.when(used)
    def _():
        x = x_ref[...].astype(BF16)
        nt = (((1,), (1,)), ((), ()))
        hg = lax.dot_general(x, w1g_ref[0], nt, preferred_element_type=F32) + b1g_ref[0]
        hl = lax.dot_general(x, w1l_ref[0], nt, preferred_element_type=F32) + b1l_ref[0]
        glu = jnp.minimum(hg, SWIGLU_LIMIT)
        lin = jnp.clip(hl, -SWIGLU_LIMIT, SWIGLU_LIMIT)
        act = glu * jax.nn.sigmoid(SWIGLU_ALPHA * glu) * (lin + 1.0)
        o_ref[...] = jnp.dot(act.astype(BF16), w2_ref[0].astype(BF16), preferred_element_type=F32) + b2_ref[0]


def _experts(blk_exp, n_used, xs, w1g_t, w1l_t, b1g, b1l, w2, b2):
    n_slot = xs.shape[0]
    n_blk = n_slot // MOE_ROWS
    row_in = lambda i, be, nb: (jnp.minimum(i, nb[0] - 1), 0)
    row = lambda i, be, nb: (i, 0)
    wsel = lambda i, be, nb: (be[i], 0, 0)
    grid_spec = pltpu.PrefetchScalarGridSpec(
        num_scalar_prefetch=2,
        grid=(n_blk,),
        in_specs=[
            pl.BlockSpec((MOE_ROWS, D_MODEL), row_in),
            pl.BlockSpec((1, D_EXPERT, D_MODEL), wsel),
            pl.BlockSpec((1, D_EXPERT, D_MODEL), wsel),
            pl.BlockSpec((1, 1, D_EXPERT), wsel),
            pl.BlockSpec((1, 1, D_EXPERT), wsel),
            pl.BlockSpec((1, D_EXPERT, D_MODEL), wsel),
            pl.BlockSpec((1, 1, D_MODEL), wsel),
        ],
        out_specs=pl.BlockSpec((MOE_ROWS, D_MODEL), row),
    )
    return pl.pallas_call(
        _expert_kernel,
        grid_spec=grid_spec,
        out_shape=jax.ShapeDtypeStruct((n_slot, D_MODEL), F32),
        compiler_params=_cparams("arbitrary"),
        name="experts",
    )(blk_exp, n_used, xs, w1g_t, w1l_t, b1g, b1l, w2, b2)


def _combine_kernel(dest_ref, dnext_ref, x1_ref, g4_ref, nf_ref, ys_ref, o_ref, buf, sems, *, n):
    tt = x1_ref.shape[0]
    i = pl.program_id(0)

    def issue(d_ref, slot):
        def body(t, carry):
            for k in range(TOP_K):
                _row_copy(ys_ref, d_ref[t * TOP_K + k], buf.at[slot, k], t, sems.at[slot]).start()
            return carry

        lax.fori_loop(0, tt, body, 0, unroll=8)

    def finish(slot):
        for k in range(TOP_K):
            pltpu.make_async_copy(ys_ref.at[pl.ds(0, tt)], buf.at[slot, k], sems.at[slot]).wait()
        g4 = g4_ref[...]
        moe = g4[:, 0:1] * buf[slot, 0]
        for k in range(1, TOP_K):
            moe = moe + g4[:, k:k + 1] * buf[slot, k]
        o_ref[...] = _rms(x1_ref[...] + moe, nf_ref[...])

    @pl.when(i == 0)
    def _():
        issue(dest_ref, 0)

    for parity in range(2):
        @pl.when(i % 2 == parity)
        def _():
            @pl.when(i + 1 < n)
            def _():
                issue(dnext_ref, 1 - parity)

            finish(parity)


def _combine(dest_flat, x1, g4, nf, ys):
    n = x1.shape[0]
    tt = _tile(n, 256)
    n_t = n // tt
    row = lambda i: (i, 0)
    return pl.pallas_call(
        functools.partial(_combine_kernel, n=n_t),
        grid=(n_t,),
        in_specs=[
            pl.BlockSpec((tt * TOP_K,), lambda i: (i,), memory_space=pltpu.SMEM),
            pl.BlockSpec((tt * TOP_K,), lambda i: (jnp.minimum(i + 1, n_t - 1),), memory_space=pltpu.SMEM),
            pl.BlockSpec((tt, D_MODEL), row),
            pl.BlockSpec((tt, TOP_K), row),
            pl.BlockSpec((1, D_MODEL), lambda i: (0, 0)),
            pl.BlockSpec(memory_space=pl.ANY),
        ],
        out_specs=pl.BlockSpec((tt, D_MODEL), row),
        out_shape=jax.ShapeDtypeStruct((n, D_MODEL), F32),
        scratch_shapes=[pltpu.VMEM((2, TOP_K, tt, D_MODEL), F32), pltpu.SemaphoreType.DMA((2,))],
        compiler_params=_cparams("arbitrary"),
        name="moe_combine",
    )(dest_flat, dest_flat, x1, g4, nf, ys)


def _rope_table(pos):
    half = MLA_ROPE // 2
    inv_freq = ROPE_THETA ** (-jnp.arange(half, dtype=F32) / half)
    ang = pos.astype(F32)[:, None] * inv_freq[None, :]
    cos, sin = jnp.cos(ang), jnp.sin(ang)
    return jnp.concatenate([cos, cos, -sin, sin], axis=-1)


def _regroup_w_in(w_in):
    o = np.cumsum((Q_LORA, KV_LORA, MLA_ROPE, GLA_DK, GLA_DK, GLA_DV, GLA_GATE_RANK, GLA_DV, 2 * D_MODEL))
    c_q, c_kv, k_r = w_in[:, :o[0]], w_in[:, o[0]:o[1]], w_in[:, o[1]:o[2]]
    gq, gk, gv = w_in[:, o[2]:o[3]], w_in[:, o[3]:o[4]], w_in[:, o[4]:o[5]]
    ga, gg, mg = w_in[:, o[5]:o[6]], w_in[:, o[6]:o[7]], w_in[:, o[7]:o[8]]
    half = MLA_ROPE // 2
    k_r_sw = jnp.concatenate([k_r[:, half:], k_r[:, :half]], axis=1)
    ga_pad = jnp.zeros((D_MODEL, LANES - GLA_GATE_RANK), w_in.dtype)
    return jnp.concatenate([c_q, c_kv, k_r, k_r_sw, ga, ga_pad, gv, gg, gq, gk, mg], axis=1).astype(BF16)


def _regroup_w_uq(w_uq):
    w = w_uq.reshape(Q_LORA, MLA_HEADS, MLA_NOPE + MLA_ROPE)
    half = MLA_ROPE // 2
    nope, rope = w[..., :MLA_NOPE], w[..., MLA_NOPE:]
    rope_sw = jnp.concatenate([rope[..., half:], rope[..., :half]], axis=-1)
    return jnp.concatenate([nope, rope, rope_sw], axis=-1).reshape(Q_LORA, MLA_HEADS * HEAD_PAD).astype(BF16)


def _regroup_w_ukv(w_ukv):
    w = w_ukv.reshape(KV_LORA, MLA_HEADS, MLA_NOPE + MLA_V)
    k = w[..., :MLA_NOPE].reshape(KV_LORA, MLA_HEADS * MLA_NOPE)
    v = w[..., MLA_NOPE:].reshape(KV_LORA, MLA_HEADS * MLA_V)
    return jnp.concatenate([k, v], axis=1).astype(BF16)


def _per_head_w_ukv(w_ukv):
    w = w_ukv.reshape(KV_LORA, MLA_HEADS, MLA_NOPE + MLA_V)
    w_uk = jnp.transpose(w[..., :MLA_NOPE], (1, 2, 0)).astype(BF16)
    w_uv = jnp.transpose(w[..., MLA_NOPE:], (1, 0, 2)).astype(BF16)
    return w_uk, w_uv


def _slot_plan(counts_first, counts_rest, n_blk):
    counts_first = counts_first.astype(jnp.int32)
    counts = counts_first + counts_rest.astype(jnp.int32)
    padded = (counts + MOE_ROWS - 1) // MOE_ROWS * MOE_ROWS
    pad_end = jnp.cumsum(padded)
    pad_start = pad_end - padded
    n_used = (pad_end[-1] // MOE_ROWS).reshape(1)
    blk_start = jnp.arange(n_blk, dtype=jnp.int32) * MOE_ROWS
    blk_exp = jnp.sum(blk_start[:, None] >= pad_end[None, :], axis=1).astype(jnp.int32)
    last_exp = blk_exp[jnp.maximum(n_used[0] - 1, 0)]
    blk_exp = jnp.where(jnp.arange(n_blk) < n_used[0], blk_exp, last_exp)
    ps_row = jnp.pad(pad_start.astype(F32), (0, LANES - N_EXPERTS))[None, :]
    zero_first = jnp.concatenate([(pad_start + counts_first) // MOE_ROWS, n_used]).astype(jnp.int32)
    zero_last = jnp.concatenate([pad_end // MOE_ROWS, jnp.full((1,), n_blk)]).astype(jnp.int32)
    return ps_row, zero_first, zero_last, blk_exp, n_used


def _mixers(x, past_lat, past_kr, s0, past_len, p):
    bsz, seq, _ = x.shape
    n = bsz * seq
    x2d = x.reshape(n, D_MODEL)
    hin = _in_proj(x2d, p["norm1_g"], p["w_all"])
    tm = min(512, n)
    cs = _rope_table(past_len + jnp.arange(seq))
    if seq < tm:
        cs = jnp.tile(cs, (tm // seq, 1))
    q, lat_new, krp_new = _mla_q(hin, cs, p["q_norm_g"], p["kv_norm_g"], p["w_uq"])
    if past_len:
        attn = _mla_cached(q, past_lat, past_kr, lat_new, krp_new, p["w_uk"], p["w_uv"], bsz, seq)
    else:
        kcat, v = _kv_up(lat_new, krp_new, p["w_ukv"])
        tq = min(1024, seq)
        tk = 1024 if seq % 1024 == 0 else seq
        attn = _flash(q.reshape(bsz, seq, -1), kcat.reshape(bsz, seq, -1), v.reshape(bsz, seq, -1), 0, tq, tk)
    og, s_new = _gla(hin, p["w_alpha"], p["b_alpha"], p["gla_norm_g"], s0, bsz, seq)
    x1, hm, sel, g4, cnt = _post_mix(attn.reshape(n, -1), og, hin, x2d, p["w_o_mla"], p["w_o_gla"], p["w_out"],
                                     p["norm2_g"], p["w_router"], p["b_router"])
    new_state = (lat_new.reshape(bsz, seq, KV_LORA), krp_new[:, :MLA_ROPE].reshape(bsz, seq, MLA_ROPE), s_new)
    return x1, hm, sel, g4, cnt, new_state


def kernel(x_prompt, x_sample, cache_mla_latent, cache_mla_krope, state_gla, norm1_g, w_in, q_norm_g, kv_norm_g,
           w_uq, w_ukv, w_o_mla, w_alpha2, b_alpha, gla_norm_g, w_o_gla, w_out, norm2_g, w_router, b_router,
           w1, b1, w2, b2, normf_g):
    depth = w_in.shape[0]
    assert depth == 1
    l = 0
    bp, sp, _ = x_prompt.shape
    bs, ss, _ = x_sample.shape
    past_len = cache_mla_latent.shape[2]
    p = {
        "norm1_g": norm1_g[l][None, :],
        "w_all": _regroup_w_in(w_in[l]),
        "q_norm_g": q_norm_g[l][None, :],
        "kv_norm_g": kv_norm_g[l][None, :],
        "w_uq": _regroup_w_uq(w_uq[l]),
        "w_ukv": _regroup_w_ukv(w_ukv[l]),
        "w_uk": _per_head_w_ukv(w_ukv[l])[0],
        "w_uv": _per_head_w_ukv(w_ukv[l])[1],
        "w_o_mla": w_o_mla[l].astype(BF16),
        "w_alpha": jnp.pad(w_alpha2[l], ((0, LANES - GLA_GATE_RANK), (0, 0))).astype(BF16),
        "b_alpha": b_alpha[l][None, :],
        "gla_norm_g": gla_norm_g[l][None, :],
        "w_o_gla": w_o_gla[l].astype(BF16),
        "w_out": w_out[l].astype(BF16),
        "norm2_g": norm2_g[l][None, :],
        "w_router": jnp.pad(w_router[l], ((0, 0), (0, LANES - N_EXPERTS))).astype(BF16),
        "b_router": jnp.pad(b_router[l], (0, LANES - N_EXPERTS))[None, :],
    }
    zeros_state = jnp.zeros((bp, GLA_HEADS, GLA_DK_HEAD, GLA_DV_HEAD), F32)
    x1p, hmp, selp, g4p, cntp, (lat_p, kr_p, st_p) = _mixers(x_prompt, None, None, zeros_state, 0, p)
    x1s, hms, sels, g4s, cnts, (lat_s, kr_s, st_s) = _mixers(x_sample, cache_mla_latent[l], cache_mla_krope[l],
                                                              state_gla[l], past_len, p)

    n_p, n_s = bp * sp, bs * ss
    n_asg = (n_p + n_s) * TOP_K
    n_slot = -(-n_asg // MOE_ROWS) * MOE_ROWS + N_EXPERTS * MOE_ROWS
    ps_row, zero_first, zero_last, blk_exp, n_used = _slot_plan(cntp[0, :N_EXPERTS], cnts[0, :N_EXPERTS],
                                                                n_slot // MOE_ROWS)
    dest = _dest(jnp.concatenate([selp, sels], axis=0), ps_row).reshape(n_asg)
    dest_p, dest_s = dest[:n_p * TOP_K], dest[n_p * TOP_K:]
    xs = _scatter_init(zero_first, zero_last, dest_p, hmp, n_slot)
    xs = _scatter_more(dest_s, hms, xs)
    w1g_t, w1l_t = _w1_split(w1[l])
    b1g = b1[l][:, None, 0::2]
    b1lin = b1[l][:, None, 1::2]
    ys_slots = _experts(blk_exp, n_used, xs, w1g_t, w1l_t, b1g, b1lin, w2[l], b2[l][:, None, :])
    yp = _combine(dest_p, x1p, g4p, normf_g[None, :], ys_slots).reshape(bp, sp, D_MODEL)
    ys = _combine(dest_s, x1s, g4s, normf_g[None, :], ys_slots).reshape(bs, ss, D_MODEL)
    return (yp, ys, lat_p[None], kr_p[None], st_p[None], lat_s[None], kr_s[None], st_s[None])
```

```python
import functools

import jax
import jax.numpy as jnp
import numpy as np
from jax import lax
from jax.experimental import pallas as pl
from jax.experimental.pallas import tpu as pltpu

F32 = jnp.float32
BF16 = jnp.bfloat16

D_MODEL = 1024
CHUNK = 64
CHUNK_SHIFT = 6
LOG2_E = 1.4426950408889634
NORM_EPS = 1e-6
MLA_HEADS = 8
MLA_NOPE = 128
MLA_ROPE = 64
MLA_V = 128
Q_LORA = 512
KV_LORA = 256
ROPE_THETA = 10000.0
MLA_SCALE = (MLA_NOPE + MLA_ROPE) ** -0.5
NEG_INF = -1e30
GLA_HEADS = 4
GLA_DK = D_MODEL // 2
GLA_DV = D_MODEL
GLA_DK_HEAD = GLA_DK // GLA_HEADS
GLA_DV_HEAD = GLA_DV // GLA_HEADS
GLA_GATE_RANK = 16
GLA_GATE_TEMP = 16.0
GLA_SUB = 16
FLASH_HEADS = 2
POST_MIX_PARTS = 2
N_EXPERTS = 32
TOP_K = 4
D_EXPERT = D_MODEL
SWIGLU_LIMIT = 7.0
SWIGLU_ALPHA = 1.702
MOE_ROWS = 512

LANES = 128
HEAD_PAD = 256

COL_CQ = 0
COL_CKV = 512
COL_KR = 768
COL_GA = 896
COL_GV = 1024
COL_GG = 2048
COL_GQ = 3072
COL_GK = 3584
COL_MG = 4096
D_IN_PAD = 6144

VMEM_LIMIT = 56 * 1024 * 1024


def _cparams(*sem, flags=None):
    return pltpu.CompilerParams(dimension_semantics=sem, vmem_limit_bytes=VMEM_LIMIT, flags=flags)


def _tile(n, pref, mult=16):
    t = min(pref, n)
    t -= t % mult
    while n % t:
        t -= mult
    return t


def _rms(x, g):
    return x * lax.rsqrt(jnp.mean(x * x, axis=-1, keepdims=True) + NORM_EPS) * g


def _in_proj_kernel(x_ref, g_ref, w_ref, o_ref):
    h = _rms(x_ref[...], g_ref[...]).astype(BF16)
    tn = 1536
    for j in range(D_IN_PAD // tn):
        o_ref[:, j * tn:(j + 1) * tn] = jnp.dot(h, w_ref[:, j * tn:(j + 1) * tn], preferred_element_type=F32)


def _in_proj(x2d, g, w_all):
    n = x2d.shape[0]
    tm = min(256, n)
    return pl.pallas_call(
        _in_proj_kernel,
        grid=(n // tm,),
        in_specs=[
            pl.BlockSpec((tm, D_MODEL), lambda i: (i, 0)),
            pl.BlockSpec((1, D_MODEL), lambda i: (0, 0)),
            pl.BlockSpec((D_MODEL, D_IN_PAD), lambda i: (0, 0)),
        ],
        out_specs=pl.BlockSpec((tm, D_IN_PAD), lambda i: (i, 0)),
        out_shape=jax.ShapeDtypeStruct((n, D_IN_PAD), F32),
        compiler_params=_cparams("parallel"),
        name="in_proj",
    )(x2d, g, w_all)


def _rope_pair(blk, cs):
    t = blk * cs
    return t + pltpu.roll(t, MLA_ROPE, axis=1)


def _mla_q_kernel(h_ref, cs_ref, qg_ref, kvg_ref, wuq_ref, q_ref, lat_ref, kr_ref):
    cs = cs_ref[...]
    qn = _rms(h_ref[:, COL_CQ:COL_CQ + Q_LORA], qg_ref[...]).astype(BF16)
    q = jnp.dot(qn, wuq_ref[...], preferred_element_type=F32)
    for hh in range(MLA_HEADS):
        c0 = hh * HEAD_PAD
        q_ref[:, c0:c0 + MLA_NOPE] = q[:, c0:c0 + MLA_NOPE].astype(BF16)
        q_ref[:, c0 + MLA_NOPE:c0 + HEAD_PAD] = _rope_pair(q[:, c0 + MLA_NOPE:c0 + HEAD_PAD], cs).astype(BF16)
    lat_ref[...] = _rms(h_ref[:, COL_CKV:COL_CKV + KV_LORA], kvg_ref[...])
    r = _rope_pair(h_ref[:, COL_KR:COL_KR + LANES], cs)
    lane = lax.broadcasted_iota(jnp.int32, r.shape, 1)
    kr_ref[...] = jnp.where(lane < MLA_ROPE, r, 0.0)


def _mla_q(hin, cs, qg, kvg, wuq):
    n = hin.shape[0]
    tm = min(512, n)
    n_cs = cs.shape[0] // tm
    return pl.pallas_call(
        _mla_q_kernel,
        grid=(n // tm,),
        in_specs=[
            pl.BlockSpec((tm, COL_GA), lambda i: (i, 0)),
            pl.BlockSpec((tm, LANES), lambda i: (i % n_cs, 0)),
            pl.BlockSpec((1, Q_LORA), lambda i: (0, 0)),
            pl.BlockSpec((1, KV_LORA), lambda i: (0, 0)),
            pl.BlockSpec((Q_LORA, MLA_HEADS * HEAD_PAD), lambda i: (0, 0)),
        ],
        out_specs=[
            pl.BlockSpec((tm, MLA_HEADS * HEAD_PAD), lambda i: (i, 0)),
            pl.BlockSpec((tm, KV_LORA), lambda i: (i, 0)),
            pl.BlockSpec((tm, LANES), lambda i: (i, 0)),
        ],
        out_shape=[
            jax.ShapeDtypeStruct((n, MLA_HEADS * HEAD_PAD), BF16),
            jax.ShapeDtypeStruct((n, KV_LORA), F32),
            jax.ShapeDtypeStruct((n, LANES), F32),
        ],
        compiler_params=_cparams("parallel"),
        name="mla_q",
    )(hin, cs, qg, kvg, wuq)


def _kv_up_kernel(lat_ref, kr_ref, w_ref, k_ref, v_ref):
    kv = jnp.dot(lat_ref[...].astype(BF16), w_ref[...], preferred_element_type=F32)
    krb = kr_ref[...].astype(BF16)
    for hh in range(MLA_HEADS):
        c0 = hh * HEAD_PAD
        k_ref[:, c0:c0 + MLA_NOPE] = kv[:, hh * MLA_NOPE:(hh + 1) * MLA_NOPE].astype(BF16)
        k_ref[:, c0 + MLA_NOPE:c0 + HEAD_PAD] = krb
    v_ref[...] = kv[:, MLA_HEADS * MLA_NOPE:].astype(BF16)


def _kv_up(lat, krp, wukv):
    n = lat.shape[0]
    tm = _tile(n, 512)
    return pl.pallas_call(
        _kv_up_kernel,
        grid=(n // tm,),
        in_specs=[
            pl.BlockSpec((tm, KV_LORA), lambda i: (i, 0)),
            pl.BlockSpec((tm, LANES), lambda i: (i, 0)),
            pl.BlockSpec((KV_LORA, MLA_HEADS * (MLA_NOPE + MLA_V)), lambda i: (0, 0)),
        ],
        out_specs=[
            pl.BlockSpec((tm, MLA_HEADS * HEAD_PAD), lambda i: (i, 0)),
            pl.BlockSpec((tm, MLA_HEADS * MLA_V), lambda i: (i, 0)),
        ],
        out_shape=[
            jax.ShapeDtypeStruct((n, MLA_HEADS * HEAD_PAD), BF16),
            jax.ShapeDtypeStruct((n, MLA_HEADS * MLA_V), BF16),
        ],
        compiler_params=_cparams("parallel"),
        name="kv_up",
    )(lat, krp, wukv)


def _flash_kernel(q_ref, k_ref, v_ref, o_ref, *, tq, tk, past_len):
    qi = pl.program_id(2)
    q_pos0 = past_len + qi * tq
    q_chunk = (q_pos0 + lax.broadcasted_iota(jnp.int32, (tq, 1), 0)) >> CHUNK_SHIFT
    n_blocks = (q_pos0 + tq + tk - 1) // tk
    n_full = ((q_pos0 >> CHUNK_SHIFT) << CHUNK_SHIFT) // tk
    qs = [q_ref[0, :, hh * HEAD_PAD:(hh + 1) * HEAD_PAD] for hh in range(FLASH_HEADS)]

    def step(j, carry, masked):
        k0 = pl.multiple_of(j * tk, tk)
        if masked:
            k_chunk = (k0 + lax.broadcasted_iota(jnp.int32, (1, tk), 1)) >> CHUNK_SHIFT
        out = []
        for hh in range(FLASH_HEADS):
            m, l, acc = carry[hh]
            k = k_ref[0, pl.ds(k0, tk), hh * HEAD_PAD:(hh + 1) * HEAD_PAD]
            v = v_ref[0, pl.ds(k0, tk), hh * MLA_V:(hh + 1) * MLA_V]
            s = lax.dot_general(qs[hh], k, (((1,), (1,)), ((), ())), preferred_element_type=F32)
            s = s * (MLA_SCALE * LOG2_E)
            if masked:
                s = jnp.where(k_chunk <= q_chunk, s, NEG_INF)
            m_new = jnp.maximum(m, jnp.max(s, axis=-1, keepdims=True))
            alpha = jnp.exp2(m - m_new)
            p = jnp.exp2(s - m_new)
            l = alpha * l + jnp.sum(p, axis=-1, keepdims=True)
            acc = alpha * acc + jnp.dot(p.astype(BF16), v, preferred_element_type=F32)
            out.append((m_new, l, acc))
        return tuple(out)

    init = tuple((jnp.full((tq, 1), NEG_INF, F32), jnp.zeros((tq, 1), F32), jnp.zeros((tq, MLA_V), F32))
                 for _ in range(FLASH_HEADS))
    carry = lax.fori_loop(0, n_full, functools.partial(step, masked=False), init)
    carry = lax.fori_loop(n_full, n_blocks, functools.partial(step, masked=True), carry)
    for hh in range(FLASH_HEADS):
        _, l, acc = carry[hh]
        o_ref[0, :, hh * MLA_V:(hh + 1) * MLA_V] = (acc / l).astype(BF16)


def _flash(q, k, v, past_len, tq, tk):
    b, sq, _ = q.shape
    sk = k.shape[1]
    fh = FLASH_HEADS
    return pl.pallas_call(
        functools.partial(_flash_kernel, tq=tq, tk=tk, past_len=past_len),
        grid=(b, MLA_HEADS // fh, sq // tq),
        in_specs=[
            pl.BlockSpec((1, tq, fh * HEAD_PAD), lambda bi, h, i: (bi, i, h)),
            pl.BlockSpec((1, sk, fh * HEAD_PAD), lambda bi, h, i: (bi, 0, h)),
            pl.BlockSpec((1, sk, fh * MLA_V), lambda bi, h, i: (bi, 0, h)),
        ],
        out_specs=pl.BlockSpec((1, tq, fh * MLA_V), lambda bi, h, i: (bi, i, h)),
        out_shape=jax.ShapeDtypeStruct((b, sq, MLA_HEADS * MLA_V), BF16),
        compiler_params=_cparams("parallel", "parallel", "arbitrary"),
        name="flash",
    )(q, k, v)


def _mla_cached_kernel(q_ref, clat_ref, ckr_ref, nlat_ref, nkr_ref, wuk_ref, wuv_ref, o_ref, *, past_len):
    seq = q_ref.shape[0]
    nt = (((1,), (1,)), ((), ()))
    q_abs = jnp.concatenate(
        [jnp.dot(q_ref[:, hh * HEAD_PAD:hh * HEAD_PAD + MLA_NOPE], wuk_ref[hh], preferred_element_type=F32)
         for hh in range(MLA_HEADS)], axis=0).astype(BF16)
    q_rot = jnp.concatenate([q_ref[:, hh * HEAD_PAD + MLA_NOPE:(hh + 1) * HEAD_PAD] for hh in range(MLA_HEADS)],
                            axis=0)
    pos_q = past_len + lax.broadcasted_iota(jnp.int32, (seq, 1), 0)
    q_chunk = jnp.concatenate([pos_q] * MLA_HEADS, axis=0) >> CHUNK_SHIFT

    def scores(lat, kr, k_pos0):
        n_k = lat.shape[0]
        kr_pad = jnp.concatenate([kr, jnp.zeros_like(kr)], axis=1)
        s = (lax.dot_general(q_abs, lat, nt, preferred_element_type=F32)
             + lax.dot_general(q_rot, kr_pad, nt, preferred_element_type=F32)) * (MLA_SCALE * LOG2_E)
        k_chunk = (k_pos0 + lax.broadcasted_iota(jnp.int32, (1, n_k), 1)) >> CHUNK_SHIFT
        return jnp.where(k_chunk <= q_chunk, s, NEG_INF)

    lat_c = clat_ref[0].astype(BF16)
    lat_n = nlat_ref[...].astype(BF16)
    s_c = scores(lat_c, ckr_ref[0].astype(BF16), 0)
    s_n = scores(lat_n, nkr_ref[:, :MLA_ROPE].astype(BF16), past_len)
    m = jnp.maximum(jnp.max(s_c, axis=-1, keepdims=True), jnp.max(s_n, axis=-1, keepdims=True))
    p_c = jnp.exp2(s_c - m)
    p_n = jnp.exp2(s_n - m)
    l = jnp.sum(p_c, axis=-1, keepdims=True) + jnp.sum(p_n, axis=-1, keepdims=True)
    o_lat = (jnp.dot(p_c.astype(BF16), lat_c, preferred_element_type=F32)
             + jnp.dot(p_n.astype(BF16), lat_n, preferred_element_type=F32)) / l
    o_lat = o_lat.astype(BF16)
    for hh in range(MLA_HEADS):
        o_ref[:, hh * MLA_V:(hh + 1) * MLA_V] = jnp.dot(o_lat[hh * seq:(hh + 1) * seq], wuv_ref[hh],
                                                         preferred_element_type=F32).astype(BF16)


def _mla_cached(q, cache_lat, cache_kr, lat_new, krp_new, w_uk, w_uv, bsz, seq):
    past_len = cache_lat.shape[1]
    return pl.pallas_call(
        functools.partial(_mla_cached_kernel, past_len=past_len),
        grid=(bsz,),
        in_specs=[
            pl.BlockSpec((seq, MLA_HEADS * HEAD_PAD), lambda b: (b, 0)),
            pl.BlockSpec((1, past_len, KV_LORA), lambda b: (b, 0, 0)),
            pl.BlockSpec((1, past_len, MLA_ROPE), lambda b: (b, 0, 0)),
            pl.BlockSpec((seq, KV_LORA), lambda b: (b, 0)),
            pl.BlockSpec((seq, LANES), lambda b: (b, 0)),
            pl.BlockSpec((MLA_HEADS, MLA_NOPE, KV_LORA), lambda b: (0, 0, 0)),
            pl.BlockSpec((MLA_HEADS, KV_LORA, MLA_V), lambda b: (0, 0, 0)),
        ],
        out_specs=pl.BlockSpec((seq, MLA_HEADS * MLA_V), lambda b: (b, 0)),
        out_shape=jax.ShapeDtypeStruct((bsz * seq, MLA_HEADS * MLA_V), BF16),
        compiler_params=_cparams("parallel"),
        name="mla_cached",
    )(q, cache_lat, cache_kr, lat_new, krp_new, w_uk, w_uv)


def _cumsum_rows(x):
    n = x.shape[0]
    row = lax.broadcasted_iota(jnp.int32, x.shape, 0)
    s = 1
    while s < n:
        x = x + jnp.where(row >= s, pltpu.roll(x, s, axis=0), 0.0)
        s *= 2
    return x


def _gla_kernel(gv_ref, gg_ref, gq_ref, gk_ref, ga_ref, wa_ref, ba_ref, ng_ref, s0_ref, og_ref, sn_ref, st_scr,
                *, csize, n_chunks):
    t = pl.program_id(1)

    @pl.when(t == 0)
    def _():
        for hh in range(GLA_HEADS):
            st_scr[hh] = s0_ref[0, hh].T

    wa = wa_ref[...]
    ba = ba_ref[...]
    ng = ng_ref[...]
    n_sub = csize // GLA_SUB
    col_id = lax.broadcasted_iota(jnp.int32, (GLA_SUB, csize), 1)
    row_in_sub = lax.broadcasted_iota(jnp.int32, (GLA_SUB, csize), 0)

    def chunk(c, carry):
        r0 = pl.multiple_of(c * csize, csize)
        rows = pl.ds(r0, csize)
        x = jnp.dot(ga_ref[rows, :].astype(BF16), wa, preferred_element_type=F32) + ba
        la = jax.nn.log_sigmoid(x) / GLA_GATE_TEMP
        b_all = _cumsum_rows(la) * LOG2_E
        for hh in range(GLA_HEADS):
            head(hh, rows, b_all[:, hh * GLA_DK_HEAD:(hh + 1) * GLA_DK_HEAD])
        return carry

    def head(hh, rows, b):
        kcols = slice(hh * GLA_DK_HEAD, (hh + 1) * GLA_DK_HEAD)
        vcols = slice(hh * GLA_DV_HEAD, (hh + 1) * GLA_DV_HEAD)
        q = gq_ref[rows, kcols] * (GLA_DK_HEAD ** -0.5)
        k = gk_ref[rows, kcols]
        v = gv_ref[rows, vcols]

        a_rows = []
        for i in range(n_sub):
            lo = i * GLA_SUB
            bi = b[lo:lo + GLA_SUB]
            qi = q[lo:lo + GLA_SUB]
            ki = k[lo:lo + GLA_SUB]
            if i == 0:
                a_i = jnp.zeros((GLA_SUB, csize), F32)
            else:
                ri = b[lo - 1:lo]
                qs = qi * jnp.exp2(bi - ri)
                ks = k * jnp.exp2(jnp.minimum(ri - b, 0.0))
                a_i = lax.dot_general(qs.astype(BF16), ks.astype(BF16), (((1,), (1,)), ((), ())),
                                      preferred_element_type=F32)
            for s in range(GLA_SUB):
                col = jnp.sum(qi * ki[s:s + 1] * jnp.exp2(bi - bi[s:s + 1]), axis=-1, keepdims=True)
                a_i = jnp.where(col_id == lo + s, col, a_i)
            a_rows.append(jnp.where(col_id <= lo + row_in_sub, a_i, 0.0))
        a = jnp.concatenate(a_rows, axis=0)

        st = st_scr[hh]
        vb = v.astype(BF16)
        o = jnp.dot(a.astype(BF16), vb, preferred_element_type=F32)
        o = o + lax.dot_general((q * jnp.exp2(b)).astype(BF16), st.astype(BF16), (((1,), (1,)), ((), ())),
                                preferred_element_type=F32)
        b_end = b[csize - 1:csize]
        kd = (k * jnp.exp2(b_end - b)).astype(BF16)
        st_scr[hh] = jnp.exp2(b_end) * st + jnp.dot(v.T.astype(BF16), kd, preferred_element_type=F32)
        on = _rms(o, ng)
        og_ref[rows, vcols] = (on * jax.nn.silu(gg_ref[rows, vcols])).astype(BF16)

    lax.fori_loop(0, n_chunks, chunk, 0)

    @pl.when(t == pl.num_programs(1) - 1)
    def _():
        for hh in range(GLA_HEADS):
            sn_ref[0, hh] = st_scr[hh].T


def _gla(hin, wa, ba, ng, s0, bsz, seq):
    csize = min(CHUNK, seq)
    tt = min(512, seq)
    n_t = seq // tt
    n = bsz * seq
    kern = functools.partial(_gla_kernel, csize=csize, n_chunks=tt // csize)
    state = pl.BlockSpec((1, GLA_HEADS, GLA_DK_HEAD, GLA_DV_HEAD), lambda b, t: (b, 0, 0, 0))
    return pl.pallas_call(
        kern,
        grid=(bsz, n_t),
        in_specs=[
            pl.BlockSpec((tt, GLA_DV), lambda b, t: (b * n_t + t, COL_GV // GLA_DV)),
            pl.BlockSpec((tt, GLA_DV), lambda b, t: (b * n_t + t, COL_GG // GLA_DV)),
            pl.BlockSpec((tt, GLA_DK), lambda b, t: (b * n_t + t, COL_GQ // GLA_DK)),
            pl.BlockSpec((tt, GLA_DK), lambda b, t: (b * n_t + t, COL_GK // GLA_DK)),
            pl.BlockSpec((tt, LANES), lambda b, t: (b * n_t + t, COL_GA // LANES)),
            pl.BlockSpec((LANES, GLA_DK), lambda b, t: (0, 0)),
            pl.BlockSpec((1, GLA_DK), lambda b, t: (0, 0)),
            pl.BlockSpec((1, GLA_DV_HEAD), lambda b, t: (0, 0)),
            state,
        ],
        out_specs=[pl.BlockSpec((tt, GLA_DV), lambda b, t: (b * n_t + t, 0)), state],
        out_shape=[
            jax.ShapeDtypeStruct((n, GLA_DV), BF16),
            jax.ShapeDtypeStruct((bsz, GLA_HEADS, GLA_DK_HEAD, GLA_DV_HEAD), F32),
        ],
        scratch_shapes=[pltpu.VMEM((GLA_HEADS, GLA_DV_HEAD, GLA_DK_HEAD), F32)],
        compiler_params=_cparams("parallel", "arbitrary"),
        name="gla",
    )(hin, hin, hin, hin, hin, wa, ba, ng, s0)


def _post_mix_kernel(at_ref, og_ref, mg1_ref, mg2_ref, x_ref, wom_ref, wog_ref, wout_ref, n2_ref, wr_ref, br_ref,
                     x1_ref, hm_ref, sel_ref, g4_ref, cnt_ref):
    @pl.when(pl.program_id(0) == 0)
    def _():
        cnt_ref[...] = jnp.zeros_like(cnt_ref)

    tm = x_ref.shape[0]
    half = tm // POST_MIX_PARTS
    for part in range(POST_MIX_PARTS):
        rows = slice(part * half, (part + 1) * half)
        y_mla = jnp.dot(at_ref[rows, :], wom_ref[...], preferred_element_type=F32)
        y_gla = jnp.dot(og_ref[rows, :], wog_ref[...], preferred_element_type=F32)
        m = jax.nn.sigmoid(mg1_ref[rows, :]) * y_mla + jax.nn.sigmoid(mg2_ref[rows, :]) * y_gla
        x1 = x_ref[rows, :] + jnp.dot(m.astype(BF16), wout_ref[...], preferred_element_type=F32)
        x1_ref[rows, :] = x1
        hm = _rms(x1, n2_ref[...])
        hm_ref[rows, :] = hm
        logits = jnp.dot(hm.astype(BF16), wr_ref[...], preferred_element_type=F32) + br_ref[...]

        lane = lax.broadcasted_iota(jnp.int32, logits.shape, 1)
        work = jnp.where(lane < N_EXPERTS, logits, -jnp.inf)
        sel = jnp.zeros(logits.shape, F32)
        vals = []
        for k in range(TOP_K):
            mk = jnp.max(work, axis=-1, keepdims=True)
            ik = jnp.min(jnp.where(work == mk, lane, LANES), axis=-1, keepdims=True)
            hit = lane == ik
            sel = jnp.where(hit, float(k + 1), sel)
            work = jnp.where(hit, -jnp.inf, work)
            vals.append(mk)
        sel_ref[rows, :] = sel
        ex = [jnp.exp(v - vals[0]) for v in vals]
        den = ex[0]
        for e in ex[1:]:
            den = den + e
        g4 = jnp.zeros(logits.shape, F32)
        for k in range(TOP_K):
            g4 = jnp.where(lane == k, ex[k] / den, g4)
        g4_ref[rows, :] = g4[:, :TOP_K]
        cnt_ref[...] += jnp.sum(jnp.where(sel > 0.0, 1.0, 0.0), axis=0, keepdims=True)


def _post_mix(attn, og, hin, x2d, wom, wog, wout, n2, wr, br):
    n = x2d.shape[0]
    tm = min(256 * POST_MIX_PARTS, n)
    row = lambda i: (i, 0)
    const = lambda i: (0, 0)
    return pl.pallas_call(
        _post_mix_kernel,
        grid=(n // tm,),
        in_specs=[
            pl.BlockSpec((tm, D_MODEL), row),
            pl.BlockSpec((tm, D_MODEL), row),
            pl.BlockSpec((tm, D_MODEL), lambda i: (i, COL_MG // D_MODEL)),
            pl.BlockSpec((tm, D_MODEL), lambda i: (i, COL_MG // D_MODEL + 1)),
            pl.BlockSpec((tm, D_MODEL), row),
            pl.BlockSpec((D_MODEL, D_MODEL), const),
            pl.BlockSpec((D_MODEL, D_MODEL), const),
            pl.BlockSpec((D_MODEL, D_MODEL), const),
            pl.BlockSpec((1, D_MODEL), const),
            pl.BlockSpec((D_MODEL, LANES), const),
            pl.BlockSpec((1, LANES), const),
        ],
        out_specs=[
            pl.BlockSpec((tm, D_MODEL), row),
            pl.BlockSpec((tm, D_MODEL), row),
            pl.BlockSpec((tm, LANES), row),
            pl.BlockSpec((tm, TOP_K), row),
            pl.BlockSpec((1, LANES), const),
        ],
        out_shape=[
            jax.ShapeDtypeStruct((n, D_MODEL), F32),
            jax.ShapeDtypeStruct((n, D_MODEL), F32),
            jax.ShapeDtypeStruct((n, LANES), F32),
            jax.ShapeDtypeStruct((n, TOP_K), F32),
            jax.ShapeDtypeStruct((1, LANES), F32),
        ],
        compiler_params=_cparams("arbitrary"),
        name="post_mix",
    )(attn, og, hin, hin, x2d, wom, wog, wout, n2, wr, br)


def _dest_kernel(sel_ref, ps_ref, dest_ref, run_scr):
    @pl.when(pl.program_id(0) == 0)
    def _():
        run_scr[...] = ps_ref[...]

    sel = sel_ref[...]
    tt = sel.shape[0]
    oh = jnp.where(sel > 0.0, 1.0, 0.0)
    r = lax.broadcasted_iota(jnp.int32, (tt, tt), 0)
    c = lax.broadcasted_iota(jnp.int32, (tt, tt), 1)
    ltri = jnp.where(r > c, 1.0, 0.0).astype(BF16)
    slot = jnp.dot(ltri, oh.astype(BF16), preferred_element_type=F32) + run_scr[...]
    run_scr[...] += jnp.sum(oh, axis=0, keepdims=True)
    lane = lax.broadcasted_iota(jnp.int32, sel.shape, 1)
    d = jnp.zeros(sel.shape, F32)
    for k in range(TOP_K):
        col = jnp.sum(jnp.where(sel == float(k + 1), slot, 0.0), axis=-1, keepdims=True)
        d = jnp.where(lane == k, col, d)
    dest_ref[...] = d[:, :TOP_K].astype(jnp.int32)


def _dest(sel, pad_start):
    n = sel.shape[0]
    tt = _tile(n, 512)
    return pl.pallas_call(
        _dest_kernel,
        grid=(n // tt,),
        in_specs=[pl.BlockSpec((tt, LANES), lambda i: (i, 0)), pl.BlockSpec((1, LANES), lambda i: (0, 0))],
        out_specs=pl.BlockSpec((tt, TOP_K), lambda i: (i, 0)),
        out_shape=jax.ShapeDtypeStruct((n, TOP_K), jnp.int32),
        scratch_shapes=[pltpu.VMEM((1, LANES), F32)],
        compiler_params=_cparams("arbitrary"),
        name="route_dest",
    )(sel, pad_start)


def _row_copy(src, src_row, dst, dst_row, sem):
    return pltpu.make_async_copy(src.at[pl.ds(src_row, 1)], dst.at[pl.ds(dst_row, 1)], sem)


def _scatter_rows(dest_ref, hm_ref, xs_ref, sem):
    tt = hm_ref.shape[0]

    def issue(t, carry):
        for k in range(TOP_K):
            _row_copy(hm_ref, t, xs_ref, dest_ref[t * TOP_K + k], sem).start()
        return carry

    lax.fori_loop(0, tt, issue, 0)
    for k in range(TOP_K):
        pltpu.make_async_copy(hm_ref, xs_ref.at[pl.ds(0, tt)], sem).wait()


def _scatter_init_kernel(zf_ref, zl_ref, dest_ref, hm_hbm, xs_ref, hbuf, zero_scr, lsems, ssems, zsem, *, n, tt):
    i = pl.program_id(0)

    def load(b, slot):
        return pltpu.make_async_copy(hm_hbm.at[pl.ds(pl.multiple_of(b * tt, tt), tt)], hbuf.at[slot], lsems.at[slot])

    def wait_rows(slot):
        for _ in range(TOP_K):
            pltpu.make_async_copy(hbuf.at[slot], xs_ref.at[pl.ds(0, tt)], ssems.at[slot]).wait()

    @pl.when(i == 0)
    def _():
        load(0, 0).start()
        zero_scr[...] = jnp.zeros_like(zero_scr)

        def zero_block(blk):
            return pltpu.make_async_copy(zero_scr,
                                         xs_ref.at[pl.ds(pl.multiple_of(blk * MOE_ROWS, MOE_ROWS), MOE_ROWS)], zsem)

        def start_range(e, carry):
            return lax.fori_loop(zf_ref[e], zl_ref[e], lambda blk, c: (zero_block(blk).start(), c)[1], carry)

        def wait_range(e, carry):
            return lax.fori_loop(zf_ref[e], zl_ref[e], lambda blk, c: (zero_block(blk).wait(), c)[1], carry)

        lax.fori_loop(0, N_EXPERTS + 1, start_range, 0)
        lax.fori_loop(0, N_EXPERTS + 1, wait_range, 0)

    for slot in range(3):
        @pl.when(i % 3 == slot)
        def _():
            @pl.when(i + 1 < n)
            def _():
                load(i + 1, (slot + 1) % 3).start()

            load(i, slot).wait()

            def issue(t, carry):
                for k in range(TOP_K):
                    _row_copy(hbuf.at[slot], t, xs_ref, dest_ref[t * TOP_K + k], ssems.at[slot]).start()
                return carry

            lax.fori_loop(0, tt, issue, 0)

            @pl.when(i >= 1)
            def _():
                wait_rows((slot + 2) % 3)

            @pl.when(i == n - 1)
            def _():
                wait_rows(slot)


def _scatter_more_kernel(dest_ref, hm_ref, xs_in, xs_ref, sem):
    del xs_in
    _scatter_rows(dest_ref, hm_ref, xs_ref, sem)


def _scatter_init(zero_first, zero_last, dest_flat, hm, n_slot):
    n = hm.shape[0]
    tt = _tile(n, 512)
    n_t = n // tt
    grid_spec = pltpu.PrefetchScalarGridSpec(
        num_scalar_prefetch=2,
        grid=(n_t,),
        in_specs=[
            pl.BlockSpec((tt * TOP_K,), lambda i, zf, zl: (i,), memory_space=pltpu.SMEM),
            pl.BlockSpec(memory_space=pl.ANY),
        ],
        out_specs=pl.BlockSpec(memory_space=pl.ANY),
        scratch_shapes=[pltpu.VMEM((3, tt, D_MODEL), F32), pltpu.VMEM((MOE_ROWS, D_MODEL), F32),
                        pltpu.SemaphoreType.DMA((3,)), pltpu.SemaphoreType.DMA((3,)), pltpu.SemaphoreType.DMA(())],
    )
    return pl.pallas_call(
        functools.partial(_scatter_init_kernel, n=n_t, tt=tt),
        grid_spec=grid_spec,
        out_shape=jax.ShapeDtypeStruct((n_slot, D_MODEL), F32),
        compiler_params=_cparams("arbitrary"),
        name="moe_scatter_init",
    )(zero_first, zero_last, dest_flat, hm)


def _scatter_more(dest_flat, hm, xs):
    n = hm.shape[0]
    tt = _tile(n, 256)
    return pl.pallas_call(
        _scatter_more_kernel,
        grid=(n // tt,),
        in_specs=[
            pl.BlockSpec((tt * TOP_K,), lambda i: (i,), memory_space=pltpu.SMEM),
            pl.BlockSpec((tt, D_MODEL), lambda i: (i, 0)),
            pl.BlockSpec(memory_space=pl.ANY),
        ],
        out_specs=pl.BlockSpec(memory_space=pl.ANY),
        out_shape=jax.ShapeDtypeStruct(xs.shape, xs.dtype),
        scratch_shapes=[pltpu.SemaphoreType.DMA(())],
        input_output_aliases={2: 0},
        compiler_params=_cparams("arbitrary"),
        name="moe_scatter",
    )(dest_flat, hm, xs)


def _w1_split_kernel(w_ref, g_ref, l_ref, t_scr):
    n_slab = w_ref.shape[1] // LANES
    for c in range(n_slab):
        t_scr[c] = w_ref[0, c * LANES:(c + 1) * LANES, :].T
    for c in range(n_slab):
        g_ref[0, :, c * LANES:(c + 1) * LANES] = t_scr[c, pl.ds(0, D_EXPERT, stride=2), :].astype(BF16)
        l_ref[0, :, c * LANES:(c + 1) * LANES] = t_scr[c, pl.ds(1, D_EXPERT, stride=2), :].astype(BF16)


def _w1_split(w1):
    ks = 512
    n_slab = ks // LANES
    out = jax.ShapeDtypeStruct((N_EXPERTS, D_EXPERT, D_MODEL), BF16)
    return pl.pallas_call(
        _w1_split_kernel,
        grid=(N_EXPERTS, D_MODEL // ks),
        in_specs=[pl.BlockSpec((1, ks, 2 * D_EXPERT), lambda e, j: (e, j, 0))],
        out_specs=[pl.BlockSpec((1, D_EXPERT, ks), lambda e, j: (e, 0, j)),
                   pl.BlockSpec((1, D_EXPERT, ks), lambda e, j: (e, 0, j))],
        out_shape=[out, out],
        scratch_shapes=[pltpu.VMEM((n_slab, 2 * D_EXPERT, LANES), F32)],
        compiler_params=_cparams("parallel", "parallel"),
        name="w1_split",
    )(w1)


def _expert_kernel(be_ref, nb_ref, x_ref, w1g_ref, w1l_ref, b1g_ref, b1l_ref, w2_ref, b2_ref, o_ref, w2b_scr):
    i = pl.program_id(0)
    used = i < nb_ref[0]

    @pl.when(jnp.logical_not(used))
    def _():
        o_ref[...] = jnp.zeros_like(o_ref)

    @pl.when(jnp.logical_or(i == 0, be_ref[i] != be_ref[jnp.maximum(i - 1, 0)]))
    def _():
        w2b_scr[...] = w2_ref[0].astype(BF16)

    @pl.when(used)
    def _():
        x = x_ref[...].astype(BF16)
        nt = (((1,), (1,)), ((), ()))
        hg = lax.dot_general(x, w1g_ref[0], nt, preferred_element_type=F32) + b1g_ref[0]
        hl = lax.dot_general(x, w1l_ref[0], nt, preferred_element_type=F32) + b1l_ref[0]
        glu = jnp.minimum(hg, SWIGLU_LIMIT)
        lin = jnp.clip(hl, -SWIGLU_LIMIT, SWIGLU_LIMIT)
        act = glu * jax.nn.sigmoid(SWIGLU_ALPHA * glu) * (lin + 1.0)
        o_ref[...] = jnp.dot(act.astype(BF16), w2b_scr[...], preferred_element_type=F32) + b2_ref[0]


def _experts(blk_exp, n_used, xs, w1g_t, w1l_t, b1g, b1l, w2, b2):
    n_slot = xs.shape[0]
    n_blk = n_slot // MOE_ROWS
    row_in = lambda i, be, nb: (jnp.minimum(i, nb[0] - 1), 0)
    row = lambda i, be, nb: (i, 0)
    wsel = lambda i, be, nb: (be[i], 0, 0)
    grid_spec = pltpu.PrefetchScalarGridSpec(
        num_scalar_prefetch=2,
        grid=(n_blk,),
        in_specs=[
            pl.BlockSpec((MOE_ROWS, D_MODEL), row_in),
            pl.BlockSpec((1, D_EXPERT, D_MODEL), wsel),
            pl.BlockSpec((1, D_EXPERT, D_MODEL), wsel),
            pl.BlockSpec((1, 1, D_EXPERT), wsel),
            pl.BlockSpec((1, 1, D_EXPERT), wsel),
            pl.BlockSpec((1, D_EXPERT, D_MODEL), wsel),
            pl.BlockSpec((1, 1, D_MODEL), wsel),
        ],
        out_specs=pl.BlockSpec((MOE_ROWS, D_MODEL), row),
        scratch_shapes=[pltpu.VMEM((D_EXPERT, D_MODEL), BF16)],
    )
    return pl.pallas_call(
        _expert_kernel,
        grid_spec=grid_spec,
        out_shape=jax.ShapeDtypeStruct((n_slot, D_MODEL), F32),
        compiler_params=_cparams("arbitrary"),
        name="experts",
    )(blk_exp, n_used, xs, w1g_t, w1l_t, b1g, b1l, w2, b2)


def _combine_kernel(dest_ref, dnext_ref, x1_ref, g4_ref, nf_ref, ys_ref, o_ref, buf, sems, *, n):
    tt = x1_ref.shape[0]
    i = pl.program_id(0)

    def issue(d_ref, slot):
        def body(t, carry):
            for k in range(TOP_K):
                _row_copy(ys_ref, d_ref[t * TOP_K + k], buf.at[slot, k], t, sems.at[slot]).start()
            return carry

        lax.fori_loop(0, tt, body, 0, unroll=8)

    def finish(slot):
        for k in range(TOP_K):
            pltpu.make_async_copy(ys_ref.at[pl.ds(0, tt)], buf.at[slot, k], sems.at[slot]).wait()
        g4 = g4_ref[...]
        moe = g4[:, 0:1] * buf[slot, 0]
        for k in range(1, TOP_K):
            moe = moe + g4[:, k:k + 1] * buf[slot, k]
        o_ref[...] = _rms(x1_ref[...] + moe, nf_ref[...])

    @pl.when(i == 0)
    def _():
        issue(dest_ref, 0)

    for parity in range(2):
        @pl.when(i % 2 == parity)
        def _():
            @pl.when(i + 1 < n)
            def _():
                issue(dnext_ref, 1 - parity)

            finish(parity)


def _combine(dest_flat, x1, g4, nf, ys):
    n = x1.shape[0]
    tt = _tile(n, 256)
    n_t = n // tt
    row = lambda i: (i, 0)
    return pl.pallas_call(
        functools.partial(_combine_kernel, n=n_t),
        grid=(n_t,),
        in_specs=[
            pl.BlockSpec((tt * TOP_K,), lambda i: (i,), memory_space=pltpu.SMEM),
            pl.BlockSpec((tt * TOP_K,), lambda i: (jnp.minimum(i + 1, n_t - 1),), memory_space=pltpu.SMEM),
            pl.BlockSpec((tt, D_MODEL), row),
            pl.BlockSpec((tt, TOP_K), row),
            pl.BlockSpec((1, D_MODEL), lambda i: (0, 0)),
            pl.BlockSpec(memory_space=pl.ANY),
        ],
        out_specs=pl.BlockSpec((tt, D_MODEL), row),
        out_shape=jax.ShapeDtypeStruct((n, D_MODEL), F32),
        scratch_shapes=[pltpu.VMEM((2, TOP_K, tt, D_MODEL), F32), pltpu.SemaphoreType.DMA((2,))],
        compiler_params=_cparams("arbitrary"),
        name="moe_combine",
    )(dest_flat, dest_flat, x1, g4, nf, ys)


def _rope_table(pos):
    half = MLA_ROPE // 2
    inv_freq = ROPE_THETA ** (-jnp.arange(half, dtype=F32) / half)
    ang = pos.astype(F32)[:, None] * inv_freq[None, :]
    cos, sin = jnp.cos(ang), jnp.sin(ang)
    return jnp.concatenate([cos, cos, -sin, sin], axis=-1)


def _regroup_w_in(w_in):
    o = np.cumsum((Q_LORA, KV_LORA, MLA_ROPE, GLA_DK, GLA_DK, GLA_DV, GLA_GATE_RANK, GLA_DV, 2 * D_MODEL))
    c_q, c_kv, k_r = w_in[:, :o[0]], w_in[:, o[0]:o[1]], w_in[:, o[1]:o[2]]
    gq, gk, gv = w_in[:, o[2]:o[3]], w_in[:, o[3]:o[4]], w_in[:, o[4]:o[5]]
    ga, gg, mg = w_in[:, o[5]:o[6]], w_in[:, o[6]:o[7]], w_in[:, o[7]:o[8]]
    half = MLA_ROPE // 2
    k_r_sw = jnp.concatenate([k_r[:, half:], k_r[:, :half]], axis=1)
    ga_pad = jnp.zeros((D_MODEL, LANES - GLA_GATE_RANK), w_in.dtype)
    return jnp.concatenate([c_q, c_kv, k_r, k_r_sw, ga, ga_pad, gv, gg, gq, gk, mg], axis=1).astype(BF16)


def _regroup_w_uq(w_uq):
    w = w_uq.reshape(Q_LORA, MLA_HEADS, MLA_NOPE + MLA_ROPE)
    half = MLA_ROPE // 2
    nope, rope = w[..., :MLA_NOPE], w[..., MLA_NOPE:]
    rope_sw = jnp.concatenate([rope[..., half:], rope[..., :half]], axis=-1)
    return jnp.concatenate([nope, rope, rope_sw], axis=-1).reshape(Q_LORA, MLA_HEADS * HEAD_PAD).astype(BF16)


def _regroup_w_ukv(w_ukv):
    w = w_ukv.reshape(KV_LORA, MLA_HEADS, MLA_NOPE + MLA_V)
    k = w[..., :MLA_NOPE].reshape(KV_LORA, MLA_HEADS * MLA_NOPE)
    v = w[..., MLA_NOPE:].reshape(KV_LORA, MLA_HEADS * MLA_V)
    return jnp.concatenate([k, v], axis=1).astype(BF16)


def _per_head_w_ukv(w_ukv):
    w = w_ukv.reshape(KV_LORA, MLA_HEADS, MLA_NOPE + MLA_V)
    w_uk = jnp.transpose(w[..., :MLA_NOPE], (1, 2, 0)).astype(BF16)
    w_uv = jnp.transpose(w[..., MLA_NOPE:], (1, 0, 2)).astype(BF16)
    return w_uk, w_uv


def _slot_plan(counts_first, counts_rest, n_blk):
    counts_first = counts_first.astype(jnp.int32)
    counts = counts_first + counts_rest.astype(jnp.int32)
    padded = (counts + MOE_ROWS - 1) // MOE_ROWS * MOE_ROWS
    pad_end = jnp.cumsum(padded)
    pad_start = pad_end - padded
    n_used = (pad_end[-1] // MOE_ROWS).reshape(1)
    blk_start = jnp.arange(n_blk, dtype=jnp.int32) * MOE_ROWS
    blk_exp = jnp.sum(blk_start[:, None] >= pad_end[None, :], axis=1).astype(jnp.int32)
    last_exp = blk_exp[jnp.maximum(n_used[0] - 1, 0)]
    blk_exp = jnp.where(jnp.arange(n_blk) < n_used[0], blk_exp, last_exp)
    ps_row = jnp.pad(pad_start.astype(F32), (0, LANES - N_EXPERTS))[None, :]
    zero_first = jnp.concatenate([(pad_start + counts_first) // MOE_ROWS, n_used]).astype(jnp.int32)
    zero_last = jnp.concatenate([pad_end // MOE_ROWS, jnp.full((1,), n_blk)]).astype(jnp.int32)
    return ps_row, zero_first, zero_last, blk_exp, n_used


def _mixers(x, past_lat, past_kr, s0, past_len, p):
    bsz, seq, _ = x.shape
    n = bsz * seq
    x2d = x.reshape(n, D_MODEL)
    hin = _in_proj(x2d, p["norm1_g"], p["w_all"])
    tm = min(512, n)
    cs = _rope_table(past_len + jnp.arange(seq))
    if seq < tm:
        cs = jnp.tile(cs, (tm // seq, 1))
    q, lat_new, krp_new = _mla_q(hin, cs, p["q_norm_g"], p["kv_norm_g"], p["w_uq"])
    if past_len:
        attn = _mla_cached(q, past_lat, past_kr, lat_new, krp_new, p["w_uk"], p["w_uv"], bsz, seq)
    else:
        kcat, v = _kv_up(lat_new, krp_new, p["w_ukv"])
        tq = min(1024, seq)
        tk = 1024 if seq % 1024 == 0 else seq
        attn = _flash(q.reshape(bsz, seq, -1), kcat.reshape(bsz, seq, -1), v.reshape(bsz, seq, -1), 0, tq, tk)
    og, s_new = _gla(hin, p["w_alpha"], p["b_alpha"], p["gla_norm_g"], s0, bsz, seq)
    x1, hm, sel, g4, cnt = _post_mix(attn.reshape(n, -1), og, hin, x2d, p["w_o_mla"], p["w_o_gla"], p["w_out"],
                                     p["norm2_g"], p["w_router"], p["b_router"])
    new_state = (lat_new.reshape(bsz, seq, KV_LORA), krp_new[:, :MLA_ROPE].reshape(bsz, seq, MLA_ROPE), s_new)
    return x1, hm, sel, g4, cnt, new_state


def kernel(x_prompt, x_sample, cache_mla_latent, cache_mla_krope, state_gla, norm1_g, w_in, q_norm_g, kv_norm_g,
           w_uq, w_ukv, w_o_mla, w_alpha2, b_alpha, gla_norm_g, w_o_gla, w_out, norm2_g, w_router, b_router,
           w1, b1, w2, b2, normf_g):
    depth = w_in.shape[0]
    assert depth == 1
    l = 0
    bp, sp, _ = x_prompt.shape
    bs, ss, _ = x_sample.shape
    past_len = cache_mla_latent.shape[2]
    p = {
        "norm1_g": norm1_g[l][None, :],
        "w_all": _regroup_w_in(w_in[l]),
        "q_norm_g": q_norm_g[l][None, :],
        "kv_norm_g": kv_norm_g[l][None, :],
        "w_uq": _regroup_w_uq(w_uq[l]),
        "w_ukv": _regroup_w_ukv(w_ukv[l]),
        "w_uk": _per_head_w_ukv(w_ukv[l])[0],
        "w_uv": _per_head_w_ukv(w_ukv[l])[1],
        "w_o_mla": w_o_mla[l].astype(BF16),
        "w_alpha": jnp.pad(w_alpha2[l], ((0, LANES - GLA_GATE_RANK), (0, 0))).astype(BF16),
        "b_alpha": b_alpha[l][None, :],
        "gla_norm_g": gla_norm_g[l][None, :],
        "w_o_gla": w_o_gla[l].astype(BF16),
        "w_out": w_out[l].astype(BF16),
        "norm2_g": norm2_g[l][None, :],
        "w_router": jnp.pad(w_router[l], ((0, 0), (0, LANES - N_EXPERTS))).astype(BF16),
        "b_router": jnp.pad(b_router[l], (0, LANES - N_EXPERTS))[None, :],
    }
    zeros_state = jnp.zeros((bp, GLA_HEADS, GLA_DK_HEAD, GLA_DV_HEAD), F32)
    x1p, hmp, selp, g4p, cntp, (lat_p, kr_p, st_p) = _mixers(x_prompt, None, None, zeros_state, 0, p)
    x1s, hms, sels, g4s, cnts, (lat_s, kr_s, st_s) = _mixers(x_sample, cache_mla_latent[l], cache_mla_krope[l],
                                                              state_gla[l], past_len, p)

    n_p, n_s = bp * sp, bs * ss
    n_asg = (n_p + n_s) * TOP_K
    n_slot = -(-n_asg // MOE_ROWS) * MOE_ROWS + N_EXPERTS * MOE_ROWS
    ps_row, zero_first, zero_last, blk_exp, n_used = _slot_plan(cntp[0, :N_EXPERTS], cnts[0, :N_EXPERTS],
                                                                n_slot // MOE_ROWS)
    dest_p = _dest(selp, ps_row).reshape(n_p * TOP_K)
    dest_s = _dest(sels, ps_row + cntp).reshape(n_s * TOP_K)
    xs = _scatter_init(zero_first, zero_last, dest_p, hmp, n_slot)
    xs = _scatter_more(dest_s, hms, xs)
    w1g_t, w1l_t = _w1_split(w1[l])
    b1g = b1[l][:, None, 0::2]
    b1lin = b1[l][:, None, 1::2]
    ys_slots = _experts(blk_exp, n_used, xs, w1g_t, w1l_t, b1g, b1lin, w2[l], b2[l][:, None, :])
    yp = _combine(dest_p, x1p, g4p, normf_g[None, :], ys_slots).reshape(bp, sp, D_MODEL)
    ys = _combine(dest_s, x1s, g4s, normf_g[None, :], ys_slots).reshape(bs, ss, D_MODEL)
    return (yp, ys, lat_p[None], kr_p[None], st_p[None], lat_s[None], kr_s[None], st_s[None])
```

```python
import functools

import jax
import jax.numpy as jnp
import numpy as np
from jax import lax
from jax.experimental import pallas as pl
from jax.experimental.pallas import tpu as pltpu

F32 = jnp.float32
BF16 = jnp.bfloat16

D_MODEL = 1024
CHUNK = 64
CHUNK_SHIFT = 6
LOG2_E = 1.4426950408889634
NORM_EPS = 1e-6
MLA_HEADS = 8
MLA_NOPE = 128
MLA_ROPE = 64
MLA_V = 128
Q_LORA = 512
KV_LORA = 256
ROPE_THETA = 10000.0
MLA_SCALE = (MLA_NOPE + MLA_ROPE) ** -0.5
NEG_INF = -1e30
GLA_HEADS = 4
GLA_DK = D_MODEL // 2
GLA_DV = D_MODEL
GLA_DK_HEAD = GLA_DK // GLA_HEADS
GLA_DV_HEAD = GLA_DV // GLA_HEADS
GLA_GATE_RANK = 16
GLA_GATE_TEMP = 16.0
GLA_SUB = 16
FLASH_HEADS = 2
POST_MIX_PARTS = 2
N_EXPERTS = 32
TOP_K = 4
D_EXPERT = D_MODEL
SWIGLU_LIMIT = 7.0
SWIGLU_ALPHA = 1.702
MOE_ROWS = 512

LANES = 128
HEAD_PAD = 256

COL_CQ = 0
COL_CKV = 512
COL_KR = 768
COL_GA = 896
COL_GV = 1024
COL_GG = 2048
COL_GQ = 3072
COL_GK = 3584
COL_MG = 4096
D_IN_PAD = 6144

VMEM_LIMIT = 56 * 1024 * 1024


def _cparams(*sem, flags=None):
    return pltpu.CompilerParams(dimension_semantics=sem, vmem_limit_bytes=VMEM_LIMIT, flags=flags)


def _tile(n, pref, mult=16):
    t = min(pref, n)
    t -= t % mult
    while n % t:
        t -= mult
    return t


def _rms(x, g):
    return x * lax.rsqrt(jnp.mean(x * x, axis=-1, keepdims=True) + NORM_EPS) * g


def _in_proj_kernel(x_ref, g_ref, w_ref, o_ref):
    h = _rms(x_ref[...], g_ref[...]).astype(BF16)
    tn = 1536
    for j in range(D_IN_PAD // tn):
        o_ref[:, j * tn:(j + 1) * tn] = jnp.dot(h, w_ref[:, j * tn:(j + 1) * tn], preferred_element_type=F32)


def _in_proj(x2d, g, w_all):
    n = x2d.shape[0]
    tm = min(256, n)
    return pl.pallas_call(
        _in_proj_kernel,
        grid=(n // tm,),
        in_specs=[
            pl.BlockSpec((tm, D_MODEL), lambda i: (i, 0)),
            pl.BlockSpec((1, D_MODEL), lambda i: (0, 0)),
            pl.BlockSpec((D_MODEL, D_IN_PAD), lambda i: (0, 0)),
        ],
        out_specs=pl.BlockSpec((tm, D_IN_PAD), lambda i: (i, 0)),
        out_shape=jax.ShapeDtypeStruct((n, D_IN_PAD), F32),
        compiler_params=_cparams("parallel"),
        name="in_proj",
    )(x2d, g, w_all)


def _rope_pair(blk, cs):
    t = blk * cs
    return t + pltpu.roll(t, MLA_ROPE, axis=1)


def _mla_q_kernel(h_ref, cs_ref, qg_ref, kvg_ref, wuq_ref, q_ref, lat_ref, kr_ref):
    cs = cs_ref[...]
    qn = _rms(h_ref[:, COL_CQ:COL_CQ + Q_LORA], qg_ref[...]).astype(BF16)
    q = jnp.dot(qn, wuq_ref[...], preferred_element_type=F32)
    for hh in range(MLA_HEADS):
        c0 = hh * HEAD_PAD
        q_ref[:, c0:c0 + MLA_NOPE] = q[:, c0:c0 + MLA_NOPE].astype(BF16)
        q_ref[:, c0 + MLA_NOPE:c0 + HEAD_PAD] = _rope_pair(q[:, c0 + MLA_NOPE:c0 + HEAD_PAD], cs).astype(BF16)
    lat_ref[...] = _rms(h_ref[:, COL_CKV:COL_CKV + KV_LORA], kvg_ref[...])
    r = _rope_pair(h_ref[:, COL_KR:COL_KR + LANES], cs)
    lane = lax.broadcasted_iota(jnp.int32, r.shape, 1)
    kr_ref[...] = jnp.where(lane < MLA_ROPE, r, 0.0)


def _mla_q(hin, cs, qg, kvg, wuq):
    n = hin.shape[0]
    tm = min(512, n)
    n_cs = cs.shape[0] // tm
    return pl.pallas_call(
        _mla_q_kernel,
        grid=(n // tm,),
        in_specs=[
            pl.BlockSpec((tm, COL_GA), lambda i: (i, 0)),
            pl.BlockSpec((tm, LANES), lambda i: (i % n_cs, 0)),
            pl.BlockSpec((1, Q_LORA), lambda i: (0, 0)),
            pl.BlockSpec((1, KV_LORA), lambda i: (0, 0)),
            pl.BlockSpec((Q_LORA, MLA_HEADS * HEAD_PAD), lambda i: (0, 0)),
        ],
        out_specs=[
            pl.BlockSpec((tm, MLA_HEADS * HEAD_PAD), lambda i: (i, 0)),
            pl.BlockSpec((tm, KV_LORA), lambda i: (i, 0)),
            pl.BlockSpec((tm, LANES), lambda i: (i, 0)),
        ],
        out_shape=[
            jax.ShapeDtypeStruct((n, MLA_HEADS * HEAD_PAD), BF16),
            jax.ShapeDtypeStruct((n, KV_LORA), F32),
            jax.ShapeDtypeStruct((n, LANES), F32),
        ],
        compiler_params=_cparams("parallel"),
        name="mla_q",
    )(hin, cs, qg, kvg, wuq)


def _kv_up_kernel(lat_ref, kr_ref, w_ref, k_ref, v_ref):
    kv = jnp.dot(lat_ref[...].astype(BF16), w_ref[...], preferred_element_type=F32)
    krb = kr_ref[...].astype(BF16)
    for hh in range(MLA_HEADS):
        c0 = hh * HEAD_PAD
        k_ref[:, c0:c0 + MLA_NOPE] = kv[:, hh * MLA_NOPE:(hh + 1) * MLA_NOPE].astype(BF16)
        k_ref[:, c0 + MLA_NOPE:c0 + HEAD_PAD] = krb
    v_ref[...] = kv[:, MLA_HEADS * MLA_NOPE:].astype(BF16)


def _kv_up(lat, krp, wukv):
    n = lat.shape[0]
    tm = _tile(n, 512)
    return pl.pallas_call(
        _kv_up_kernel,
        grid=(n // tm,),
        in_specs=[
            pl.BlockSpec((tm, KV_LORA), lambda i: (i, 0)),
            pl.BlockSpec((tm, LANES), lambda i: (i, 0)),
            pl.BlockSpec((KV_LORA, MLA_HEADS * (MLA_NOPE + MLA_V)), lambda i: (0, 0)),
        ],
        out_specs=[
            pl.BlockSpec((tm, MLA_HEADS * HEAD_PAD), lambda i: (i, 0)),
            pl.BlockSpec((tm, MLA_HEADS * MLA_V), lambda i: (i, 0)),
        ],
        out_shape=[
            jax.ShapeDtypeStruct((n, MLA_HEADS * HEAD_PAD), BF16),
            jax.ShapeDtypeStruct((n, MLA_HEADS * MLA_V), BF16),
        ],
        compiler_params=_cparams("parallel"),
        name="kv_up",
    )(lat, krp, wukv)


def _flash_kernel(q_ref, k_ref, v_ref, o_ref, *, tq, tk, past_len):
    qi = pl.program_id(2)
    q_pos0 = past_len + qi * tq
    q_chunk = (q_pos0 + lax.broadcasted_iota(jnp.int32, (tq, 1), 0)) >> CHUNK_SHIFT
    n_blocks = (q_pos0 + tq + tk - 1) // tk
    n_full = ((q_pos0 >> CHUNK_SHIFT) << CHUNK_SHIFT) // tk
    qs = [q_ref[0, :, hh * HEAD_PAD:(hh + 1) * HEAD_PAD] for hh in range(FLASH_HEADS)]

    def step(j, carry, masked):
        k0 = pl.multiple_of(j * tk, tk)
        if masked:
            k_chunk = (k0 + lax.broadcasted_iota(jnp.int32, (1, tk), 1)) >> CHUNK_SHIFT
        out = []
        for hh in range(FLASH_HEADS):
            m, l, acc = carry[hh]
            k = k_ref[0, pl.ds(k0, tk), hh * HEAD_PAD:(hh + 1) * HEAD_PAD]
            v = v_ref[0, pl.ds(k0, tk), hh * MLA_V:(hh + 1) * MLA_V]
            s = lax.dot_general(qs[hh], k, (((1,), (1,)), ((), ())), preferred_element_type=F32)
            s = s * (MLA_SCALE * LOG2_E)
            if masked:
                s = jnp.where(k_chunk <= q_chunk, s, NEG_INF)
            m_new = jnp.maximum(m, jnp.max(s, axis=-1, keepdims=True))
            alpha = jnp.exp2(m - m_new)
            p = jnp.exp2(s - m_new)
            l = alpha * l + jnp.sum(p, axis=-1, keepdims=True)
            acc = alpha * acc + jnp.dot(p.astype(BF16), v, preferred_element_type=F32)
            out.append((m_new, l, acc))
        return tuple(out)

    init = tuple((jnp.full((tq, 1), NEG_INF, F32), jnp.zeros((tq, 1), F32), jnp.zeros((tq, MLA_V), F32))
                 for _ in range(FLASH_HEADS))
    carry = lax.fori_loop(0, n_full, functools.partial(step, masked=False), init)
    carry = lax.fori_loop(n_full, n_blocks, functools.partial(step, masked=True), carry)
    for hh in range(FLASH_HEADS):
        _, l, acc = carry[hh]
        o_ref[0, :, hh * MLA_V:(hh + 1) * MLA_V] = (acc / l).astype(BF16)


def _flash(q, k, v, past_len, tq, tk):
    b, sq, _ = q.shape
    sk = k.shape[1]
    fh = FLASH_HEADS
    return pl.pallas_call(
        functools.partial(_flash_kernel, tq=tq, tk=tk, past_len=past_len),
        grid=(b, MLA_HEADS // fh, sq // tq),
        in_specs=[
            pl.BlockSpec((1, tq, fh * HEAD_PAD), lambda bi, h, i: (bi, i, h)),
            pl.BlockSpec((1, sk, fh * HEAD_PAD), lambda bi, h, i: (bi, 0, h)),
            pl.BlockSpec((1, sk, fh * MLA_V), lambda bi, h, i: (bi, 0, h)),
        ],
        out_specs=pl.BlockSpec((1, tq, fh * MLA_V), lambda bi, h, i: (bi, i, h)),
        out_shape=jax.ShapeDtypeStruct((b, sq, MLA_HEADS * MLA_V), BF16),
        compiler_params=_cparams("parallel", "parallel", "arbitrary"),
        name="flash",
    )(q, k, v)


def _mla_cached_kernel(q_ref, clat_ref, ckr_ref, nlat_ref, nkr_ref, wuk_ref, wuv_ref, o_ref, *, past_len):
    seq = q_ref.shape[0]
    nt = (((1,), (1,)), ((), ()))
    q_abs = jnp.concatenate(
        [jnp.dot(q_ref[:, hh * HEAD_PAD:hh * HEAD_PAD + MLA_NOPE], wuk_ref[hh], preferred_element_type=F32)
         for hh in range(MLA_HEADS)], axis=0).astype(BF16)
    q_rot = jnp.concatenate([q_ref[:, hh * HEAD_PAD + MLA_NOPE:(hh + 1) * HEAD_PAD] for hh in range(MLA_HEADS)],
                            axis=0)
    pos_q = past_len + lax.broadcasted_iota(jnp.int32, (seq, 1), 0)
    q_chunk = jnp.concatenate([pos_q] * MLA_HEADS, axis=0) >> CHUNK_SHIFT

    def scores(lat, kr, k_pos0):
        n_k = lat.shape[0]
        kr_pad = jnp.concatenate([kr, jnp.zeros_like(kr)], axis=1)
        s = (lax.dot_general(q_abs, lat, nt, preferred_element_type=F32)
             + lax.dot_general(q_rot, kr_pad, nt, preferred_element_type=F32)) * (MLA_SCALE * LOG2_E)
        k_chunk = (k_pos0 + lax.broadcasted_iota(jnp.int32, (1, n_k), 1)) >> CHUNK_SHIFT
        return jnp.where(k_chunk <= q_chunk, s, NEG_INF)

    lat_c = clat_ref[0].astype(BF16)
    lat_n = nlat_ref[...].astype(BF16)
    s_c = scores(lat_c, ckr_ref[0].astype(BF16), 0)
    s_n = scores(lat_n, nkr_ref[:, :MLA_ROPE].astype(BF16), past_len)
    m = jnp.maximum(jnp.max(s_c, axis=-1, keepdims=True), jnp.max(s_n, axis=-1, keepdims=True))
    p_c = jnp.exp2(s_c - m)
    p_n = jnp.exp2(s_n - m)
    l = jnp.sum(p_c, axis=-1, keepdims=True) + jnp.sum(p_n, axis=-1, keepdims=True)
    o_lat = (jnp.dot(p_c.astype(BF16), lat_c, preferred_element_type=F32)
             + jnp.dot(p_n.astype(BF16), lat_n, preferred_element_type=F32)) / l
    o_lat = o_lat.astype(BF16)
    for hh in range(MLA_HEADS):
        o_ref[:, hh * MLA_V:(hh + 1) * MLA_V] = jnp.dot(o_lat[hh * seq:(hh + 1) * seq], wuv_ref[hh],
                                                         preferred_element_type=F32).astype(BF16)


def _mla_cached(q, cache_lat, cache_kr, lat_new, krp_new, w_uk, w_uv, bsz, seq):
    past_len = cache_lat.shape[1]
    return pl.pallas_call(
        functools.partial(_mla_cached_kernel, past_len=past_len),
        grid=(bsz,),
        in_specs=[
            pl.BlockSpec((seq, MLA_HEADS * HEAD_PAD), lambda b: (b, 0)),
            pl.BlockSpec((1, past_len, KV_LORA), lambda b: (b, 0, 0)),
            pl.BlockSpec((1, past_len, MLA_ROPE), lambda b: (b, 0, 0)),
            pl.BlockSpec((seq, KV_LORA), lambda b: (b, 0)),
            pl.BlockSpec((seq, LANES), lambda b: (b, 0)),
            pl.BlockSpec((MLA_HEADS, MLA_NOPE, KV_LORA), lambda b: (0, 0, 0)),
            pl.BlockSpec((MLA_HEADS, KV_LORA, MLA_V), lambda b: (0, 0, 0)),
        ],
        out_specs=pl.BlockSpec((seq, MLA_HEADS * MLA_V), lambda b: (b, 0)),
        out_shape=jax.ShapeDtypeStruct((bsz * seq, MLA_HEADS * MLA_V), BF16),
        compiler_params=_cparams("parallel"),
        name="mla_cached",
    )(q, cache_lat, cache_kr, lat_new, krp_new, w_uk, w_uv)


def _cumsum_rows(x):
    n = x.shape[0]
    row = lax.broadcasted_iota(jnp.int32, x.shape, 0)
    s = 1
    while s < n:
        x = x + jnp.where(row >= s, pltpu.roll(x, s, axis=0), 0.0)
        s *= 2
    return x


def _gla_kernel(gv_ref, gg_ref, gq_ref, gk_ref, ga_ref, wa_ref, ba_ref, ng_ref, s0_ref, og_ref, sn_ref, st_scr,
                *, csize, n_chunks):
    t = pl.program_id(1)

    @pl.when(t == 0)
    def _():
        for hh in range(GLA_HEADS):
            st_scr[hh] = s0_ref[0, hh].T

    wa = wa_ref[...]
    ba = ba_ref[...]
    ng = ng_ref[...]
    n_sub = csize // GLA_SUB
    col_id = lax.broadcasted_iota(jnp.int32, (GLA_SUB, csize), 1)
    row_in_sub = lax.broadcasted_iota(jnp.int32, (GLA_SUB, csize), 0)

    def chunk(c, carry):
        r0 = pl.multiple_of(c * csize, csize)
        rows = pl.ds(r0, csize)
        x = jnp.dot(ga_ref[rows, :].astype(BF16), wa, preferred_element_type=F32) + ba
        la = jax.nn.log_sigmoid(x) / GLA_GATE_TEMP
        b_all = _cumsum_rows(la) * LOG2_E
        for hh in range(GLA_HEADS):
            head(hh, rows, b_all[:, hh * GLA_DK_HEAD:(hh + 1) * GLA_DK_HEAD])
        return carry

    def head(hh, rows, b):
        kcols = slice(hh * GLA_DK_HEAD, (hh + 1) * GLA_DK_HEAD)
        vcols = slice(hh * GLA_DV_HEAD, (hh + 1) * GLA_DV_HEAD)
        q = gq_ref[rows, kcols] * (GLA_DK_HEAD ** -0.5)
        k = gk_ref[rows, kcols]
        v = gv_ref[rows, vcols]

        a_rows = []
        for i in range(n_sub):
            lo = i * GLA_SUB
            bi = b[lo:lo + GLA_SUB]
            qi = q[lo:lo + GLA_SUB]
            ki = k[lo:lo + GLA_SUB]
            if i == 0:
                a_i = jnp.zeros((GLA_SUB, csize), F32)
            else:
                ri = b[lo - 1:lo]
                qs = qi * jnp.exp2(bi - ri)
                ks = k * jnp.exp2(jnp.minimum(ri - b, 0.0))
                a_i = lax.dot_general(qs.astype(BF16), ks.astype(BF16), (((1,), (1,)), ((), ())),
                                      preferred_element_type=F32)
            for s in range(GLA_SUB):
                col = jnp.sum(qi * ki[s:s + 1] * jnp.exp2(bi - bi[s:s + 1]), axis=-1, keepdims=True)
                a_i = jnp.where(col_id == lo + s, col, a_i)
            a_rows.append(jnp.where(col_id <= lo + row_in_sub, a_i, 0.0))
        a = jnp.concatenate(a_rows, axis=0)

        st = st_scr[hh]
        vb = v.astype(BF16)
        o = jnp.dot(a.astype(BF16), vb, preferred_element_type=F32)
        o = o + lax.dot_general((q * jnp.exp2(b)).astype(BF16), st.astype(BF16), (((1,), (1,)), ((), ())),
                                preferred_element_type=F32)
        b_end = b[csize - 1:csize]
        kd = (k * jnp.exp2(b_end - b)).astype(BF16)
        st_scr[hh] = jnp.exp2(b_end) * st + jnp.dot(v.T.astype(BF16), kd, preferred_element_type=F32)
        on = _rms(o, ng)
        og_ref[rows, vcols] = (on * jax.nn.silu(gg_ref[rows, vcols])).astype(BF16)

    lax.fori_loop(0, n_chunks, chunk, 0)

    @pl.when(t == pl.num_programs(1) - 1)
    def _():
        for hh in range(GLA_HEADS):
            sn_ref[0, hh] = st_scr[hh].T


def _gla(hin, wa, ba, ng, s0, bsz, seq):
    csize = min(CHUNK, seq)
    tt = min(512, seq)
    n_t = seq // tt
    n = bsz * seq
    kern = functools.partial(_gla_kernel, csize=csize, n_chunks=tt // csize)
    state = pl.BlockSpec((1, GLA_HEADS, GLA_DK_HEAD, GLA_DV_HEAD), lambda b, t: (b, 0, 0, 0))
    return pl.pallas_call(
        kern,
        grid=(bsz, n_t),
        in_specs=[
            pl.BlockSpec((tt, GLA_DV), lambda b, t: (b * n_t + t, COL_GV // GLA_DV)),
            pl.BlockSpec((tt, GLA_DV), lambda b, t: (b * n_t + t, COL_GG // GLA_DV)),
            pl.BlockSpec((tt, GLA_DK), lambda b, t: (b * n_t + t, COL_GQ // GLA_DK)),
            pl.BlockSpec((tt, GLA_DK), lambda b, t: (b * n_t + t, COL_GK // GLA_DK)),
            pl.BlockSpec((tt, LANES), lambda b, t: (b * n_t + t, COL_GA // LANES)),
            pl.BlockSpec((LANES, GLA_DK), lambda b, t: (0, 0)),
            pl.BlockSpec((1, GLA_DK), lambda b, t: (0, 0)),
            pl.BlockSpec((1, GLA_DV_HEAD), lambda b, t: (0, 0)),
            state,
        ],
        out_specs=[pl.BlockSpec((tt, GLA_DV), lambda b, t: (b * n_t + t, 0)), state],
        out_shape=[
            jax.ShapeDtypeStruct((n, GLA_DV), BF16),
            jax.ShapeDtypeStruct((bsz, GLA_HEADS, GLA_DK_HEAD, GLA_DV_HEAD), F32),
        ],
        scratch_shapes=[pltpu.VMEM((GLA_HEADS, GLA_DV_HEAD, GLA_DK_HEAD), F32)],
        compiler_params=_cparams("parallel", "arbitrary"),
        name="gla",
    )(hin, hin, hin, hin, hin, wa, ba, ng, s0)


def _post_mix_kernel(at_ref, og_ref, mg1_ref, mg2_ref, x_ref, wom_ref, wog_ref, wout_ref, n2_ref, wr_ref, br_ref,
                     x1_ref, hm_ref, sel_ref, g4_ref, cnt_ref):
    @pl.when(pl.program_id(0) == 0)
    def _():
        cnt_ref[...] = jnp.zeros_like(cnt_ref)

    tm = x_ref.shape[0]
    half = tm // POST_MIX_PARTS
    for part in range(POST_MIX_PARTS):
        rows = slice(part * half, (part + 1) * half)
        y_mla = jnp.dot(at_ref[rows, :], wom_ref[...], preferred_element_type=F32)
        y_gla = jnp.dot(og_ref[rows, :], wog_ref[...], preferred_element_type=F32)
        m = jax.nn.sigmoid(mg1_ref[rows, :]) * y_mla + jax.nn.sigmoid(mg2_ref[rows, :]) * y_gla
        x1 = x_ref[rows, :] + jnp.dot(m.astype(BF16), wout_ref[...], preferred_element_type=F32)
        x1_ref[rows, :] = x1
        hm = _rms(x1, n2_ref[...])
        hm_ref[rows, :] = hm
        logits = jnp.dot(hm.astype(BF16), wr_ref[...], preferred_element_type=F32) + br_ref[...]

        lane = lax.broadcasted_iota(jnp.int32, logits.shape, 1)
        work = jnp.where(lane < N_EXPERTS, logits, -jnp.inf)
        sel = jnp.zeros(logits.shape, F32)
        vals = []
        for k in range(TOP_K):
            mk = jnp.max(work, axis=-1, keepdims=True)
            ik = jnp.min(jnp.where(work == mk, lane, LANES), axis=-1, keepdims=True)
            hit = lane == ik
            sel = jnp.where(hit, float(k + 1), sel)
            work = jnp.where(hit, -jnp.inf, work)
            vals.append(mk)
        sel_ref[rows, :] = sel
        ex = [jnp.exp(v - vals[0]) for v in vals]
        den = ex[0]
        for e in ex[1:]:
            den = den + e
        g4 = jnp.zeros(logits.shape, F32)
        for k in range(TOP_K):
            g4 = jnp.where(lane == k, ex[k] / den, g4)
        g4_ref[rows, :] = g4[:, :TOP_K]
        cnt_ref[...] += jnp.sum(jnp.where(sel > 0.0, 1.0, 0.0), axis=0, keepdims=True)


def _post_mix(attn, og, hin, x2d, wom, wog, wout, n2, wr, br):
    n = x2d.shape[0]
    tm = min(256 * POST_MIX_PARTS, n)
    row = lambda i: (i, 0)
    const = lambda i: (0, 0)
    return pl.pallas_call(
        _post_mix_kernel,
        grid=(n // tm,),
        in_specs=[
            pl.BlockSpec((tm, D_MODEL), row),
            pl.BlockSpec((tm, D_MODEL), row),
            pl.BlockSpec((tm, D_MODEL), lambda i: (i, COL_MG // D_MODEL)),
            pl.BlockSpec((tm, D_MODEL), lambda i: (i, COL_MG // D_MODEL + 1)),
            pl.BlockSpec((tm, D_MODEL), row),
            pl.BlockSpec((D_MODEL, D_MODEL), const),
            pl.BlockSpec((D_MODEL, D_MODEL), const),
            pl.BlockSpec((D_MODEL, D_MODEL), const),
            pl.BlockSpec((1, D_MODEL), const),
            pl.BlockSpec((D_MODEL, LANES), const),
            pl.BlockSpec((1, LANES), const),
        ],
        out_specs=[
            pl.BlockSpec((tm, D_MODEL), row),
            pl.BlockSpec((tm, D_MODEL), row),
            pl.BlockSpec((tm, LANES), row),
            pl.BlockSpec((tm, TOP_K), row),
            pl.BlockSpec((1, LANES), const),
        ],
        out_shape=[
            jax.ShapeDtypeStruct((n, D_MODEL), F32),
            jax.ShapeDtypeStruct((n, D_MODEL), F32),
            jax.ShapeDtypeStruct((n, LANES), F32),
            jax.ShapeDtypeStruct((n, TOP_K), F32),
            jax.ShapeDtypeStruct((1, LANES), F32),
        ],
        compiler_params=_cparams("arbitrary"),
        name="post_mix",
    )(attn, og, hin, hin, x2d, wom, wog, wout, n2, wr, br)


def _dest_kernel(sel_ref, ps_ref, dest_ref, run_scr):
    @pl.when(pl.program_id(0) == 0)
    def _():
        run_scr[...] = ps_ref[...]

    sel = sel_ref[...]
    tt = sel.shape[0]
    oh = jnp.where(sel > 0.0, 1.0, 0.0)
    r = lax.broadcasted_iota(jnp.int32, (tt, tt), 0)
    c = lax.broadcasted_iota(jnp.int32, (tt, tt), 1)
    ltri = jnp.where(r > c, 1.0, 0.0).astype(BF16)
    slot = jnp.dot(ltri, oh.astype(BF16), preferred_element_type=F32) + run_scr[...]
    run_scr[...] += jnp.sum(oh, axis=0, keepdims=True)
    lane = lax.broadcasted_iota(jnp.int32, sel.shape, 1)
    d = jnp.zeros(sel.shape, F32)
    for k in range(TOP_K):
        col = jnp.sum(jnp.where(sel == float(k + 1), slot, 0.0), axis=-1, keepdims=True)
        d = jnp.where(lane == k, col, d)
    dest_ref[...] = d[:, :TOP_K].astype(jnp.int32)


def _dest(sel, pad_start):
    n = sel.shape[0]
    tt = _tile(n, 512)
    return pl.pallas_call(
        _dest_kernel,
        grid=(n // tt,),
        in_specs=[pl.BlockSpec((tt, LANES), lambda i: (i, 0)), pl.BlockSpec((1, LANES), lambda i: (0, 0))],
        out_specs=pl.BlockSpec((tt, TOP_K), lambda i: (i, 0)),
        out_shape=jax.ShapeDtypeStruct((n, TOP_K), jnp.int32),
        scratch_shapes=[pltpu.VMEM((1, LANES), F32)],
        compiler_params=_cparams("arbitrary"),
        name="route_dest",
    )(sel, pad_start)


def _row_copy(src, src_row, dst, dst_row, sem):
    return pltpu.make_async_copy(src.at[pl.ds(src_row, 1)], dst.at[pl.ds(dst_row, 1)], sem)


def _scatter_rows(dest_ref, hm_ref, xs_ref, sem):
    tt = hm_ref.shape[0]

    def issue(t, carry):
        for k in range(TOP_K):
            _row_copy(hm_ref, t, xs_ref, dest_ref[t * TOP_K + k], sem).start()
        return carry

    lax.fori_loop(0, tt, issue, 0)
    for k in range(TOP_K):
        pltpu.make_async_copy(hm_ref, xs_ref.at[pl.ds(0, tt)], sem).wait()


def _scatter_init_kernel(zf_ref, zl_ref, dest_ref, hm_hbm, xs_ref, hbuf, zero_scr, lsems, ssems, zsem, *, n, tt):
    i = pl.program_id(0)

    def load(b, slot):
        return pltpu.make_async_copy(hm_hbm.at[pl.ds(pl.multiple_of(b * tt, tt), tt)], hbuf.at[slot], lsems.at[slot])

    def wait_rows(slot):
        for _ in range(TOP_K):
            pltpu.make_async_copy(hbuf.at[slot], xs_ref.at[pl.ds(0, tt)], ssems.at[slot]).wait()

    @pl.when(i == 0)
    def _():
        load(0, 0).start()
        zero_scr[...] = jnp.zeros_like(zero_scr)

        def zero_block(blk):
            return pltpu.make_async_copy(zero_scr,
                                         xs_ref.at[pl.ds(pl.multiple_of(blk * MOE_ROWS, MOE_ROWS), MOE_ROWS)], zsem)

        def start_range(e, carry):
            return lax.fori_loop(zf_ref[e], zl_ref[e], lambda blk, c: (zero_block(blk).start(), c)[1], carry)

        def wait_range(e, carry):
            return lax.fori_loop(zf_ref[e], zl_ref[e], lambda blk, c: (zero_block(blk).wait(), c)[1], carry)

        lax.fori_loop(0, N_EXPERTS + 1, start_range, 0)
        lax.fori_loop(0, N_EXPERTS + 1, wait_range, 0)

    for slot in range(3):
        @pl.when(i % 3 == slot)
        def _():
            @pl.when(i + 1 < n)
            def _():
                load(i + 1, (slot + 1) % 3).start()

            load(i, slot).wait()

            def issue(t, carry):
                for k in range(TOP_K):
                    _row_copy(hbuf.at[slot], t, xs_ref, dest_ref[t * TOP_K + k], ssems.at[slot]).start(priority=k % 2)
                return carry

            lax.fori_loop(0, tt, issue, 0)

            @pl.when(i >= 1)
            def _():
                wait_rows((slot + 2) % 3)

            @pl.when(i == n - 1)
            def _():
                wait_rows(slot)


def _scatter_more_kernel(dest_ref, hm_ref, xs_in, xs_ref, sem):
    del xs_in
    _scatter_rows(dest_ref, hm_ref, xs_ref, sem)


def _scatter_init(zero_first, zero_last, dest_flat, hm, n_slot):
    n = hm.shape[0]
    tt = _tile(n, 512)
    n_t = n // tt
    grid_spec = pltpu.PrefetchScalarGridSpec(
        num_scalar_prefetch=2,
        grid=(n_t,),
        in_specs=[
            pl.BlockSpec((tt * TOP_K,), lambda i, zf, zl: (i,), memory_space=pltpu.SMEM),
            pl.BlockSpec(memory_space=pl.ANY),
        ],
        out_specs=pl.BlockSpec(memory_space=pl.ANY),
        scratch_shapes=[pltpu.VMEM((3, tt, D_MODEL), F32), pltpu.VMEM((MOE_ROWS, D_MODEL), F32),
                        pltpu.SemaphoreType.DMA((3,)), pltpu.SemaphoreType.DMA((3,)), pltpu.SemaphoreType.DMA(())],
    )
    return pl.pallas_call(
        functools.partial(_scatter_init_kernel, n=n_t, tt=tt),
        grid_spec=grid_spec,
        out_shape=jax.ShapeDtypeStruct((n_slot, D_MODEL), F32),
        compiler_params=_cparams("arbitrary"),
        name="moe_scatter_init",
    )(zero_first, zero_last, dest_flat, hm)


def _scatter_more(dest_flat, hm, xs):
    n = hm.shape[0]
    tt = _tile(n, 256)
    return pl.pallas_call(
        _scatter_more_kernel,
        grid=(n // tt,),
        in_specs=[
            pl.BlockSpec((tt * TOP_K,), lambda i: (i,), memory_space=pltpu.SMEM),
            pl.BlockSpec((tt, D_MODEL), lambda i: (i, 0)),
            pl.BlockSpec(memory_space=pl.ANY),
        ],
        out_specs=pl.BlockSpec(memory_space=pl.ANY),
        out_shape=jax.ShapeDtypeStruct(xs.shape, xs.dtype),
        scratch_shapes=[pltpu.SemaphoreType.DMA(())],
        input_output_aliases={2: 0},
        compiler_params=_cparams("arbitrary"),
        name="moe_scatter",
    )(dest_flat, hm, xs)


def _w1_split_kernel(w_ref, g_ref, l_ref, t_scr):
    n_slab = w_ref.shape[1] // LANES
    for c in range(n_slab):
        t_scr[c] = w_ref[0, c * LANES:(c + 1) * LANES, :].T
    for c in range(n_slab):
        g_ref[0, :, c * LANES:(c + 1) * LANES] = t_scr[c, pl.ds(0, D_EXPERT, stride=2), :].astype(BF16)
        l_ref[0, :, c * LANES:(c + 1) * LANES] = t_scr[c, pl.ds(1, D_EXPERT, stride=2), :].astype(BF16)


def _w1_split(w1):
    ks = 512
    n_slab = ks // LANES
    out = jax.ShapeDtypeStruct((N_EXPERTS, D_EXPERT, D_MODEL), BF16)
    return pl.pallas_call(
        _w1_split_kernel,
        grid=(N_EXPERTS, D_MODEL // ks),
        in_specs=[pl.BlockSpec((1, ks, 2 * D_EXPERT), lambda e, j: (e, j, 0))],
        out_specs=[pl.BlockSpec((1, D_EXPERT, ks), lambda e, j: (e, 0, j)),
                   pl.BlockSpec((1, D_EXPERT, ks), lambda e, j: (e, 0, j))],
        out_shape=[out, out],
        scratch_shapes=[pltpu.VMEM((n_slab, 2 * D_EXPERT, LANES), F32)],
        compiler_params=_cparams("parallel", "parallel"),
        name="w1_split",
    )(w1)


def _expert_kernel(be_ref, nb_ref, x_ref, w1g_ref, w1l_ref, b1g_ref, b1l_ref, w2_ref, b2_ref, o_ref, w2b_scr):
    i = pl.program_id(0)
    used = i < nb_ref[0]

    @pl.when(jnp.logical_not(used))
    def _():
        o_ref[...] = jnp.zeros_like(o_ref)

    @pl.when(jnp.logical_or(i == 0, be_ref[i] != be_ref[jnp.maximum(i - 1, 0)]))
    def _():
        w2b_scr[...] = w2_ref[0].astype(BF16)

    @pl.when(used)
    def _():
        x = x_ref[...].astype(BF16)
        nt = (((1,), (1,)), ((), ()))
        hg = lax.dot_general(x, w1g_ref[0], nt, preferred_element_type=F32) + b1g_ref[0]
        hl = lax.dot_general(x, w1l_ref[0], nt, preferred_element_type=F32) + b1l_ref[0]
        glu = jnp.minimum(hg, SWIGLU_LIMIT)
        lin = jnp.clip(hl, -SWIGLU_LIMIT, SWIGLU_LIMIT)
        act = glu * jax.nn.sigmoid(SWIGLU_ALPHA * glu) * (lin + 1.0)
        o_ref[...] = jnp.dot(act.astype(BF16), w2b_scr[...], preferred_element_type=F32) + b2_ref[0]


def _experts(blk_exp, n_used, xs, w1g_t, w1l_t, b1g, b1l, w2, b2):
    n_slot = xs.shape[0]
    n_blk = n_slot // MOE_ROWS
    row_in = lambda i, be, nb: (jnp.minimum(i, nb[0] - 1), 0)
    row = lambda i, be, nb: (i, 0)
    wsel = lambda i, be, nb: (be[i], 0, 0)
    grid_spec = pltpu.PrefetchScalarGridSpec(
        num_scalar_prefetch=2,
        grid=(n_blk,),
        in_specs=[
            pl.BlockSpec((MOE_ROWS, D_MODEL), row_in),
            pl.BlockSpec((1, D_EXPERT, D_MODEL), wsel),
            pl.BlockSpec((1, D_EXPERT, D_MODEL), wsel),
            pl.BlockSpec((1, 1, D_EXPERT), wsel),
            pl.BlockSpec((1, 1, D_EXPERT), wsel),
            pl.BlockSpec((1, D_EXPERT, D_MODEL), wsel),
            pl.BlockSpec((1, 1, D_MODEL), wsel),
        ],
        out_specs=pl.BlockSpec((MOE_ROWS, D_MODEL), row),
        scratch_shapes=[pltpu.VMEM((D_EXPERT, D_MODEL), BF16)],
    )
    return pl.pallas_call(
        _expert_kernel,
        grid_spec=grid_spec,
        out_shape=jax.ShapeDtypeStruct((n_slot, D_MODEL), F32),
        compiler_params=_cparams("arbitrary"),
        name="experts",
    )(blk_exp, n_used, xs, w1g_t, w1l_t, b1g, b1l, w2, b2)


def _combine_kernel(dest_ref, dnext_ref, x1_ref, g4_ref, nf_ref, ys_ref, o_ref, buf, sems, *, n):
    tt = x1_ref.shape[0]
    i = pl.program_id(0)

    def issue(d_ref, slot):
        def body(t, carry):
            for k in range(TOP_K):
                _row_copy(ys_ref, d_ref[t * TOP_K + k], buf.at[slot, k], t, sems.at[slot]).start(priority=k % 2)
            return carry

        lax.fori_loop(0, tt, body, 0, unroll=8)

    def finish(slot):
        for k in range(TOP_K):
            pltpu.make_async_copy(ys_ref.at[pl.ds(0, tt)], buf.at[slot, k], sems.at[slot]).wait()
        g4 = g4_ref[...]
        moe = g4[:, 0:1] * buf[slot, 0]
        for k in range(1, TOP_K):
            moe = moe + g4[:, k:k + 1] * buf[slot, k]
        o_ref[...] = _rms(x1_ref[...] + moe, nf_ref[...])

    @pl.when(i == 0)
    def _():
        issue(dest_ref, 0)

    for parity in range(2):
        @pl.when(i % 2 == parity)
        def _():
            @pl.when(i + 1 < n)
            def _():
                issue(dnext_ref, 1 - parity)

            finish(parity)


def _combine(dest_flat, x1, g4, nf, ys):
    n = x1.shape[0]
    tt = _tile(n, 256)
    n_t = n // tt
    row = lambda i: (i, 0)
    return pl.pallas_call(
        functools.partial(_combine_kernel, n=n_t),
        grid=(n_t,),
        in_specs=[
            pl.BlockSpec((tt * TOP_K,), lambda i: (i,), memory_space=pltpu.SMEM),
            pl.BlockSpec((tt * TOP_K,), lambda i: (jnp.minimum(i + 1, n_t - 1),), memory_space=pltpu.SMEM),
            pl.BlockSpec((tt, D_MODEL), row),
            pl.BlockSpec((tt, TOP_K), row),
            pl.BlockSpec((1, D_MODEL), lambda i: (0, 0)),
            pl.BlockSpec(memory_space=pl.ANY),
        ],
        out_specs=pl.BlockSpec((tt, D_MODEL), row),
        out_shape=jax.ShapeDtypeStruct((n, D_MODEL), F32),
        scratch_shapes=[pltpu.VMEM((2, TOP_K, tt, D_MODEL), F32), pltpu.SemaphoreType.DMA((2,))],
        compiler_params=_cparams("arbitrary"),
        name="moe_combine",
    )(dest_flat, dest_flat, x1, g4, nf, ys)


def _rope_table(pos):
    half = MLA_ROPE // 2
    inv_freq = ROPE_THETA ** (-jnp.arange(half, dtype=F32) / half)
    ang = pos.astype(F32)[:, None] * inv_freq[None, :]
    cos, sin = jnp.cos(ang), jnp.sin(ang)
    return jnp.concatenate([cos, cos, -sin, sin], axis=-1)


def _regroup_w_in(w_in):
    o = np.cumsum((Q_LORA, KV_LORA, MLA_ROPE, GLA_DK, GLA_DK, GLA_DV, GLA_GATE_RANK, GLA_DV, 2 * D_MODEL))
    c_q, c_kv, k_r = w_in[:, :o[0]], w_in[:, o[0]:o[1]], w_in[:, o[1]:o[2]]
    gq, gk, gv = w_in[:, o[2]:o[3]], w_in[:, o[3]:o[4]], w_in[:, o[4]:o[5]]
    ga, gg, mg = w_in[:, o[5]:o[6]], w_in[:, o[6]:o[7]], w_in[:, o[7]:o[8]]
    half = MLA_ROPE // 2
    k_r_sw = jnp.concatenate([k_r[:, half:], k_r[:, :half]], axis=1)
    ga_pad = jnp.zeros((D_MODEL, LANES - GLA_GATE_RANK), w_in.dtype)
    return jnp.concatenate([c_q, c_kv, k_r, k_r_sw, ga, ga_pad, gv, gg, gq, gk, mg], axis=1).astype(BF16)


def _regroup_w_uq(w_uq):
    w = w_uq.reshape(Q_LORA, MLA_HEADS, MLA_NOPE + MLA_ROPE)
    half = MLA_ROPE // 2
    nope, rope = w[..., :MLA_NOPE], w[..., MLA_NOPE:]
    rope_sw = jnp.concatenate([rope[..., half:], rope[..., :half]], axis=-1)
    return jnp.concatenate([nope, rope, rope_sw], axis=-1).reshape(Q_LORA, MLA_HEADS * HEAD_PAD).astype(BF16)


def _regroup_w_ukv(w_ukv):
    w = w_ukv.reshape(KV_LORA, MLA_HEADS, MLA_NOPE + MLA_V)
    k = w[..., :MLA_NOPE].reshape(KV_LORA, MLA_HEADS * MLA_NOPE)
    v = w[..., MLA_NOPE:].reshape(KV_LORA, MLA_HEADS * MLA_V)
    return jnp.concatenate([k, v], axis=1).astype(BF16)


def _per_head_w_ukv(w_ukv):
    w = w_ukv.reshape(KV_LORA, MLA_HEADS, MLA_NOPE + MLA_V)
    w_uk = jnp.transpose(w[..., :MLA_NOPE], (1, 2, 0)).astype(BF16)
    w_uv = jnp.transpose(w[..., MLA_NOPE:], (1, 0, 2)).astype(BF16)
    return w_uk, w_uv


def _slot_plan(counts_first, counts_rest, n_blk):
    counts_first = counts_first.astype(jnp.int32)
    counts = counts_first + counts_rest.astype(jnp.int32)
    padded = (counts + MOE_ROWS - 1) // MOE_ROWS * MOE_ROWS
    pad_end = jnp.cumsum(padded)
    pad_start = pad_end - padded
    n_used = (pad_end[-1] // MOE_ROWS).reshape(1)
    blk_start = jnp.arange(n_blk, dtype=jnp.int32) * MOE_ROWS
    blk_exp = jnp.sum(blk_start[:, None] >= pad_end[None, :], axis=1).astype(jnp.int32)
    last_exp = blk_exp[jnp.maximum(n_used[0] - 1, 0)]
    blk_exp = jnp.where(jnp.arange(n_blk) < n_used[0], blk_exp, last_exp)
    ps_row = jnp.pad(pad_start.astype(F32), (0, LANES - N_EXPERTS))[None, :]
    zero_first = jnp.concatenate([(pad_start + counts_first) // MOE_ROWS, n_used]).astype(jnp.int32)
    zero_last = jnp.concatenate([pad_end // MOE_ROWS, jnp.full((1,), n_blk)]).astype(jnp.int32)
    return ps_row, zero_first, zero_last, blk_exp, n_used


def _mixers(x, past_lat, past_kr, s0, past_len, p):
    bsz, seq, _ = x.shape
    n = bsz * seq
    x2d = x.reshape(n, D_MODEL)
    hin = _in_proj(x2d, p["norm1_g"], p["w_all"])
    tm = min(512, n)
    cs = _rope_table(past_len + jnp.arange(seq))
    if seq < tm:
        cs = jnp.tile(cs, (tm // seq, 1))
    q, lat_new, krp_new = _mla_q(hin, cs, p["q_norm_g"], p["kv_norm_g"], p["w_uq"])
    if past_len:
        attn = _mla_cached(q, past_lat, past_kr, lat_new, krp_new, p["w_uk"], p["w_uv"], bsz, seq)
    else:
        kcat, v = _kv_up(lat_new, krp_new, p["w_ukv"])
        tq = min(1024, seq)
        tk = 1024 if seq % 1024 == 0 else seq
        attn = _flash(q.reshape(bsz, seq, -1), kcat.reshape(bsz, seq, -1), v.reshape(bsz, seq, -1), 0, tq, tk)
    og, s_new = _gla(hin, p["w_alpha"], p["b_alpha"], p["gla_norm_g"], s0, bsz, seq)
    x1, hm, sel, g4, cnt = _post_mix(attn.reshape(n, -1), og, hin, x2d, p["w_o_mla"], p["w_o_gla"], p["w_out"],
                                     p["norm2_g"], p["w_router"], p["b_router"])
    new_state = (lat_new.reshape(bsz, seq, KV_LORA), krp_new[:, :MLA_ROPE].reshape(bsz, seq, MLA_ROPE), s_new)
    return x1, hm, sel, g4, cnt, new_state


def kernel(x_prompt, x_sample, cache_mla_latent, cache_mla_krope, state_gla, norm1_g, w_in, q_norm_g, kv_norm_g,
           w_uq, w_ukv, w_o_mla, w_alpha2, b_alpha, gla_norm_g, w_o_gla, w_out, norm2_g, w_router, b_router,
           w1, b1, w2, b2, normf_g):
    depth = w_in.shape[0]
    assert depth == 1
    l = 0
    bp, sp, _ = x_prompt.shape
    bs, ss, _ = x_sample.shape
    past_len = cache_mla_latent.shape[2]
    p = {
        "norm1_g": norm1_g[l][None, :],
        "w_all": _regroup_w_in(w_in[l]),
        "q_norm_g": q_norm_g[l][None, :],
        "kv_norm_g": kv_norm_g[l][None, :],
        "w_uq": _regroup_w_uq(w_uq[l]),
        "w_ukv": _regroup_w_ukv(w_ukv[l]),
        "w_uk": _per_head_w_ukv(w_ukv[l])[0],
        "w_uv": _per_head_w_ukv(w_ukv[l])[1],
        "w_o_mla": w_o_mla[l].astype(BF16),
        "w_alpha": jnp.pad(w_alpha2[l], ((0, LANES - GLA_GATE_RANK), (0, 0))).astype(BF16),
        "b_alpha": b_alpha[l][None, :],
        "gla_norm_g": gla_norm_g[l][None, :],
        "w_o_gla": w_o_gla[l].astype(BF16),
        "w_out": w_out[l].astype(BF16),
        "norm2_g": norm2_g[l][None, :],
        "w_router": jnp.pad(w_router[l], ((0, 0), (0, LANES - N_EXPERTS))).astype(BF16),
        "b_router": jnp.pad(b_router[l], (0, LANES - N_EXPERTS))[None, :],
    }
    zeros_state = jnp.zeros((bp, GLA_HEADS, GLA_DK_HEAD, GLA_DV_HEAD), F32)
    x1p, hmp, selp, g4p, cntp, (lat_p, kr_p, st_p) = _mixers(x_prompt, None, None, zeros_state, 0, p)
    x1s, hms, sels, g4s, cnts, (lat_s, kr_s, st_s) = _mixers(x_sample, cache_mla_latent[l], cache_mla_krope[l],
                                                              state_gla[l], past_len, p)

    n_p, n_s = bp * sp, bs * ss
    n_asg = (n_p + n_s) * TOP_K
    n_slot = -(-n_asg // MOE_ROWS) * MOE_ROWS + N_EXPERTS * MOE_ROWS
    ps_row, zero_first, zero_last, blk_exp, n_used = _slot_plan(cntp[0, :N_EXPERTS], cnts[0, :N_EXPERTS],
                                                                n_slot // MOE_ROWS)
    dest_p = _dest(selp, ps_row).reshape(n_p * TOP_K)
    dest_s = _dest(sels, ps_row + cntp).reshape(n_s * TOP_K)
    xs = _scatter_init(zero_first, zero_last, dest_p, hmp, n_slot)
    xs = _scatter_more(dest_s, hms, xs)
    w1g_t, w1l_t = _w1_split(w1[l])
    b1g = b1[l][:, None, 0::2]
    b1lin = b1[l][:, None, 1::2]
    ys_slots = _experts(blk_exp, n_used, xs, w1g_t, w1l_t, b1g, b1lin, w2[l], b2[l][:, None, :])
    yp = _combine(dest_p, x1p, g4p, normf_g[None, :], ys_slots).reshape(bp, sp, D_MODEL)
    ys = _combine(dest_s, x1s, g4s, normf_g[None, :], ys_slots).reshape(bs, ss, D_MODEL)
    return (yp, ys, lat_p[None], kr_p[None], st_p[None], lat_s[None], kr_s[None], st_s[None])
```

```python
import functools

import jax
import jax.numpy as jnp
import numpy as np
from jax import lax
from jax.experimental import pallas as pl
from jax.experimental.pallas import tpu as pltpu

F32 = jnp.float32
BF16 = jnp.bfloat16

D_MODEL = 1024
CHUNK = 64
CHUNK_SHIFT = 6
LOG2_E = 1.4426950408889634
NORM_EPS = 1e-6
MLA_HEADS = 8
MLA_NOPE = 128
MLA_ROPE = 64
MLA_V = 128
Q_LORA = 512
KV_LORA = 256
ROPE_THETA = 10000.0
MLA_SCALE = (MLA_NOPE + MLA_ROPE) ** -0.5
NEG_INF = -1e30
GLA_HEADS = 4
GLA_DK = D_MODEL // 2
GLA_DV = D_MODEL
GLA_DK_HEAD = GLA_DK // GLA_HEADS
GLA_DV_HEAD = GLA_DV // GLA_HEADS
GLA_GATE_RANK = 16
GLA_GATE_TEMP = 16.0
GLA_SUB = 16
FLASH_HEADS = 2
POST_MIX_PARTS = 2
N_EXPERTS = 32
TOP_K = 4
D_EXPERT = D_MODEL
SWIGLU_LIMIT = 7.0
SWIGLU_ALPHA = 1.702
MOE_ROWS = 512

LANES = 128
HEAD_PAD = 256

COL_CQ = 0
COL_CKV = 512
COL_KR = 768
COL_GA = 896
COL_GV = 1024
COL_GG = 2048
COL_GQ = 3072
COL_GK = 3584
COL_MG = 4096
D_IN_PAD = 6144

V7X_VMEM_BYTES = 64 * 1024 * 1024
VMEM_LIMIT = V7X_VMEM_BYTES - 8 * 1024 * 1024


def _cparams(*sem):
    return pltpu.CompilerParams(dimension_semantics=sem, vmem_limit_bytes=VMEM_LIMIT)


def _tile(n, pref, mult=16):
    t = min(pref, n)
    t -= t % mult
    while n % t:
        t -= mult
    return t


def _rms(x, g):
    return x * lax.rsqrt(jnp.mean(x * x, axis=-1, keepdims=True) + NORM_EPS) * g


def _in_proj_kernel(x_ref, g_ref, w_ref, o_ref):
    h = _rms(x_ref[...], g_ref[...]).astype(BF16)
    tn = 1536
    for j in range(D_IN_PAD // tn):
        o_ref[:, j * tn:(j + 1) * tn] = jnp.dot(h, w_ref[:, j * tn:(j + 1) * tn], preferred_element_type=F32)


def _in_proj(x2d, g, w_all):
    n = x2d.shape[0]
    tm = min(256, n)
    return pl.pallas_call(
        _in_proj_kernel,
        grid=(n // tm,),
        in_specs=[
            pl.BlockSpec((tm, D_MODEL), lambda i: (i, 0)),
            pl.BlockSpec((1, D_MODEL), lambda i: (0, 0)),
            pl.BlockSpec((D_MODEL, D_IN_PAD), lambda i: (0, 0)),
        ],
        out_specs=pl.BlockSpec((tm, D_IN_PAD), lambda i: (i, 0)),
        out_shape=jax.ShapeDtypeStruct((n, D_IN_PAD), F32),
        compiler_params=_cparams("parallel"),
        name="in_proj",
    )(x2d, g, w_all)


def _rope_pair(blk, cs):
    t = blk * cs
    return t + pltpu.roll(t, MLA_ROPE, axis=1)


def _mla_q_kernel(h_ref, cs_ref, qg_ref, kvg_ref, wuq_ref, q_ref, lat_ref, kr_ref):
    cs = cs_ref[...]
    qn = _rms(h_ref[:, COL_CQ:COL_CQ + Q_LORA], qg_ref[...]).astype(BF16)
    q = jnp.dot(qn, wuq_ref[...], preferred_element_type=F32)
    for hh in range(MLA_HEADS):
        c0 = hh * HEAD_PAD
        q_ref[:, c0:c0 + MLA_NOPE] = q[:, c0:c0 + MLA_NOPE].astype(BF16)
        q_ref[:, c0 + MLA_NOPE:c0 + HEAD_PAD] = _rope_pair(q[:, c0 + MLA_NOPE:c0 + HEAD_PAD], cs).astype(BF16)
    lat_ref[...] = _rms(h_ref[:, COL_CKV:COL_CKV + KV_LORA], kvg_ref[...])
    r = _rope_pair(h_ref[:, COL_KR:COL_KR + LANES], cs)
    lane = lax.broadcasted_iota(jnp.int32, r.shape, 1)
    kr_ref[...] = jnp.where(lane < MLA_ROPE, r, 0.0)


def _mla_q(hin, cs, qg, kvg, wuq):
    n = hin.shape[0]
    tm = min(512, n)
    n_cs = cs.shape[0] // tm
    return pl.pallas_call(
        _mla_q_kernel,
        grid=(n // tm,),
        in_specs=[
            pl.BlockSpec((tm, COL_GA), lambda i: (i, 0)),
            pl.BlockSpec((tm, LANES), lambda i: (i % n_cs, 0)),
            pl.BlockSpec((1, Q_LORA), lambda i: (0, 0)),
            pl.BlockSpec((1, KV_LORA), lambda i: (0, 0)),
            pl.BlockSpec((Q_LORA, MLA_HEADS * HEAD_PAD), lambda i: (0, 0)),
        ],
        out_specs=[
            pl.BlockSpec((tm, MLA_HEADS * HEAD_PAD), lambda i: (i, 0)),
            pl.BlockSpec((tm, KV_LORA), lambda i: (i, 0)),
            pl.BlockSpec((tm, LANES), lambda i: (i, 0)),
        ],
        out_shape=[
            jax.ShapeDtypeStruct((n, MLA_HEADS * HEAD_PAD), BF16),
            jax.ShapeDtypeStruct((n, KV_LORA), F32),
            jax.ShapeDtypeStruct((n, LANES), F32),
        ],
        compiler_params=_cparams("parallel"),
        name="mla_q",
    )(hin, cs, qg, kvg, wuq)


def _kv_up_kernel(lat_ref, kr_ref, w_ref, k_ref, v_ref):
    kv = jnp.dot(lat_ref[...].astype(BF16), w_ref[...], preferred_element_type=F32)
    krb = kr_ref[...].astype(BF16)
    for hh in range(MLA_HEADS):
        c0 = hh * HEAD_PAD
        k_ref[:, c0:c0 + MLA_NOPE] = kv[:, hh * MLA_NOPE:(hh + 1) * MLA_NOPE].astype(BF16)
        k_ref[:, c0 + MLA_NOPE:c0 + HEAD_PAD] = krb
    v_ref[...] = kv[:, MLA_HEADS * MLA_NOPE:].astype(BF16)


def _kv_up(lat, krp, wukv):
    n = lat.shape[0]
    tm = _tile(n, 512)
    return pl.pallas_call(
        _kv_up_kernel,
        grid=(n // tm,),
        in_specs=[
            pl.BlockSpec((tm, KV_LORA), lambda i: (i, 0)),
            pl.BlockSpec((tm, LANES), lambda i: (i, 0)),
            pl.BlockSpec((KV_LORA, MLA_HEADS * (MLA_NOPE + MLA_V)), lambda i: (0, 0)),
        ],
        out_specs=[
            pl.BlockSpec((tm, MLA_HEADS * HEAD_PAD), lambda i: (i, 0)),
            pl.BlockSpec((tm, MLA_HEADS * MLA_V), lambda i: (i, 0)),
        ],
        out_shape=[
            jax.ShapeDtypeStruct((n, MLA_HEADS * HEAD_PAD), BF16),
            jax.ShapeDtypeStruct((n, MLA_HEADS * MLA_V), BF16),
        ],
        compiler_params=_cparams("parallel"),
        name="kv_up",
    )(lat, krp, wukv)


def _flash_kernel(q_ref, k_ref, v_ref, o_ref, *, tq, tk, past_len):
    qi = pl.program_id(2)
    q_pos0 = past_len + qi * tq
    q_chunk = (q_pos0 + lax.broadcasted_iota(jnp.int32, (tq, 1), 0)) >> CHUNK_SHIFT
    n_blocks = (q_pos0 + tq + tk - 1) // tk
    n_full = ((q_pos0 >> CHUNK_SHIFT) << CHUNK_SHIFT) // tk
    qs = [q_ref[0, :, hh * HEAD_PAD:(hh + 1) * HEAD_PAD] for hh in range(FLASH_HEADS)]

    def step(j, carry, masked):
        k0 = pl.multiple_of(j * tk, tk)
        if masked:
            k_chunk = (k0 + lax.broadcasted_iota(jnp.int32, (1, tk), 1)) >> CHUNK_SHIFT
        out = []
        for hh in range(FLASH_HEADS):
            m, l, acc = carry[hh]
            k = k_ref[0, pl.ds(k0, tk), hh * HEAD_PAD:(hh + 1) * HEAD_PAD]
            v = v_ref[0, pl.ds(k0, tk), hh * MLA_V:(hh + 1) * MLA_V]
            s = lax.dot_general(qs[hh], k, (((1,), (1,)), ((), ())), preferred_element_type=F32)
            s = s * (MLA_SCALE * LOG2_E)
            if masked:
                s = jnp.where(k_chunk <= q_chunk, s, NEG_INF)
            m_new = jnp.maximum(m, jnp.max(s, axis=-1, keepdims=True))
            alpha = jnp.exp2(m - m_new)
            p = jnp.exp2(s - m_new)
            l = alpha * l + jnp.sum(p, axis=-1, keepdims=True)
            acc = alpha * acc + jnp.dot(p.astype(BF16), v, preferred_element_type=F32)
            out.append((m_new, l, acc))
        return tuple(out)

    init = tuple((jnp.full((tq, 1), NEG_INF, F32), jnp.zeros((tq, 1), F32), jnp.zeros((tq, MLA_V), F32))
                 for _ in range(FLASH_HEADS))
    carry = lax.fori_loop(0, n_full, functools.partial(step, masked=False), init)
    carry = lax.fori_loop(n_full, n_blocks, functools.partial(step, masked=True), carry)
    for hh in range(FLASH_HEADS):
        _, l, acc = carry[hh]
        o_ref[0, :, hh * MLA_V:(hh + 1) * MLA_V] = (acc / l).astype(BF16)


def _flash(q, k, v, past_len, tq, tk):
    b, sq, _ = q.shape
    sk = k.shape[1]
    fh = FLASH_HEADS
    return pl.pallas_call(
        functools.partial(_flash_kernel, tq=tq, tk=tk, past_len=past_len),
        grid=(b, MLA_HEADS // fh, sq // tq),
        in_specs=[
            pl.BlockSpec((1, tq, fh * HEAD_PAD), lambda bi, h, i: (bi, i, h)),
            pl.BlockSpec((1, sk, fh * HEAD_PAD), lambda bi, h, i: (bi, 0, h)),
            pl.BlockSpec((1, sk, fh * MLA_V), lambda bi, h, i: (bi, 0, h)),
        ],
        out_specs=pl.BlockSpec((1, tq, fh * MLA_V), lambda bi, h, i: (bi, i, h)),
        out_shape=jax.ShapeDtypeStruct((b, sq, MLA_HEADS * MLA_V), BF16),
        compiler_params=_cparams("parallel", "parallel", "arbitrary"),
        name="flash",
    )(q, k, v)


def _mla_cached_kernel(q_ref, clat_ref, ckr_ref, nlat_ref, nkr_ref, wuk_ref, wuv_ref, o_ref, *, past_len):
    seq = q_ref.shape[0]
    nt = (((1,), (1,)), ((), ()))
    q_abs = jnp.concatenate(
        [jnp.dot(q_ref[:, hh * HEAD_PAD:hh * HEAD_PAD + MLA_NOPE], wuk_ref[hh], preferred_element_type=F32)
         for hh in range(MLA_HEADS)], axis=0).astype(BF16)
    q_rot = jnp.concatenate([q_ref[:, hh * HEAD_PAD + MLA_NOPE:(hh + 1) * HEAD_PAD] for hh in range(MLA_HEADS)],
                            axis=0)
    pos_q = past_len + lax.broadcasted_iota(jnp.int32, (seq, 1), 0)
    q_chunk = jnp.concatenate([pos_q] * MLA_HEADS, axis=0) >> CHUNK_SHIFT

    def scores(lat, kr, k_pos0):
        n_k = lat.shape[0]
        kr_pad = jnp.concatenate([kr, jnp.zeros_like(kr)], axis=1)
        s = (lax.dot_general(q_abs, lat, nt, preferred_element_type=F32)
             + lax.dot_general(q_rot, kr_pad, nt, preferred_element_type=F32)) * (MLA_SCALE * LOG2_E)
        k_chunk = (k_pos0 + lax.broadcasted_iota(jnp.int32, (1, n_k), 1)) >> CHUNK_SHIFT
        return jnp.where(k_chunk <= q_chunk, s, NEG_INF)

    lat_c = clat_ref[0].astype(BF16)
    lat_n = nlat_ref[...].astype(BF16)
    s_c = scores(lat_c, ckr_ref[0].astype(BF16), 0)
    s_n = scores(lat_n, nkr_ref[:, :MLA_ROPE].astype(BF16), past_len)
    m = jnp.maximum(jnp.max(s_c, axis=-1, keepdims=True), jnp.max(s_n, axis=-1, keepdims=True))
    p_c = jnp.exp2(s_c - m)
    p_n = jnp.exp2(s_n - m)
    l = jnp.sum(p_c, axis=-1, keepdims=True) + jnp.sum(p_n, axis=-1, keepdims=True)
    o_lat = (jnp.dot(p_c.astype(BF16), lat_c, preferred_element_type=F32)
             + jnp.dot(p_n.astype(BF16), lat_n, preferred_element_type=F32)) / l
    o_lat = o_lat.astype(BF16)
    for hh in range(MLA_HEADS):
        o_ref[:, hh * MLA_V:(hh + 1) * MLA_V] = jnp.dot(o_lat[hh * seq:(hh + 1) * seq], wuv_ref[hh],
                                                         preferred_element_type=F32).astype(BF16)


def _mla_cached(q, cache_lat, cache_kr, lat_new, krp_new, w_uk, w_uv, bsz, seq):
    past_len = cache_lat.shape[1]
    return pl.pallas_call(
        functools.partial(_mla_cached_kernel, past_len=past_len),
        grid=(bsz,),
        in_specs=[
            pl.BlockSpec((seq, MLA_HEADS * HEAD_PAD), lambda b: (b, 0)),
            pl.BlockSpec((1, past_len, KV_LORA), lambda b: (b, 0, 0)),
            pl.BlockSpec((1, past_len, MLA_ROPE), lambda b: (b, 0, 0)),
            pl.BlockSpec((seq, KV_LORA), lambda b: (b, 0)),
            pl.BlockSpec((seq, LANES), lambda b: (b, 0)),
            pl.BlockSpec((MLA_HEADS, MLA_NOPE, KV_LORA), lambda b: (0, 0, 0)),
            pl.BlockSpec((MLA_HEADS, KV_LORA, MLA_V), lambda b: (0, 0, 0)),
        ],
        out_specs=pl.BlockSpec((seq, MLA_HEADS * MLA_V), lambda b: (b, 0)),
        out_shape=jax.ShapeDtypeStruct((bsz * seq, MLA_HEADS * MLA_V), BF16),
        compiler_params=_cparams("parallel"),
        name="mla_cached",
    )(q, cache_lat, cache_kr, lat_new, krp_new, w_uk, w_uv)


def _cumsum_rows(x):
    n = x.shape[0]
    row = lax.broadcasted_iota(jnp.int32, x.shape, 0)
    s = 1
    while s < n:
        x = x + jnp.where(row >= s, pltpu.roll(x, s, axis=0), 0.0)
        s *= 2
    return x


def _gla_kernel(gv_ref, gg_ref, gq_ref, gk_ref, ga_ref, wa_ref, ba_ref, ng_ref, s0_ref, og_ref, sn_ref, st_scr,
                *, csize, n_chunks):
    t = pl.program_id(1)

    @pl.when(t == 0)
    def _():
        for hh in range(GLA_HEADS):
            st_scr[hh] = s0_ref[0, hh].T

    wa = wa_ref[...]
    ba = ba_ref[...]
    ng = ng_ref[...]
    n_sub = csize // GLA_SUB
    col_id = lax.broadcasted_iota(jnp.int32, (GLA_SUB, csize), 1)
    row_in_sub = lax.broadcasted_iota(jnp.int32, (GLA_SUB, csize), 0)

    def chunk(c, carry):
        r0 = pl.multiple_of(c * csize, csize)
        rows = pl.ds(r0, csize)
        x = jnp.dot(ga_ref[rows, :].astype(BF16), wa, preferred_element_type=F32) + ba
        la = jax.nn.log_sigmoid(x) / GLA_GATE_TEMP
        b_all = _cumsum_rows(la) * LOG2_E
        for hh in range(GLA_HEADS):
            head(hh, rows, b_all[:, hh * GLA_DK_HEAD:(hh + 1) * GLA_DK_HEAD])
        return carry

    def head(hh, rows, b):
        kcols = slice(hh * GLA_DK_HEAD, (hh + 1) * GLA_DK_HEAD)
        vcols = slice(hh * GLA_DV_HEAD, (hh + 1) * GLA_DV_HEAD)
        q = gq_ref[rows, kcols] * (GLA_DK_HEAD ** -0.5)
        k = gk_ref[rows, kcols]
        v = gv_ref[rows, vcols]

        a_rows = []
        for i in range(n_sub):
            lo = i * GLA_SUB
            bi = b[lo:lo + GLA_SUB]
            qi = q[lo:lo + GLA_SUB]
            ki = k[lo:lo + GLA_SUB]
            if i == 0:
                a_i = jnp.zeros((GLA_SUB, csize), F32)
            else:
                ri = b[lo - 1:lo]
                qs = qi * jnp.exp2(bi - ri)
                ks = k * jnp.exp2(jnp.minimum(ri - b, 0.0))
                a_i = lax.dot_general(qs.astype(BF16), ks.astype(BF16), (((1,), (1,)), ((), ())),
                                      preferred_element_type=F32)
            for s in range(GLA_SUB):
                col = jnp.sum(qi * ki[s:s + 1] * jnp.exp2(bi - bi[s:s + 1]), axis=-1, keepdims=True)
                a_i = jnp.where(col_id == lo + s, col, a_i)
            a_rows.append(jnp.where(col_id <= lo + row_in_sub, a_i, 0.0))
        a = jnp.concatenate(a_rows, axis=0)

        st = st_scr[hh]
        vb = v.astype(BF16)
        o = jnp.dot(a.astype(BF16), vb, preferred_element_type=F32)
        o = o + lax.dot_general((q * jnp.exp2(b)).astype(BF16), st.astype(BF16), (((1,), (1,)), ((), ())),
                                preferred_element_type=F32)
        b_end = b[csize - 1:csize]
        kd = (k * jnp.exp2(b_end - b)).astype(BF16)
        st_scr[hh] = jnp.exp2(b_end) * st + jnp.dot(v.T.astype(BF16), kd, preferred_element_type=F32)
        on = _rms(o, ng)
        og_ref[rows, vcols] = (on * jax.nn.silu(gg_ref[rows, vcols])).astype(BF16)

    lax.fori_loop(0, n_chunks, chunk, 0)

    @pl.when(t == pl.num_programs(1) - 1)
    def _():
        for hh in range(GLA_HEADS):
            sn_ref[0, hh] = st_scr[hh].T


def _gla(hin, wa, ba, ng, s0, bsz, seq):
    csize = min(CHUNK, seq)
    tt = min(512, seq)
    n_t = seq // tt
    n = bsz * seq
    kern = functools.partial(_gla_kernel, csize=csize, n_chunks=tt // csize)
    state = pl.BlockSpec((1, GLA_HEADS, GLA_DK_HEAD, GLA_DV_HEAD), lambda b, t: (b, 0, 0, 0))
    return pl.pallas_call(
        kern,
        grid=(bsz, n_t),
        in_specs=[
            pl.BlockSpec((tt, GLA_DV), lambda b, t: (b * n_t + t, COL_GV // GLA_DV)),
            pl.BlockSpec((tt, GLA_DV), lambda b, t: (b * n_t + t, COL_GG // GLA_DV)),
            pl.BlockSpec((tt, GLA_DK), lambda b, t: (b * n_t + t, COL_GQ // GLA_DK)),
            pl.BlockSpec((tt, GLA_DK), lambda b, t: (b * n_t + t, COL_GK // GLA_DK)),
            pl.BlockSpec((tt, LANES), lambda b, t: (b * n_t + t, COL_GA // LANES)),
            pl.BlockSpec((LANES, GLA_DK), lambda b, t: (0, 0)),
            pl.BlockSpec((1, GLA_DK), lambda b, t: (0, 0)),
            pl.BlockSpec((1, GLA_DV_HEAD), lambda b, t: (0, 0)),
            state,
        ],
        out_specs=[pl.BlockSpec((tt, GLA_DV), lambda b, t: (b * n_t + t, 0)), state],
        out_shape=[
            jax.ShapeDtypeStruct((n, GLA_DV), BF16),
            jax.ShapeDtypeStruct((bsz, GLA_HEADS, GLA_DK_HEAD, GLA_DV_HEAD), F32),
        ],
        scratch_shapes=[pltpu.VMEM((GLA_HEADS, GLA_DV_HEAD, GLA_DK_HEAD), F32)],
        compiler_params=_cparams("parallel", "arbitrary"),
        name="gla",
    )(hin, hin, hin, hin, hin, wa, ba, ng, s0)


def _post_mix_kernel(at_ref, og_ref, mg1_ref, mg2_ref, x_ref, wom_ref, wog_ref, wout_ref, n2_ref, wr_ref, br_ref,
                     x1_ref, hm_ref, sel_ref, g4_ref, cnt_ref):
    @pl.when(pl.program_id(0) == 0)
    def _():
        cnt_ref[...] = jnp.zeros_like(cnt_ref)

    tm = x_ref.shape[0]
    half = tm // POST_MIX_PARTS
    for part in range(POST_MIX_PARTS):
        rows = slice(part * half, (part + 1) * half)
        y_mla = jnp.dot(at_ref[rows, :], wom_ref[...], preferred_element_type=F32)
        y_gla = jnp.dot(og_ref[rows, :], wog_ref[...], preferred_element_type=F32)
        m = jax.nn.sigmoid(mg1_ref[rows, :]) * y_mla + jax.nn.sigmoid(mg2_ref[rows, :]) * y_gla
        x1 = x_ref[rows, :] + jnp.dot(m.astype(BF16), wout_ref[...], preferred_element_type=F32)
        x1_ref[rows, :] = x1
        hm = _rms(x1, n2_ref[...])
        hm_ref[rows, :] = hm
        logits = jnp.dot(hm.astype(BF16), wr_ref[...], preferred_element_type=F32) + br_ref[...]

        lane = lax.broadcasted_iota(jnp.int32, logits.shape, 1)
        work = jnp.where(lane < N_EXPERTS, logits, -jnp.inf)
        sel = jnp.zeros(logits.shape, F32)
        vals = []
        for k in range(TOP_K):
            mk = jnp.max(work, axis=-1, keepdims=True)
            ik = jnp.min(jnp.where(work == mk, lane, LANES), axis=-1, keepdims=True)
            hit = lane == ik
            sel = jnp.where(hit, float(k + 1), sel)
            work = jnp.where(hit, -jnp.inf, work)
            vals.append(mk)
        sel_ref[rows, :] = sel
        ex = [jnp.exp(v - vals[0]) for v in vals]
        den = ex[0]
        for e in ex[1:]:
            den = den + e
        g4 = jnp.zeros(logits.shape, F32)
        for k in range(TOP_K):
            g4 = jnp.where(lane == k, ex[k] / den, g4)
        g4_ref[rows, :] = g4[:, :TOP_K]
        cnt_ref[...] += jnp.sum(jnp.where(sel > 0.0, 1.0, 0.0), axis=0, keepdims=True)


def _post_mix(attn, og, hin, x2d, wom, wog, wout, n2, wr, br):
    n = x2d.shape[0]
    tm = min(256 * POST_MIX_PARTS, n)
    row = lambda i: (i, 0)
    const = lambda i: (0, 0)
    return pl.pallas_call(
        _post_mix_kernel,
        grid=(n // tm,),
        in_specs=[
            pl.BlockSpec((tm, D_MODEL), row),
            pl.BlockSpec((tm, D_MODEL), row),
            pl.BlockSpec((tm, D_MODEL), lambda i: (i, COL_MG // D_MODEL)),
            pl.BlockSpec((tm, D_MODEL), lambda i: (i, COL_MG // D_MODEL + 1)),
            pl.BlockSpec((tm, D_MODEL), row),
            pl.BlockSpec((D_MODEL, D_MODEL), const),
            pl.BlockSpec((D_MODEL, D_MODEL), const),
            pl.BlockSpec((D_MODEL, D_MODEL), const),
            pl.BlockSpec((1, D_MODEL), const),
            pl.BlockSpec((D_MODEL, LANES), const),
            pl.BlockSpec((1, LANES), const),
        ],
        out_specs=[
            pl.BlockSpec((tm, D_MODEL), row),
            pl.BlockSpec((tm, D_MODEL), row),
            pl.BlockSpec((tm, LANES), row),
            pl.BlockSpec((tm, TOP_K), row),
            pl.BlockSpec((1, LANES), const),
        ],
        out_shape=[
            jax.ShapeDtypeStruct((n, D_MODEL), F32),
            jax.ShapeDtypeStruct((n, D_MODEL), F32),
            jax.ShapeDtypeStruct((n, LANES), F32),
            jax.ShapeDtypeStruct((n, TOP_K), F32),
            jax.ShapeDtypeStruct((1, LANES), F32),
        ],
        compiler_params=_cparams("arbitrary"),
        name="post_mix",
    )(attn, og, hin, hin, x2d, wom, wog, wout, n2, wr, br)


def _dest_kernel(sel_ref, ps_ref, dest_ref, run_scr):
    @pl.when(pl.program_id(0) == 0)
    def _():
        run_scr[...] = ps_ref[...]

    sel = sel_ref[...]
    tt = sel.shape[0]
    oh = jnp.where(sel > 0.0, 1.0, 0.0)
    r = lax.broadcasted_iota(jnp.int32, (tt, tt), 0)
    c = lax.broadcasted_iota(jnp.int32, (tt, tt), 1)
    ltri = jnp.where(r > c, 1.0, 0.0).astype(BF16)
    slot = jnp.dot(ltri, oh.astype(BF16), preferred_element_type=F32) + run_scr[...]
    run_scr[...] += jnp.sum(oh, axis=0, keepdims=True)
    lane = lax.broadcasted_iota(jnp.int32, sel.shape, 1)
    d = jnp.zeros(sel.shape, F32)
    for k in range(TOP_K):
        col = jnp.sum(jnp.where(sel == float(k + 1), slot, 0.0), axis=-1, keepdims=True)
        d = jnp.where(lane == k, col, d)
    dest_ref[...] = d[:, :TOP_K].astype(jnp.int32)


def _dest(sel, pad_start):
    n = sel.shape[0]
    tt = _tile(n, 512)
    return pl.pallas_call(
        _dest_kernel,
        grid=(n // tt,),
        in_specs=[pl.BlockSpec((tt, LANES), lambda i: (i, 0)), pl.BlockSpec((1, LANES), lambda i: (0, 0))],
        out_specs=pl.BlockSpec((tt, TOP_K), lambda i: (i, 0)),
        out_shape=jax.ShapeDtypeStruct((n, TOP_K), jnp.int32),
        scratch_shapes=[pltpu.VMEM((1, LANES), F32)],
        compiler_params=_cparams("arbitrary"),
        name="route_dest",
    )(sel, pad_start)


def _row_copy(src, src_row, dst, dst_row, sem):
    return pltpu.make_async_copy(src.at[pl.ds(src_row, 1)], dst.at[pl.ds(dst_row, 1)], sem)


def _scatter_rows(dest_ref, hm_ref, xs_ref, sem):
    tt = hm_ref.shape[0]

    def issue(t, carry):
        for k in range(TOP_K):
            _row_copy(hm_ref, t, xs_ref, dest_ref[t * TOP_K + k], sem).start()
        return carry

    lax.fori_loop(0, tt, issue, 0)
    for k in range(TOP_K):
        pltpu.make_async_copy(hm_ref, xs_ref.at[pl.ds(0, tt)], sem).wait()


def _scatter_init_kernel(zf_ref, zl_ref, dest_ref, hm_hbm, xs_ref, hbuf, zero_scr, lsems, ssems, zsem, *, n, tt):
    i = pl.program_id(0)

    def load(b, slot):
        return pltpu.make_async_copy(hm_hbm.at[pl.ds(pl.multiple_of(b * tt, tt), tt)], hbuf.at[slot], lsems.at[slot])

    def wait_rows(slot):
        for _ in range(TOP_K):
            pltpu.make_async_copy(hbuf.at[slot], xs_ref.at[pl.ds(0, tt)], ssems.at[slot]).wait()

    @pl.when(i == 0)
    def _():
        load(0, 0).start()
        zero_scr[...] = jnp.zeros_like(zero_scr)

        def zero_block(blk):
            return pltpu.make_async_copy(zero_scr,
                                         xs_ref.at[pl.ds(pl.multiple_of(blk * MOE_ROWS, MOE_ROWS), MOE_ROWS)], zsem)

        def start_block(blk, carry):
            zero_block(blk).start()
            return carry

        def wait_block(blk, carry):
            zero_block(blk).wait()
            return carry

        def start_range(e, carry):
            return lax.fori_loop(zf_ref[e], zl_ref[e], start_block, carry)

        def wait_range(e, carry):
            return lax.fori_loop(zf_ref[e], zl_ref[e], wait_block, carry)

        lax.fori_loop(0, N_EXPERTS + 1, start_range, 0)
        lax.fori_loop(0, N_EXPERTS + 1, wait_range, 0)

    for slot in range(3):
        @pl.when(i % 3 == slot)
        def _():
            @pl.when(i + 1 < n)
            def _():
                load(i + 1, (slot + 1) % 3).start()

            load(i, slot).wait()

            def issue(t, carry):
                for k in range(TOP_K):
                    _row_copy(hbuf.at[slot], t, xs_ref, dest_ref[t * TOP_K + k], ssems.at[slot]).start(priority=k % 2)
                return carry

            lax.fori_loop(0, tt, issue, 0)

            @pl.when(i >= 1)
            def _():
                wait_rows((slot + 2) % 3)

            @pl.when(i == n - 1)
            def _():
                wait_rows(slot)


def _scatter_more_kernel(dest_ref, hm_ref, xs_in, xs_ref, sem):
    del xs_in
    _scatter_rows(dest_ref, hm_ref, xs_ref, sem)


def _scatter_init(zero_first, zero_last, dest_flat, hm, n_slot):
    n = hm.shape[0]
    tt = _tile(n, 512)
    n_t = n // tt
    grid_spec = pltpu.PrefetchScalarGridSpec(
        num_scalar_prefetch=2,
        grid=(n_t,),
        in_specs=[
            pl.BlockSpec((tt * TOP_K,), lambda i, zf, zl: (i,), memory_space=pltpu.SMEM),
            pl.BlockSpec(memory_space=pl.ANY),
        ],
        out_specs=pl.BlockSpec(memory_space=pl.ANY),
        scratch_shapes=[pltpu.VMEM((3, tt, D_MODEL), F32), pltpu.VMEM((MOE_ROWS, D_MODEL), F32),
                        pltpu.SemaphoreType.DMA((3,)), pltpu.SemaphoreType.DMA((3,)), pltpu.SemaphoreType.DMA(())],
    )
    return pl.pallas_call(
        functools.partial(_scatter_init_kernel, n=n_t, tt=tt),
        grid_spec=grid_spec,
        out_shape=jax.ShapeDtypeStruct((n_slot, D_MODEL), F32),
        compiler_params=_cparams("arbitrary"),
        name="moe_scatter_init",
    )(zero_first, zero_last, dest_flat, hm)


def _scatter_more(dest_flat, hm, xs):
    n = hm.shape[0]
    tt = _tile(n, 256)
    return pl.pallas_call(
        _scatter_more_kernel,
        grid=(n // tt,),
        in_specs=[
            pl.BlockSpec((tt * TOP_K,), lambda i: (i,), memory_space=pltpu.SMEM),
            pl.BlockSpec((tt, D_MODEL), lambda i: (i, 0)),
            pl.BlockSpec(memory_space=pl.ANY),
        ],
        out_specs=pl.BlockSpec(memory_space=pl.ANY),
        out_shape=jax.ShapeDtypeStruct(xs.shape, xs.dtype),
        scratch_shapes=[pltpu.SemaphoreType.DMA(())],
        input_output_aliases={2: 0},
        compiler_params=_cparams("arbitrary"),
        name="moe_scatter",
    )(dest_flat, hm, xs)


def _w1_split_kernel(w_ref, g_ref, l_ref, t_scr):
    n_slab = w_ref.shape[1] // LANES
    for c in range(n_slab):
        t_scr[c] = w_ref[0, c * LANES:(c + 1) * LANES, :].T
    for c in range(n_slab):
        g_ref[0, :, c * LANES:(c + 1) * LANES] = t_scr[c, pl.ds(0, D_EXPERT, stride=2), :].astype(BF16)
        l_ref[0, :, c * LANES:(c + 1) * LANES] = t_scr[c, pl.ds(1, D_EXPERT, stride=2), :].astype(BF16)


def _w1_split(w1):
    ks = 512
    n_slab = ks // LANES
    out = jax.ShapeDtypeStruct((N_EXPERTS, D_EXPERT, D_MODEL), BF16)
    return pl.pallas_call(
        _w1_split_kernel,
        grid=(N_EXPERTS, D_MODEL // ks),
        in_specs=[pl.BlockSpec((1, ks, 2 * D_EXPERT), lambda e, j: (e, j, 0))],
        out_specs=[pl.BlockSpec((1, D_EXPERT, ks), lambda e, j: (e, 0, j)),
                   pl.BlockSpec((1, D_EXPERT, ks), lambda e, j: (e, 0, j))],
        out_shape=[out, out],
        scratch_shapes=[pltpu.VMEM((n_slab, 2 * D_EXPERT, LANES), F32)],
        compiler_params=_cparams("parallel", "parallel"),
        name="w1_split",
    )(w1)


def _expert_kernel(be_ref, nb_ref, x_ref, w1g_ref, w1l_ref, b1g_ref, b1l_ref, w2_ref, b2_ref, o_ref, w2b_scr):
    i = pl.program_id(0)
    used = i < nb_ref[0]

    @pl.when(jnp.logical_not(used))
    def _():
        o_ref[...] = jnp.zeros_like(o_ref)

    @pl.when(jnp.logical_or(i == 0, be_ref[i] != be_ref[jnp.maximum(i - 1, 0)]))
    def _():
        w2b_scr[...] = w2_ref[0].astype(BF16)

    @pl.when(used)
    def _():
        x = x_ref[...].astype(BF16)
        nt = (((1,), (1,)), ((), ()))
        hg = lax.dot_general(x, w1g_ref[0], nt, preferred_element_type=F32) + b1g_ref[0]
        hl = lax.dot_general(x, w1l_ref[0], nt, preferred_element_type=F32) + b1l_ref[0]
        glu = jnp.minimum(hg, SWIGLU_LIMIT)
        lin = jnp.clip(hl, -SWIGLU_LIMIT, SWIGLU_LIMIT)
        act = glu * jax.nn.sigmoid(SWIGLU_ALPHA * glu) * (lin + 1.0)
        o_ref[...] = jnp.dot(act.astype(BF16), w2b_scr[...], preferred_element_type=F32) + b2_ref[0]


def _experts(blk_exp, n_used, xs, w1g_t, w1l_t, b1g, b1l, w2, b2):
    n_slot = xs.shape[0]
    n_blk = n_slot // MOE_ROWS
    row_in = lambda i, be, nb: (jnp.minimum(i, nb[0] - 1), 0)
    row = lambda i, be, nb: (i, 0)
    wsel = lambda i, be, nb: (be[i], 0, 0)
    grid_spec = pltpu.PrefetchScalarGridSpec(
        num_scalar_prefetch=2,
        grid=(n_blk,),
        in_specs=[
            pl.BlockSpec((MOE_ROWS, D_MODEL), row_in),
            pl.BlockSpec((1, D_EXPERT, D_MODEL), wsel),
            pl.BlockSpec((1, D_EXPERT, D_MODEL), wsel),
            pl.BlockSpec((1, 1, D_EXPERT), wsel),
            pl.BlockSpec((1, 1, D_EXPERT), wsel),
            pl.BlockSpec((1, D_EXPERT, D_MODEL), wsel),
            pl.BlockSpec((1, 1, D_MODEL), wsel),
        ],
        out_specs=pl.BlockSpec((MOE_ROWS, D_MODEL), row),
        scratch_shapes=[pltpu.VMEM((D_EXPERT, D_MODEL), BF16)],
    )
    return pl.pallas_call(
        _expert_kernel,
        grid_spec=grid_spec,
        out_shape=jax.ShapeDtypeStruct((n_slot, D_MODEL), F32),
        compiler_params=_cparams("arbitrary"),
        name="experts",
    )(blk_exp, n_used, xs, w1g_t, w1l_t, b1g, b1l, w2, b2)


def _combine_kernel(dest_ref, dnext_ref, x1_ref, g4_ref, nf_ref, ys_ref, o_ref, buf, sems, *, n):
    tt = x1_ref.shape[0]
    i = pl.program_id(0)

    def issue(d_ref, slot):
        def body(t, carry):
            for k in range(TOP_K):
                _row_copy(ys_ref, d_ref[t * TOP_K + k], buf.at[slot, k], t, sems.at[slot]).start(priority=k % 2)
            return carry

        lax.fori_loop(0, tt, body, 0, unroll=8)

    def finish(slot):
        for k in range(TOP_K):
            pltpu.make_async_copy(ys_ref.at[pl.ds(0, tt)], buf.at[slot, k], sems.at[slot]).wait()
        g4 = g4_ref[...]
        moe = g4[:, 0:1] * buf[slot, 0]
        for k in range(1, TOP_K):
            moe = moe + g4[:, k:k + 1] * buf[slot, k]
        o_ref[...] = _rms(x1_ref[...] + moe, nf_ref[...])

    @pl.when(i == 0)
    def _():
        issue(dest_ref, 0)

    for parity in range(2):
        @pl.when(i % 2 == parity)
        def _():
            @pl.when(i + 1 < n)
            def _():
                issue(dnext_ref, 1 - parity)

            finish(parity)


def _combine(dest_flat, x1, g4, nf, ys):
    n = x1.shape[0]
    tt = _tile(n, 256)
    n_t = n // tt
    row = lambda i: (i, 0)
    return pl.pallas_call(
        functools.partial(_combine_kernel, n=n_t),
        grid=(n_t,),
        in_specs=[
            pl.BlockSpec((tt * TOP_K,), lambda i: (i,), memory_space=pltpu.SMEM),
            pl.BlockSpec((tt * TOP_K,), lambda i: (jnp.minimum(i + 1, n_t - 1),), memory_space=pltpu.SMEM),
            pl.BlockSpec((tt, D_MODEL), row),
            pl.BlockSpec((tt, TOP_K), row),
            pl.BlockSpec((1, D_MODEL), lambda i: (0, 0)),
            pl.BlockSpec(memory_space=pl.ANY),
        ],
        out_specs=pl.BlockSpec((tt, D_MODEL), row),
        out_shape=jax.ShapeDtypeStruct((n, D_MODEL), F32),
        scratch_shapes=[pltpu.VMEM((2, TOP_K, tt, D_MODEL), F32), pltpu.SemaphoreType.DMA((2,))],
        compiler_params=_cparams("arbitrary"),
        name="moe_combine",
    )(dest_flat, dest_flat, x1, g4, nf, ys)


def _rope_table(pos):
    half = MLA_ROPE // 2
    inv_freq = ROPE_THETA ** (-jnp.arange(half, dtype=F32) / half)
    ang = pos.astype(F32)[:, None] * inv_freq[None, :]
    cos, sin = jnp.cos(ang), jnp.sin(ang)
    return jnp.concatenate([cos, cos, -sin, sin], axis=-1)


def _regroup_w_in(w_in):
    o = np.cumsum((Q_LORA, KV_LORA, MLA_ROPE, GLA_DK, GLA_DK, GLA_DV, GLA_GATE_RANK, GLA_DV, 2 * D_MODEL))
    c_q, c_kv, k_r = w_in[:, :o[0]], w_in[:, o[0]:o[1]], w_in[:, o[1]:o[2]]
    gq, gk, gv = w_in[:, o[2]:o[3]], w_in[:, o[3]:o[4]], w_in[:, o[4]:o[5]]
    ga, gg, mg = w_in[:, o[5]:o[6]], w_in[:, o[6]:o[7]], w_in[:, o[7]:o[8]]
    half = MLA_ROPE // 2
    k_r_sw = jnp.concatenate([k_r[:, half:], k_r[:, :half]], axis=1)
    ga_pad = jnp.zeros((D_MODEL, LANES - GLA_GATE_RANK), w_in.dtype)
    return jnp.concatenate([c_q, c_kv, k_r, k_r_sw, ga, ga_pad, gv, gg, gq, gk, mg], axis=1).astype(BF16)


def _regroup_w_uq(w_uq):
    w = w_uq.reshape(Q_LORA, MLA_HEADS, MLA_NOPE + MLA_ROPE)
    half = MLA_ROPE // 2
    nope, rope = w[..., :MLA_NOPE], w[..., MLA_NOPE:]
    rope_sw = jnp.concatenate([rope[..., half:], rope[..., :half]], axis=-1)
    return jnp.concatenate([nope, rope, rope_sw], axis=-1).reshape(Q_LORA, MLA_HEADS * HEAD_PAD).astype(BF16)


def _regroup_w_ukv(w_ukv):
    w = w_ukv.reshape(KV_LORA, MLA_HEADS, MLA_NOPE + MLA_V)
    k = w[..., :MLA_NOPE].reshape(KV_LORA, MLA_HEADS * MLA_NOPE)
    v = w[..., MLA_NOPE:].reshape(KV_LORA, MLA_HEADS * MLA_V)
    return jnp.concatenate([k, v], axis=1).astype(BF16)


def _per_head_w_ukv(w_ukv):
    w = w_ukv.reshape(KV_LORA, MLA_HEADS, MLA_NOPE + MLA_V)
    w_uk = jnp.transpose(w[..., :MLA_NOPE], (1, 2, 0)).astype(BF16)
    w_uv = jnp.transpose(w[..., MLA_NOPE:], (1, 0, 2)).astype(BF16)
    return w_uk, w_uv


def _slot_plan(counts_first, counts_rest, n_blk):
    counts_first = counts_first.astype(jnp.int32)
    counts = counts_first + counts_rest.astype(jnp.int32)
    padded = (counts + MOE_ROWS - 1) // MOE_ROWS * MOE_ROWS
    pad_end = jnp.cumsum(padded)
    pad_start = pad_end - padded
    n_used = (pad_end[-1] // MOE_ROWS).reshape(1)
    blk_start = jnp.arange(n_blk, dtype=jnp.int32) * MOE_ROWS
    blk_exp = jnp.sum(blk_start[:, None] >= pad_end[None, :], axis=1).astype(jnp.int32)
    last_exp = blk_exp[jnp.maximum(n_used[0] - 1, 0)]
    blk_exp = jnp.where(jnp.arange(n_blk) < n_used[0], blk_exp, last_exp)
    ps_row = jnp.pad(pad_start.astype(F32), (0, LANES - N_EXPERTS))[None, :]
    zero_first = jnp.concatenate([(pad_start + counts_first) // MOE_ROWS, n_used]).astype(jnp.int32)
    zero_last = jnp.concatenate([pad_end // MOE_ROWS, jnp.full((1,), n_blk)]).astype(jnp.int32)
    return ps_row, zero_first, zero_last, blk_exp, n_used


def _mixers(x, past_lat, past_kr, s0, past_len, p):
    bsz, seq, _ = x.shape
    n = bsz * seq
    x2d = x.reshape(n, D_MODEL)
    hin = _in_proj(x2d, p["norm1_g"], p["w_all"])
    tm = min(512, n)
    cs = _rope_table(past_len + jnp.arange(seq))
    if seq < tm:
        cs = jnp.tile(cs, (tm // seq, 1))
    q, lat_new, krp_new = _mla_q(hin, cs, p["q_norm_g"], p["kv_norm_g"], p["w_uq"])
    if past_len:
        attn = _mla_cached(q, past_lat, past_kr, lat_new, krp_new, p["w_uk"], p["w_uv"], bsz, seq)
    else:
        kcat, v = _kv_up(lat_new, krp_new, p["w_ukv"])
        tq = min(1024, seq)
        tk = 1024 if seq % 1024 == 0 else seq
        attn = _flash(q.reshape(bsz, seq, -1), kcat.reshape(bsz, seq, -1), v.reshape(bsz, seq, -1), 0, tq, tk)
    og, s_new = _gla(hin, p["w_alpha"], p["b_alpha"], p["gla_norm_g"], s0, bsz, seq)
    x1, hm, sel, g4, cnt = _post_mix(attn.reshape(n, -1), og, hin, x2d, p["w_o_mla"], p["w_o_gla"], p["w_out"],
                                     p["norm2_g"], p["w_router"], p["b_router"])
    new_state = (lat_new.reshape(bsz, seq, KV_LORA), krp_new[:, :MLA_ROPE].reshape(bsz, seq, MLA_ROPE), s_new)
    return x1, hm, sel, g4, cnt, new_state


def kernel(x_prompt, x_sample, cache_mla_latent, cache_mla_krope, state_gla, norm1_g, w_in, q_norm_g, kv_norm_g,
           w_uq, w_ukv, w_o_mla, w_alpha2, b_alpha, gla_norm_g, w_o_gla, w_out, norm2_g, w_router, b_router,
           w1, b1, w2, b2, normf_g):
    depth = w_in.shape[0]
    assert depth == 1
    l = 0
    bp, sp, _ = x_prompt.shape
    bs, ss, _ = x_sample.shape
    past_len = cache_mla_latent.shape[2]
    p = {
        "norm1_g": norm1_g[l][None, :],
        "w_all": _regroup_w_in(w_in[l]),
        "q_norm_g": q_norm_g[l][None, :],
        "kv_norm_g": kv_norm_g[l][None, :],
        "w_uq": _regroup_w_uq(w_uq[l]),
        "w_ukv": _regroup_w_ukv(w_ukv[l]),
        "w_uk": _per_head_w_ukv(w_ukv[l])[0],
        "w_uv": _per_head_w_ukv(w_ukv[l])[1],
        "w_o_mla": w_o_mla[l].astype(BF16),
        "w_alpha": jnp.pad(w_alpha2[l], ((0, LANES - GLA_GATE_RANK), (0, 0))).astype(BF16),
        "b_alpha": b_alpha[l][None, :],
        "gla_norm_g": gla_norm_g[l][None, :],
        "w_o_gla": w_o_gla[l].astype(BF16),
        "w_out": w_out[l].astype(BF16),
        "norm2_g": norm2_g[l][None, :],
        "w_router": jnp.pad(w_router[l], ((0, 0), (0, LANES - N_EXPERTS))).astype(BF16),
        "b_router": jnp.pad(b_router[l], (0, LANES - N_EXPERTS))[None, :],
    }
    zeros_state = jnp.zeros((bp, GLA_HEADS, GLA_DK_HEAD, GLA_DV_HEAD), F32)
    x1p, hmp, selp, g4p, cntp, (lat_p, kr_p, st_p) = _mixers(x_prompt, None, None, zeros_state, 0, p)
    x1s, hms, sels, g4s, cnts, (lat_s, kr_s, st_s) = _mixers(x_sample, cache_mla_latent[l], cache_mla_krope[l],
                                                              state_gla[l], past_len, p)

    n_p, n_s = bp * sp, bs * ss
    n_asg = (n_p + n_s) * TOP_K
    n_slot = -(-n_asg // MOE_ROWS) * MOE_ROWS + N_EXPERTS * MOE_ROWS
    ps_row, zero_first, zero_last, blk_exp, n_used = _slot_plan(cntp[0, :N_EXPERTS], cnts[0, :N_EXPERTS],
                                                                n_slot // MOE_ROWS)
    dest_p = _dest(selp, ps_row).reshape(n_p * TOP_K)
    dest_s = _dest(sels, ps_row + cntp).reshape(n_s * TOP_K)
    xs = _scatter_init(zero_first, zero_last, dest_p, hmp, n_slot)
    xs = _scatter_more(dest_s, hms, xs)
    w1g_t, w1l_t = _w1_split(w1[l])
    b1g = b1[l][:, None, 0::2]
    b1lin = b1[l][:, None, 1::2]
    ys_slots = _experts(blk_exp, n_used, xs, w1g_t, w1l_t, b1g, b1lin, w2[l], b2[l][:, None, :])
    yp = _combine(dest_p, x1p, g4p, normf_g[None, :], ys_slots).reshape(bp, sp, D_MODEL)
    ys = _combine(dest_s, x1s, g4s, normf_g[None, :], ys_slots).reshape(bs, ss, D_MODEL)
    return (yp, ys, lat_p[None], kr_p[None], st_p[None], lat_s[None], kr_s[None], st_s[None])
```

```python
import functools

import jax
import jax.numpy as jnp
import numpy as np
from jax import lax
from jax.experimental import pallas as pl
from jax.experimental.pallas import tpu as pltpu

F32 = jnp.float32
BF16 = jnp.bfloat16

D_MODEL = 1024
CHUNK = 64
CHUNK_SHIFT = 6
LOG2_E = 1.4426950408889634
NORM_EPS = 1e-6
MLA_HEADS = 8
MLA_NOPE = 128
MLA_ROPE = 64
MLA_V = 128
Q_LORA = 512
KV_LORA = 256
ROPE_THETA = 10000.0
MLA_SCALE = (MLA_NOPE + MLA_ROPE) ** -0.5
NEG_INF = -1e30
GLA_HEADS = 4
GLA_DK = D_MODEL // 2
GLA_DV = D_MODEL
GLA_DK_HEAD = GLA_DK // GLA_HEADS
GLA_DV_HEAD = GLA_DV // GLA_HEADS
GLA_GATE_RANK = 16
GLA_GATE_TEMP = 16.0
GLA_SUB = 16
FLASH_HEADS = 2
POST_MIX_PARTS = 2
N_EXPERTS = 32
TOP_K = 4
D_EXPERT = D_MODEL
SWIGLU_LIMIT = 7.0
SWIGLU_ALPHA = 1.702
MOE_ROWS = 512

LANES = 128
HEAD_PAD = 256

COL_CQ = 0
COL_CKV = 512
COL_KR = 768
COL_GA = 896
COL_GV = 1024
COL_GG = 2048
COL_GQ = 3072
COL_GK = 3584
COL_MG = 4096
D_IN_PAD = 6144

V7X_VMEM_BYTES = 64 * 1024 * 1024
VMEM_LIMIT = V7X_VMEM_BYTES - 8 * 1024 * 1024


def _cparams(*sem):
    return pltpu.CompilerParams(dimension_semantics=sem, vmem_limit_bytes=VMEM_LIMIT)


def _tile(n, pref, mult=16):
    t = min(pref, n)
    t -= t % mult
    while n % t:
        t -= mult
    return t


def _rms(x, g):
    return x * lax.rsqrt(jnp.mean(x * x, axis=-1, keepdims=True) + NORM_EPS) * g


def _in_proj_kernel(x_ref, g_ref, w_ref, o_ref):
    h = _rms(x_ref[...], g_ref[...]).astype(BF16)
    tn = 1536
    for j in range(D_IN_PAD // tn):
        o_ref[:, j * tn:(j + 1) * tn] = jnp.dot(h, w_ref[:, j * tn:(j + 1) * tn], preferred_element_type=F32)


def _in_proj(x2d, g, w_all):
    n = x2d.shape[0]
    tm = min(256, n)
    return pl.pallas_call(
        _in_proj_kernel,
        grid=(n // tm,),
        in_specs=[
            pl.BlockSpec((tm, D_MODEL), lambda i: (i, 0)),
            pl.BlockSpec((1, D_MODEL), lambda i: (0, 0)),
            pl.BlockSpec((D_MODEL, D_IN_PAD), lambda i: (0, 0)),
        ],
        out_specs=pl.BlockSpec((tm, D_IN_PAD), lambda i: (i, 0)),
        out_shape=jax.ShapeDtypeStruct((n, D_IN_PAD), F32),
        compiler_params=_cparams("parallel"),
        name="in_proj",
    )(x2d, g, w_all)


def _rope_pair(blk, cs):
    t = blk * cs
    return t + pltpu.roll(t, MLA_ROPE, axis=1)


def _mla_q_kernel(h_ref, cs_ref, qg_ref, kvg_ref, wuq_ref, q_ref, lat_ref, kr_ref):
    cs = cs_ref[...]
    qn = _rms(h_ref[:, COL_CQ:COL_CQ + Q_LORA], qg_ref[...]).astype(BF16)
    q = jnp.dot(qn, wuq_ref[...], preferred_element_type=F32)
    for hh in range(MLA_HEADS):
        c0 = hh * HEAD_PAD
        q_ref[:, c0:c0 + MLA_NOPE] = q[:, c0:c0 + MLA_NOPE].astype(BF16)
        q_ref[:, c0 + MLA_NOPE:c0 + HEAD_PAD] = _rope_pair(q[:, c0 + MLA_NOPE:c0 + HEAD_PAD], cs).astype(BF16)
    lat_ref[...] = _rms(h_ref[:, COL_CKV:COL_CKV + KV_LORA], kvg_ref[...])
    r = _rope_pair(h_ref[:, COL_KR:COL_KR + LANES], cs)
    lane = lax.broadcasted_iota(jnp.int32, r.shape, 1)
    kr_ref[...] = jnp.where(lane < MLA_ROPE, r, 0.0)


def _mla_q(hin, cs, qg, kvg, wuq):
    n = hin.shape[0]
    tm = min(512, n)
    n_cs = cs.shape[0] // tm
    return pl.pallas_call(
        _mla_q_kernel,
        grid=(n // tm,),
        in_specs=[
            pl.BlockSpec((tm, COL_GA), lambda i: (i, 0)),
            pl.BlockSpec((tm, LANES), lambda i: (i % n_cs, 0)),
            pl.BlockSpec((1, Q_LORA), lambda i: (0, 0)),
            pl.BlockSpec((1, KV_LORA), lambda i: (0, 0)),
            pl.BlockSpec((Q_LORA, MLA_HEADS * HEAD_PAD), lambda i: (0, 0)),
        ],
        out_specs=[
            pl.BlockSpec((tm, MLA_HEADS * HEAD_PAD), lambda i: (i, 0)),
            pl.BlockSpec((tm, KV_LORA), lambda i: (i, 0)),
            pl.BlockSpec((tm, LANES), lambda i: (i, 0)),
        ],
        out_shape=[
            jax.ShapeDtypeStruct((n, MLA_HEADS * HEAD_PAD), BF16),
            jax.ShapeDtypeStruct((n, KV_LORA), F32),
            jax.ShapeDtypeStruct((n, LANES), F32),
        ],
        compiler_params=_cparams("parallel"),
        name="mla_q",
    )(hin, cs, qg, kvg, wuq)


def _kv_up_kernel(lat_ref, kr_ref, w_ref, k_ref, v_ref):
    kv = jnp.dot(lat_ref[...].astype(BF16), w_ref[...], preferred_element_type=F32)
    krb = kr_ref[...].astype(BF16)
    for hh in range(MLA_HEADS):
        c0 = hh * HEAD_PAD
        k_ref[:, c0:c0 + MLA_NOPE] = kv[:, hh * MLA_NOPE:(hh + 1) * MLA_NOPE].astype(BF16)
        k_ref[:, c0 + MLA_NOPE:c0 + HEAD_PAD] = krb
    v_ref[...] = kv[:, MLA_HEADS * MLA_NOPE:].astype(BF16)


def _kv_up(lat, krp, wukv):
    n = lat.shape[0]
    tm = _tile(n, 512)
    return pl.pallas_call(
        _kv_up_kernel,
        grid=(n // tm,),
        in_specs=[
            pl.BlockSpec((tm, KV_LORA), lambda i: (i, 0)),
            pl.BlockSpec((tm, LANES), lambda i: (i, 0)),
            pl.BlockSpec((KV_LORA, MLA_HEADS * (MLA_NOPE + MLA_V)), lambda i: (0, 0)),
        ],
        out_specs=[
            pl.BlockSpec((tm, MLA_HEADS * HEAD_PAD), lambda i: (i, 0)),
            pl.BlockSpec((tm, MLA_HEADS * MLA_V), lambda i: (i, 0)),
        ],
        out_shape=[
            jax.ShapeDtypeStruct((n, MLA_HEADS * HEAD_PAD), BF16),
            jax.ShapeDtypeStruct((n, MLA_HEADS * MLA_V), BF16),
        ],
        compiler_params=_cparams("parallel"),
        name="kv_up",
    )(lat, krp, wukv)


def _flash_kernel(q_ref, k_ref, v_ref, o_ref, *, tq, tk, past_len):
    qi = pl.program_id(2)
    q_pos0 = past_len + qi * tq
    q_chunk = (q_pos0 + lax.broadcasted_iota(jnp.int32, (tq, 1), 0)) >> CHUNK_SHIFT
    n_blocks = (q_pos0 + tq + tk - 1) // tk
    n_full = ((q_pos0 >> CHUNK_SHIFT) << CHUNK_SHIFT) // tk
    qs = [q_ref[0, :, hh * HEAD_PAD:(hh + 1) * HEAD_PAD] for hh in range(FLASH_HEADS)]

    def step(j, carry, masked):
        k0 = pl.multiple_of(j * tk, tk)
        if masked:
            k_chunk = (k0 + lax.broadcasted_iota(jnp.int32, (1, tk), 1)) >> CHUNK_SHIFT
        out = []
        for hh in range(FLASH_HEADS):
            m, l, acc = carry[hh]
            k = k_ref[0, pl.ds(k0, tk), hh * HEAD_PAD:(hh + 1) * HEAD_PAD]
            v = v_ref[0, pl.ds(k0, tk), hh * MLA_V:(hh + 1) * MLA_V]
            s = lax.dot_general(qs[hh], k, (((1,), (1,)), ((), ())), preferred_element_type=F32)
            s = s * (MLA_SCALE * LOG2_E)
            if masked:
                s = jnp.where(k_chunk <= q_chunk, s, NEG_INF)
            m_new = jnp.maximum(m, jnp.max(s, axis=-1, keepdims=True))
            alpha = jnp.exp2(m - m_new)
            p = jnp.exp2(s - m_new)
            l = alpha * l + jnp.sum(p, axis=-1, keepdims=True)
            acc = alpha * acc + jnp.dot(p.astype(BF16), v, preferred_element_type=F32)
            out.append((m_new, l, acc))
        return tuple(out)

    init = tuple((jnp.full((tq, 1), NEG_INF, F32), jnp.zeros((tq, 1), F32), jnp.zeros((tq, MLA_V), F32))
                 for _ in range(FLASH_HEADS))
    carry = lax.fori_loop(0, n_full, functools.partial(step, masked=False), init)
    carry = lax.fori_loop(n_full, n_blocks, functools.partial(step, masked=True), carry)
    for hh in range(FLASH_HEADS):
        _, l, acc = carry[hh]
        o_ref[0, :, hh * MLA_V:(hh + 1) * MLA_V] = (acc / l).astype(BF16)


def _flash(q, k, v, past_len, tq, tk):
    b, sq, _ = q.shape
    sk = k.shape[1]
    fh = FLASH_HEADS
    return pl.pallas_call(
        functools.partial(_flash_kernel, tq=tq, tk=tk, past_len=past_len),
        grid=(b, MLA_HEADS // fh, sq // tq),
        in_specs=[
            pl.BlockSpec((1, tq, fh * HEAD_PAD), lambda bi, h, i: (bi, i, h)),
            pl.BlockSpec((1, sk, fh * HEAD_PAD), lambda bi, h, i: (bi, 0, h)),
            pl.BlockSpec((1, sk, fh * MLA_V), lambda bi, h, i: (bi, 0, h)),
        ],
        out_specs=pl.BlockSpec((1, tq, fh * MLA_V), lambda bi, h, i: (bi, i, h)),
        out_shape=jax.ShapeDtypeStruct((b, sq, MLA_HEADS * MLA_V), BF16),
        compiler_params=_cparams("parallel", "parallel", "arbitrary"),
        name="flash",
    )(q, k, v)


def _mla_cached_kernel(q_ref, clat_ref, ckr_ref, nlat_ref, nkr_ref, wuk_ref, wuv_ref, o_ref, *, past_len):
    seq = q_ref.shape[0]
    nt = (((1,), (1,)), ((), ()))
    q_abs = jnp.concatenate(
        [jnp.dot(q_ref[:, hh * HEAD_PAD:hh * HEAD_PAD + MLA_NOPE], wuk_ref[hh], preferred_element_type=F32)
         for hh in range(MLA_HEADS)], axis=0).astype(BF16)
    q_rot = jnp.concatenate([q_ref[:, hh * HEAD_PAD + MLA_NOPE:(hh + 1) * HEAD_PAD] for hh in range(MLA_HEADS)],
                            axis=0)
    pos_q = past_len + lax.broadcasted_iota(jnp.int32, (seq, 1), 0)
    q_chunk = jnp.concatenate([pos_q] * MLA_HEADS, axis=0) >> CHUNK_SHIFT

    def scores(lat, kr, k_pos0):
        n_k = lat.shape[0]
        kr_pad = jnp.concatenate([kr, jnp.zeros_like(kr)], axis=1)
        s = (lax.dot_general(q_abs, lat, nt, preferred_element_type=F32)
             + lax.dot_general(q_rot, kr_pad, nt, preferred_element_type=F32)) * (MLA_SCALE * LOG2_E)
        k_chunk = (k_pos0 + lax.broadcasted_iota(jnp.int32, (1, n_k), 1)) >> CHUNK_SHIFT
        return jnp.where(k_chunk <= q_chunk, s, NEG_INF)

    lat_c = clat_ref[0].astype(BF16)
    lat_n = nlat_ref[...].astype(BF16)
    s_c = scores(lat_c, ckr_ref[0].astype(BF16), 0)
    s_n = scores(lat_n, nkr_ref[:, :MLA_ROPE].astype(BF16), past_len)
    m = jnp.maximum(jnp.max(s_c, axis=-1, keepdims=True), jnp.max(s_n, axis=-1, keepdims=True))
    p_c = jnp.exp2(s_c - m)
    p_n = jnp.exp2(s_n - m)
    l = jnp.sum(p_c, axis=-1, keepdims=True) + jnp.sum(p_n, axis=-1, keepdims=True)
    o_lat = (jnp.dot(p_c.astype(BF16), lat_c, preferred_element_type=F32)
             + jnp.dot(p_n.astype(BF16), lat_n, preferred_element_type=F32)) / l
    o_lat = o_lat.astype(BF16)
    for hh in range(MLA_HEADS):
        o_ref[:, hh * MLA_V:(hh + 1) * MLA_V] = jnp.dot(o_lat[hh * seq:(hh + 1) * seq], wuv_ref[hh],
                                                         preferred_element_type=F32).astype(BF16)


def _mla_cached(q, cache_lat, cache_kr, lat_new, krp_new, w_uk, w_uv, bsz, seq):
    past_len = cache_lat.shape[1]
    return pl.pallas_call(
        functools.partial(_mla_cached_kernel, past_len=past_len),
        grid=(bsz,),
        in_specs=[
            pl.BlockSpec((seq, MLA_HEADS * HEAD_PAD), lambda b: (b, 0)),
            pl.BlockSpec((1, past_len, KV_LORA), lambda b: (b, 0, 0)),
            pl.BlockSpec((1, past_len, MLA_ROPE), lambda b: (b, 0, 0)),
            pl.BlockSpec((seq, KV_LORA), lambda b: (b, 0)),
            pl.BlockSpec((seq, LANES), lambda b: (b, 0)),
            pl.BlockSpec((MLA_HEADS, MLA_NOPE, KV_LORA), lambda b: (0, 0, 0)),
            pl.BlockSpec((MLA_HEADS, KV_LORA, MLA_V), lambda b: (0, 0, 0)),
        ],
        out_specs=pl.BlockSpec((seq, MLA_HEADS * MLA_V), lambda b: (b, 0)),
        out_shape=jax.ShapeDtypeStruct((bsz * seq, MLA_HEADS * MLA_V), BF16),
        compiler_params=_cparams("parallel"),
        name="mla_cached",
    )(q, cache_lat, cache_kr, lat_new, krp_new, w_uk, w_uv)


def _cumsum_rows(x):
    n = x.shape[0]
    row = lax.broadcasted_iota(jnp.int32, x.shape, 0)
    s = 1
    while s < n:
        x = x + jnp.where(row >= s, pltpu.roll(x, s, axis=0), 0.0)
        s *= 2
    return x


def _gla_kernel(gv_ref, gg_ref, gq_ref, gk_ref, ga_ref, wa_ref, ba_ref, ng_ref, s0_ref, og_ref, sn_ref, st_scr,
                *, csize, n_chunks):
    t = pl.program_id(1)

    @pl.when(t == 0)
    def _():
        for hh in range(GLA_HEADS):
            st_scr[hh] = s0_ref[0, hh].T

    wa = wa_ref[...]
    ba = ba_ref[...]
    ng = ng_ref[...]
    n_sub = csize // GLA_SUB
    col_id = lax.broadcasted_iota(jnp.int32, (GLA_SUB, csize), 1)
    row_in_sub = lax.broadcasted_iota(jnp.int32, (GLA_SUB, csize), 0)

    def chunk(c, carry):
        r0 = pl.multiple_of(c * csize, csize)
        rows = pl.ds(r0, csize)
        x = jnp.dot(ga_ref[rows, :].astype(BF16), wa, preferred_element_type=F32) + ba
        la = jax.nn.log_sigmoid(x) / GLA_GATE_TEMP
        b_all = _cumsum_rows(la) * LOG2_E
        for hh in range(GLA_HEADS):
            head(hh, rows, b_all[:, hh * GLA_DK_HEAD:(hh + 1) * GLA_DK_HEAD])
        return carry

    def head(hh, rows, b):
        kcols = slice(hh * GLA_DK_HEAD, (hh + 1) * GLA_DK_HEAD)
        vcols = slice(hh * GLA_DV_HEAD, (hh + 1) * GLA_DV_HEAD)
        q = gq_ref[rows, kcols] * (GLA_DK_HEAD ** -0.5)
        k = gk_ref[rows, kcols]
        v = gv_ref[rows, vcols]

        a_rows = []
        for i in range(n_sub):
            lo = i * GLA_SUB
            bi = b[lo:lo + GLA_SUB]
            qi = q[lo:lo + GLA_SUB]
            ki = k[lo:lo + GLA_SUB]
            if i == 0:
                a_i = jnp.zeros((GLA_SUB, csize), F32)
            else:
                ri = b[lo - 1:lo]
                qs = qi * jnp.exp2(bi - ri)
                ks = k * jnp.exp2(jnp.minimum(ri - b, 0.0))
                a_i = lax.dot_general(qs.astype(BF16), ks.astype(BF16), (((1,), (1,)), ((), ())),
                                      preferred_element_type=F32)
            for s in range(GLA_SUB):
                col = jnp.sum(qi * ki[s:s + 1] * jnp.exp2(bi - bi[s:s + 1]), axis=-1, keepdims=True)
                a_i = jnp.where(col_id == lo + s, col, a_i)
            a_rows.append(jnp.where(col_id <= lo + row_in_sub, a_i, 0.0))
        a = jnp.concatenate(a_rows, axis=0)

        st = st_scr[hh]
        vb = v.astype(BF16)
        o = jnp.dot(a.astype(BF16), vb, preferred_element_type=F32)
        o = o + lax.dot_general((q * jnp.exp2(b)).astype(BF16), st.astype(BF16), (((1,), (1,)), ((), ())),
                                preferred_element_type=F32)
        b_end = b[csize - 1:csize]
        kd = (k * jnp.exp2(b_end - b)).astype(BF16)
        st_scr[hh] = jnp.exp2(b_end) * st + jnp.dot(v.T.astype(BF16), kd, preferred_element_type=F32)
        on = _rms(o, ng)
        og_ref[rows, vcols] = (on * jax.nn.silu(gg_ref[rows, vcols])).astype(BF16)

    lax.fori_loop(0, n_chunks, chunk, 0)

    @pl.when(t == pl.num_programs(1) - 1)
    def _():
        for hh in range(GLA_HEADS):
            sn_ref[0, hh] = st_scr[hh].T


def _gla(hin, wa, ba, ng, s0, bsz, seq):
    csize = min(CHUNK, seq)
    tt = min(1024, seq)
    n_t = seq // tt
    n = bsz * seq
    kern = functools.partial(_gla_kernel, csize=csize, n_chunks=tt // csize)
    state = pl.BlockSpec((1, GLA_HEADS, GLA_DK_HEAD, GLA_DV_HEAD), lambda b, t: (b, 0, 0, 0))
    return pl.pallas_call(
        kern,
        grid=(bsz, n_t),
        in_specs=[
            pl.BlockSpec((tt, GLA_DV), lambda b, t: (b * n_t + t, COL_GV // GLA_DV)),
            pl.BlockSpec((tt, GLA_DV), lambda b, t: (b * n_t + t, COL_GG // GLA_DV)),
            pl.BlockSpec((tt, GLA_DK), lambda b, t: (b * n_t + t, COL_GQ // GLA_DK)),
            pl.BlockSpec((tt, GLA_DK), lambda b, t: (b * n_t + t, COL_GK // GLA_DK)),
            pl.BlockSpec((tt, LANES), lambda b, t: (b * n_t + t, COL_GA // LANES)),
            pl.BlockSpec((LANES, GLA_DK), lambda b, t: (0, 0)),
            pl.BlockSpec((1, GLA_DK), lambda b, t: (0, 0)),
            pl.BlockSpec((1, GLA_DV_HEAD), lambda b, t: (0, 0)),
            state,
        ],
        out_specs=[pl.BlockSpec((tt, GLA_DV), lambda b, t: (b * n_t + t, 0)), state],
        out_shape=[
            jax.ShapeDtypeStruct((n, GLA_DV), BF16),
            jax.ShapeDtypeStruct((bsz, GLA_HEADS, GLA_DK_HEAD, GLA_DV_HEAD), F32),
        ],
        scratch_shapes=[pltpu.VMEM((GLA_HEADS, GLA_DV_HEAD, GLA_DK_HEAD), F32)],
        compiler_params=_cparams("parallel", "arbitrary"),
        name="gla",
    )(hin, hin, hin, hin, hin, wa, ba, ng, s0)


def _post_mix_kernel(at_ref, og_ref, mg1_ref, mg2_ref, x_ref, wom_ref, wog_ref, wout_ref, n2_ref, wr_ref, br_ref,
                     x1_ref, hm_ref, sel_ref, g4_ref, cnt_ref):
    @pl.when(pl.program_id(0) == 0)
    def _():
        cnt_ref[...] = jnp.zeros_like(cnt_ref)

    tm = x_ref.shape[0]
    half = tm // POST_MIX_PARTS
    for part in range(POST_MIX_PARTS):
        rows = slice(part * half, (part + 1) * half)
        y_mla = jnp.dot(at_ref[rows, :], wom_ref[...], preferred_element_type=F32)
        y_gla = jnp.dot(og_ref[rows, :], wog_ref[...], preferred_element_type=F32)
        m = jax.nn.sigmoid(mg1_ref[rows, :]) * y_mla + jax.nn.sigmoid(mg2_ref[rows, :]) * y_gla
        x1 = x_ref[rows, :] + jnp.dot(m.astype(BF16), wout_ref[...], preferred_element_type=F32)
        x1_ref[rows, :] = x1
        hm = _rms(x1, n2_ref[...])
        hm_ref[rows, :] = hm
        logits = jnp.dot(hm.astype(BF16), wr_ref[...], preferred_element_type=F32) + br_ref[...]

        lane = lax.broadcasted_iota(jnp.int32, logits.shape, 1)
        work = jnp.where(lane < N_EXPERTS, logits, -jnp.inf)
        sel = jnp.zeros(logits.shape, F32)
        vals = []
        for k in range(TOP_K):
            mk = jnp.max(work, axis=-1, keepdims=True)
            ik = jnp.min(jnp.where(work == mk, lane, LANES), axis=-1, keepdims=True)
            hit = lane == ik
            sel = jnp.where(hit, float(k + 1), sel)
            work = jnp.where(hit, -jnp.inf, work)
            vals.append(mk)
        sel_ref[rows, :] = sel
        ex = [jnp.exp(v - vals[0]) for v in vals]
        den = ex[0]
        for e in ex[1:]:
            den = den + e
        g4 = jnp.zeros(logits.shape, F32)
        for k in range(TOP_K):
            g4 = jnp.where(lane == k, ex[k] / den, g4)
        g4_ref[rows, :] = g4[:, :TOP_K]
        cnt_ref[...] += jnp.sum(jnp.where(sel > 0.0, 1.0, 0.0), axis=0, keepdims=True)


def _post_mix(attn, og, hin, x2d, wom, wog, wout, n2, wr, br):
    n = x2d.shape[0]
    tm = min(256 * POST_MIX_PARTS, n)
    row = lambda i: (i, 0)
    const = lambda i: (0, 0)
    return pl.pallas_call(
        _post_mix_kernel,
        grid=(n // tm,),
        in_specs=[
            pl.BlockSpec((tm, D_MODEL), row),
            pl.BlockSpec((tm, D_MODEL), row),
            pl.BlockSpec((tm, D_MODEL), lambda i: (i, COL_MG // D_MODEL)),
            pl.BlockSpec((tm, D_MODEL), lambda i: (i, COL_MG // D_MODEL + 1)),
            pl.BlockSpec((tm, D_MODEL), row),
            pl.BlockSpec((D_MODEL, D_MODEL), const),
            pl.BlockSpec((D_MODEL, D_MODEL), const),
            pl.BlockSpec((D_MODEL, D_MODEL), const),
            pl.BlockSpec((1, D_MODEL), const),
            pl.BlockSpec((D_MODEL, LANES), const),
            pl.BlockSpec((1, LANES), const),
        ],
        out_specs=[
            pl.BlockSpec((tm, D_MODEL), row),
            pl.BlockSpec((tm, D_MODEL), row),
            pl.BlockSpec((tm, LANES), row),
            pl.BlockSpec((tm, TOP_K), row),
            pl.BlockSpec((1, LANES), const),
        ],
        out_shape=[
            jax.ShapeDtypeStruct((n, D_MODEL), F32),
            jax.ShapeDtypeStruct((n, D_MODEL), F32),
            jax.ShapeDtypeStruct((n, LANES), F32),
            jax.ShapeDtypeStruct((n, TOP_K), F32),
            jax.ShapeDtypeStruct((1, LANES), F32),
        ],
        compiler_params=_cparams("arbitrary"),
        name="post_mix",
    )(attn, og, hin, hin, x2d, wom, wog, wout, n2, wr, br)


def _dest_kernel(sel_ref, ps_ref, dest_ref, run_scr):
    @pl.when(pl.program_id(0) == 0)
    def _():
        run_scr[...] = ps_ref[...]

    sel = sel_ref[...]
    tt = sel.shape[0]
    oh = jnp.where(sel > 0.0, 1.0, 0.0)
    r = lax.broadcasted_iota(jnp.int32, (tt, tt), 0)
    c = lax.broadcasted_iota(jnp.int32, (tt, tt), 1)
    ltri = jnp.where(r > c, 1.0, 0.0).astype(BF16)
    slot = jnp.dot(ltri, oh.astype(BF16), preferred_element_type=F32) + run_scr[...]
    run_scr[...] += jnp.sum(oh, axis=0, keepdims=True)
    lane = lax.broadcasted_iota(jnp.int32, sel.shape, 1)
    d = jnp.zeros(sel.shape, F32)
    for k in range(TOP_K):
        col = jnp.sum(jnp.where(sel == float(k + 1), slot, 0.0), axis=-1, keepdims=True)
        d = jnp.where(lane == k, col, d)
    dest_ref[...] = d[:, :TOP_K].astype(jnp.int32)


def _dest(sel, pad_start):
    n = sel.shape[0]
    tt = _tile(n, 512)
    return pl.pallas_call(
        _dest_kernel,
        grid=(n // tt,),
        in_specs=[pl.BlockSpec((tt, LANES), lambda i: (i, 0)), pl.BlockSpec((1, LANES), lambda i: (0, 0))],
        out_specs=pl.BlockSpec((tt, TOP_K), lambda i: (i, 0)),
        out_shape=jax.ShapeDtypeStruct((n, TOP_K), jnp.int32),
        scratch_shapes=[pltpu.VMEM((1, LANES), F32)],
        compiler_params=_cparams("arbitrary"),
        name="route_dest",
    )(sel, pad_start)


def _row_copy(src, src_row, dst, dst_row, sem):
    return pltpu.make_async_copy(src.at[pl.ds(src_row, 1)], dst.at[pl.ds(dst_row, 1)], sem)


def _scatter_rows(dest_ref, hm_ref, xs_ref, sem):
    tt = hm_ref.shape[0]

    def issue(t, carry):
        for k in range(TOP_K):
            _row_copy(hm_ref, t, xs_ref, dest_ref[t * TOP_K + k], sem).start()
        return carry

    lax.fori_loop(0, tt, issue, 0)
    for k in range(TOP_K):
        pltpu.make_async_copy(hm_ref, xs_ref.at[pl.ds(0, tt)], sem).wait()


def _scatter_init_kernel(zf_ref, zl_ref, dest_ref, hm_hbm, xs_ref, hbuf, zero_scr, lsems, ssems, zsem, *, n, tt):
    i = pl.program_id(0)

    def load(b, slot):
        return pltpu.make_async_copy(hm_hbm.at[pl.ds(pl.multiple_of(b * tt, tt), tt)], hbuf.at[slot], lsems.at[slot])

    def wait_rows(slot):
        for _ in range(TOP_K):
            pltpu.make_async_copy(hbuf.at[slot], xs_ref.at[pl.ds(0, tt)], ssems.at[slot]).wait()

    @pl.when(i == 0)
    def _():
        load(0, 0).start()
        zero_scr[...] = jnp.zeros_like(zero_scr)

        def zero_block(blk):
            return pltpu.make_async_copy(zero_scr,
                                         xs_ref.at[pl.ds(pl.multiple_of(blk * MOE_ROWS, MOE_ROWS), MOE_ROWS)], zsem)

        def start_block(blk, carry):
            zero_block(blk).start()
            return carry

        def wait_block(blk, carry):
            zero_block(blk).wait()
            return carry

        def start_range(e, carry):
            return lax.fori_loop(zf_ref[e], zl_ref[e], start_block, carry)

        def wait_range(e, carry):
            return lax.fori_loop(zf_ref[e], zl_ref[e], wait_block, carry)

        lax.fori_loop(0, N_EXPERTS + 1, start_range, 0)
        lax.fori_loop(0, N_EXPERTS + 1, wait_range, 0)

    for slot in range(3):
        @pl.when(i % 3 == slot)
        def _():
            @pl.when(i + 1 < n)
            def _():
                load(i + 1, (slot + 1) % 3).start()

            load(i, slot).wait()

            def issue(t, carry):
                for k in range(TOP_K):
                    _row_copy(hbuf.at[slot], t, xs_ref, dest_ref[t * TOP_K + k], ssems.at[slot]).start(priority=k % 2)
                return carry

            lax.fori_loop(0, tt, issue, 0)

            @pl.when(i >= 1)
            def _():
                wait_rows((slot + 2) % 3)

            @pl.when(i == n - 1)
            def _():
                wait_rows(slot)


def _scatter_more_kernel(dest_ref, hm_ref, xs_in, xs_ref, sem):
    del xs_in
    _scatter_rows(dest_ref, hm_ref, xs_ref, sem)


def _scatter_init(zero_first, zero_last, dest_flat, hm, n_slot):
    n = hm.shape[0]
    tt = _tile(n, 512)
    n_t = n // tt
    grid_spec = pltpu.PrefetchScalarGridSpec(
        num_scalar_prefetch=2,
        grid=(n_t,),
        in_specs=[
            pl.BlockSpec((tt * TOP_K,), lambda i, zf, zl: (i,), memory_space=pltpu.SMEM),
            pl.BlockSpec(memory_space=pl.ANY),
        ],
        out_specs=pl.BlockSpec(memory_space=pl.ANY),
        scratch_shapes=[pltpu.VMEM((3, tt, D_MODEL), F32), pltpu.VMEM((MOE_ROWS, D_MODEL), F32),
                        pltpu.SemaphoreType.DMA((3,)), pltpu.SemaphoreType.DMA((3,)), pltpu.SemaphoreType.DMA(())],
    )
    return pl.pallas_call(
        functools.partial(_scatter_init_kernel, n=n_t, tt=tt),
        grid_spec=grid_spec,
        out_shape=jax.ShapeDtypeStruct((n_slot, D_MODEL), F32),
        compiler_params=_cparams("arbitrary"),
        name="moe_scatter_init",
    )(zero_first, zero_last, dest_flat, hm)


def _scatter_more(dest_flat, hm, xs):
    n = hm.shape[0]
    tt = _tile(n, 256)
    return pl.pallas_call(
        _scatter_more_kernel,
        grid=(n // tt,),
        in_specs=[
            pl.BlockSpec((tt * TOP_K,), lambda i: (i,), memory_space=pltpu.SMEM),
            pl.BlockSpec((tt, D_MODEL), lambda i: (i, 0)),
            pl.BlockSpec(memory_space=pl.ANY),
        ],
        out_specs=pl.BlockSpec(memory_space=pl.ANY),
        out_shape=jax.ShapeDtypeStruct(xs.shape, xs.dtype),
        scratch_shapes=[pltpu.SemaphoreType.DMA(())],
        input_output_aliases={2: 0},
        compiler_params=_cparams("arbitrary"),
        name="moe_scatter",
    )(dest_flat, hm, xs)


def _w1_split_kernel(w_ref, g_ref, l_ref, t_scr):
    n_slab = w_ref.shape[1] // LANES
    for c in range(n_slab):
        t_scr[c] = w_ref[0, c * LANES:(c + 1) * LANES, :].T
    for c in range(n_slab):
        g_ref[0, :, c * LANES:(c + 1) * LANES] = t_scr[c, pl.ds(0, D_EXPERT, stride=2), :].astype(BF16)
        l_ref[0, :, c * LANES:(c + 1) * LANES] = t_scr[c, pl.ds(1, D_EXPERT, stride=2), :].astype(BF16)


def _w1_split(w1):
    ks = 512
    n_slab = ks // LANES
    out = jax.ShapeDtypeStruct((N_EXPERTS, D_EXPERT, D_MODEL), BF16)
    return pl.pallas_call(
        _w1_split_kernel,
        grid=(N_EXPERTS, D_MODEL // ks),
        in_specs=[pl.BlockSpec((1, ks, 2 * D_EXPERT), lambda e, j: (e, j, 0))],
        out_specs=[pl.BlockSpec((1, D_EXPERT, ks), lambda e, j: (e, 0, j)),
                   pl.BlockSpec((1, D_EXPERT, ks), lambda e, j: (e, 0, j))],
        out_shape=[out, out],
        scratch_shapes=[pltpu.VMEM((n_slab, 2 * D_EXPERT, LANES), F32)],
        compiler_params=_cparams("parallel", "parallel"),
        name="w1_split",
    )(w1)


def _expert_kernel(be_ref, nb_ref, x_ref, w1g_ref, w1l_ref, b1g_ref, b1l_ref, w2_ref, b2_ref, o_ref, w2b_scr):
    i = pl.program_id(0)
    used = i < nb_ref[0]

    @pl.when(jnp.logical_not(used))
    def _():
        o_ref[...] = jnp.zeros_like(o_ref)

    @pl.when(jnp.logical_or(i == 0, be_ref[i] != be_ref[jnp.maximum(i - 1, 0)]))
    def _():
        w2b_scr[...] = w2_ref[0].astype(BF16)

    @pl.when(used)
    def _():
        x = x_ref[...].astype(BF16)
        nt = (((1,), (1,)), ((), ()))
        hg = lax.dot_general(x, w1g_ref[0], nt, preferred_element_type=F32) + b1g_ref[0]
        hl = lax.dot_general(x, w1l_ref[0], nt, preferred_element_type=F32) + b1l_ref[0]
        glu = jnp.minimum(hg, SWIGLU_LIMIT)
        lin = jnp.clip(hl, -SWIGLU_LIMIT, SWIGLU_LIMIT)
        act = glu * jax.nn.sigmoid(SWIGLU_ALPHA * glu) * (lin + 1.0)
        o_ref[...] = jnp.dot(act.astype(BF16), w2b_scr[...], preferred_element_type=F32) + b2_ref[0]


def _experts(blk_exp, n_used, xs, w1g_t, w1l_t, b1g, b1l, w2, b2):
    n_slot = xs.shape[0]
    n_blk = n_slot // MOE_ROWS
    row_in = lambda i, be, nb: (jnp.minimum(i, nb[0] - 1), 0)
    row = lambda i, be, nb: (i, 0)
    wsel = lambda i, be, nb: (be[i], 0, 0)
    grid_spec = pltpu.PrefetchScalarGridSpec(
        num_scalar_prefetch=2,
        grid=(n_blk,),
        in_specs=[
            pl.BlockSpec((MOE_ROWS, D_MODEL), row_in),
            pl.BlockSpec((1, D_EXPERT, D_MODEL), wsel),
            pl.BlockSpec((1, D_EXPERT, D_MODEL), wsel),
            pl.BlockSpec((1, 1, D_EXPERT), wsel),
            pl.BlockSpec((1, 1, D_EXPERT), wsel),
            pl.BlockSpec((1, D_EXPERT, D_MODEL), wsel),
            pl.BlockSpec((1, 1, D_MODEL), wsel),
        ],
        out_specs=pl.BlockSpec((MOE_ROWS, D_MODEL), row),
        scratch_shapes=[pltpu.VMEM((D_EXPERT, D_MODEL), BF16)],
    )
    return pl.pallas_call(
        _expert_kernel,
        grid_spec=grid_spec,
        out_shape=jax.ShapeDtypeStruct((n_slot, D_MODEL), F32),
        compiler_params=_cparams("arbitrary"),
        name="experts",
    )(blk_exp, n_used, xs, w1g_t, w1l_t, b1g, b1l, w2, b2)


def _combine_kernel(dest_ref, dnext_ref, x1_ref, g4_ref, nf_ref, ys_ref, o_ref, buf, sems, *, n):
    tt = x1_ref.shape[0]
    i = pl.program_id(0)

    def issue(d_ref, slot):
        def body(t, carry):
            for k in range(TOP_K):
                _row_copy(ys_ref, d_ref[t * TOP_K + k], buf.at[slot, k], t, sems.at[slot]).start(priority=k % 2)
            return carry

        lax.fori_loop(0, tt, body, 0, unroll=8)

    def finish(slot):
        for k in range(TOP_K):
            pltpu.make_async_copy(ys_ref.at[pl.ds(0, tt)], buf.at[slot, k], sems.at[slot]).wait()
        g4 = g4_ref[...]
        moe = g4[:, 0:1] * buf[slot, 0]
        for k in range(1, TOP_K):
            moe = moe + g4[:, k:k + 1] * buf[slot, k]
        o_ref[...] = _rms(x1_ref[...] + moe, nf_ref[...])

    @pl.when(i == 0)
    def _():
        issue(dest_ref, 0)

    for parity in range(2):
        @pl.when(i % 2 == parity)
        def _():
            @pl.when(i + 1 < n)
            def _():
                issue(dnext_ref, 1 - parity)

            finish(parity)


def _combine(dest_flat, x1, g4, nf, ys):
    n = x1.shape[0]
    tt = _tile(n, 512)
    n_t = n // tt
    row = lambda i: (i, 0)
    return pl.pallas_call(
        functools.partial(_combine_kernel, n=n_t),
        grid=(n_t,),
        in_specs=[
            pl.BlockSpec((tt * TOP_K,), lambda i: (i,), memory_space=pltpu.SMEM),
            pl.BlockSpec((tt * TOP_K,), lambda i: (jnp.minimum(i + 1, n_t - 1),), memory_space=pltpu.SMEM),
            pl.BlockSpec((tt, D_MODEL), row),
            pl.BlockSpec((tt, TOP_K), row),
            pl.BlockSpec((1, D_MODEL), lambda i: (0, 0)),
            pl.BlockSpec(memory_space=pl.ANY),
        ],
        out_specs=pl.BlockSpec((tt, D_MODEL), row),
        out_shape=jax.ShapeDtypeStruct((n, D_MODEL), F32),
        scratch_shapes=[pltpu.VMEM((2, TOP_K, tt, D_MODEL), F32), pltpu.SemaphoreType.DMA((2,))],
        compiler_params=_cparams("arbitrary"),
        name="moe_combine",
    )(dest_flat, dest_flat, x1, g4, nf, ys)


def _rope_table(pos):
    half = MLA_ROPE // 2
    inv_freq = ROPE_THETA ** (-jnp.arange(half, dtype=F32) / half)
    ang = pos.astype(F32)[:, None] * inv_freq[None, :]
    cos, sin = jnp.cos(ang), jnp.sin(ang)
    return jnp.concatenate([cos, cos, -sin, sin], axis=-1)


def _regroup_w_in(w_in):
    o = np.cumsum((Q_LORA, KV_LORA, MLA_ROPE, GLA_DK, GLA_DK, GLA_DV, GLA_GATE_RANK, GLA_DV, 2 * D_MODEL))
    c_q, c_kv, k_r = w_in[:, :o[0]], w_in[:, o[0]:o[1]], w_in[:, o[1]:o[2]]
    gq, gk, gv = w_in[:, o[2]:o[3]], w_in[:, o[3]:o[4]], w_in[:, o[4]:o[5]]
    ga, gg, mg = w_in[:, o[5]:o[6]], w_in[:, o[6]:o[7]], w_in[:, o[7]:o[8]]
    half = MLA_ROPE // 2
    k_r_sw = jnp.concatenate([k_r[:, half:], k_r[:, :half]], axis=1)
    ga_pad = jnp.zeros((D_MODEL, LANES - GLA_GATE_RANK), w_in.dtype)
    return jnp.concatenate([c_q, c_kv, k_r, k_r_sw, ga, ga_pad, gv, gg, gq, gk, mg], axis=1).astype(BF16)


def _regroup_w_uq(w_uq):
    w = w_uq.reshape(Q_LORA, MLA_HEADS, MLA_NOPE + MLA_ROPE)
    half = MLA_ROPE // 2
    nope, rope = w[..., :MLA_NOPE], w[..., MLA_NOPE:]
    rope_sw = jnp.concatenate([rope[..., half:], rope[..., :half]], axis=-1)
    return jnp.concatenate([nope, rope, rope_sw], axis=-1).reshape(Q_LORA, MLA_HEADS * HEAD_PAD).astype(BF16)


def _regroup_w_ukv(w_ukv):
    w = w_ukv.reshape(KV_LORA, MLA_HEADS, MLA_NOPE + MLA_V)
    k = w[..., :MLA_NOPE].reshape(KV_LORA, MLA_HEADS * MLA_NOPE)
    v = w[..., MLA_NOPE:].reshape(KV_LORA, MLA_HEADS * MLA_V)
    return jnp.concatenate([k, v], axis=1).astype(BF16)


def _per_head_w_ukv(w_ukv):
    w = w_ukv.reshape(KV_LORA, MLA_HEADS, MLA_NOPE + MLA_V)
    w_uk = jnp.transpose(w[..., :MLA_NOPE], (1, 2, 0)).astype(BF16)
    w_uv = jnp.transpose(w[..., MLA_NOPE:], (1, 0, 2)).astype(BF16)
    return w_uk, w_uv


def _slot_plan(counts_first, counts_rest, n_blk):
    counts_first = counts_first.astype(jnp.int32)
    counts = counts_first + counts_rest.astype(jnp.int32)
    padded = (counts + MOE_ROWS - 1) // MOE_ROWS * MOE_ROWS
    pad_end = jnp.cumsum(padded)
    pad_start = pad_end - padded
    n_used = (pad_end[-1] // MOE_ROWS).reshape(1)
    blk_start = jnp.arange(n_blk, dtype=jnp.int32) * MOE_ROWS
    blk_exp = jnp.sum(blk_start[:, None] >= pad_end[None, :], axis=1).astype(jnp.int32)
    last_exp = blk_exp[jnp.maximum(n_used[0] - 1, 0)]
    blk_exp = jnp.where(jnp.arange(n_blk) < n_used[0], blk_exp, last_exp)
    ps_row = jnp.pad(pad_start.astype(F32), (0, LANES - N_EXPERTS))[None, :]
    zero_first = jnp.concatenate([(pad_start + counts_first) // MOE_ROWS, n_used]).astype(jnp.int32)
    zero_last = jnp.concatenate([pad_end // MOE_ROWS, jnp.full((1,), n_blk)]).astype(jnp.int32)
    return ps_row, zero_first, zero_last, blk_exp, n_used


def _mixers(x, past_lat, past_kr, s0, past_len, p):
    bsz, seq, _ = x.shape
    n = bsz * seq
    x2d = x.reshape(n, D_MODEL)
    hin = _in_proj(x2d, p["norm1_g"], p["w_all"])
    tm = min(512, n)
    cs = _rope_table(past_len + jnp.arange(seq))
    if seq < tm:
        cs = jnp.tile(cs, (tm // seq, 1))
    q, lat_new, krp_new = _mla_q(hin, cs, p["q_norm_g"], p["kv_norm_g"], p["w_uq"])
    if past_len:
        attn = _mla_cached(q, past_lat, past_kr, lat_new, krp_new, p["w_uk"], p["w_uv"], bsz, seq)
    else:
        kcat, v = _kv_up(lat_new, krp_new, p["w_ukv"])
        tq = min(1024, seq)
        tk = 1024 if seq % 1024 == 0 else seq
        attn = _flash(q.reshape(bsz, seq, -1), kcat.reshape(bsz, seq, -1), v.reshape(bsz, seq, -1), 0, tq, tk)
    og, s_new = _gla(hin, p["w_alpha"], p["b_alpha"], p["gla_norm_g"], s0, bsz, seq)
    x1, hm, sel, g4, cnt = _post_mix(attn.reshape(n, -1), og, hin, x2d, p["w_o_mla"], p["w_o_gla"], p["w_out"],
                                     p["norm2_g"], p["w_router"], p["b_router"])
    new_state = (lat_new.reshape(bsz, seq, KV_LORA), krp_new[:, :MLA_ROPE].reshape(bsz, seq, MLA_ROPE), s_new)
    return x1, hm, sel, g4, cnt, new_state


def kernel(x_prompt, x_sample, cache_mla_latent, cache_mla_krope, state_gla, norm1_g, w_in, q_norm_g, kv_norm_g,
           w_uq, w_ukv, w_o_mla, w_alpha2, b_alpha, gla_norm_g, w_o_gla, w_out, norm2_g, w_router, b_router,
           w1, b1, w2, b2, normf_g):
    depth = w_in.shape[0]
    assert depth == 1
    l = 0
    bp, sp, _ = x_prompt.shape
    bs, ss, _ = x_sample.shape
    past_len = cache_mla_latent.shape[2]
    p = {
        "norm1_g": norm1_g[l][None, :],
        "w_all": _regroup_w_in(w_in[l]),
        "q_norm_g": q_norm_g[l][None, :],
        "kv_norm_g": kv_norm_g[l][None, :],
        "w_uq": _regroup_w_uq(w_uq[l]),
        "w_ukv": _regroup_w_ukv(w_ukv[l]),
        "w_uk": _per_head_w_ukv(w_ukv[l])[0],
        "w_uv": _per_head_w_ukv(w_ukv[l])[1],
        "w_o_mla": w_o_mla[l].astype(BF16),
        "w_alpha": jnp.pad(w_alpha2[l], ((0, LANES - GLA_GATE_RANK), (0, 0))).astype(BF16),
        "b_alpha": b_alpha[l][None, :],
        "gla_norm_g": gla_norm_g[l][None, :],
        "w_o_gla": w_o_gla[l].astype(BF16),
        "w_out": w_out[l].astype(BF16),
        "norm2_g": norm2_g[l][None, :],
        "w_router": jnp.pad(w_router[l], ((0, 0), (0, LANES - N_EXPERTS))).astype(BF16),
        "b_router": jnp.pad(b_router[l], (0, LANES - N_EXPERTS))[None, :],
    }
    zeros_state = jnp.zeros((bp, GLA_HEADS, GLA_DK_HEAD, GLA_DV_HEAD), F32)
    x1p, hmp, selp, g4p, cntp, (lat_p, kr_p, st_p) = _mixers(x_prompt, None, None, zeros_state, 0, p)
    x1s, hms, sels, g4s, cnts, (lat_s, kr_s, st_s) = _mixers(x_sample, cache_mla_latent[l], cache_mla_krope[l],
                                                              state_gla[l], past_len, p)

    n_p, n_s = bp * sp, bs * ss
    n_asg = (n_p + n_s) * TOP_K
    n_slot = -(-n_asg // MOE_ROWS) * MOE_ROWS + N_EXPERTS * MOE_ROWS
    ps_row, zero_first, zero_last, blk_exp, n_used = _slot_plan(cntp[0, :N_EXPERTS], cnts[0, :N_EXPERTS],
                                                                n_slot // MOE_ROWS)
    dest_p = _dest(selp, ps_row).reshape(n_p * TOP_K)
    dest_s = _dest(sels, ps_row + cntp).reshape(n_s * TOP_K)
    xs = _scatter_init(zero_first, zero_last, dest_p, hmp, n_slot)
    xs = _scatter_more(dest_s, hms, xs)
    w1g_t, w1l_t = _w1_split(w1[l])
    b1g = b1[l][:, None, 0::2]
    b1lin = b1[l][:, None, 1::2]
    ys_slots = _experts(blk_exp, n_used, xs, w1g_t, w1l_t, b1g, b1lin, w2[l], b2[l][:, None, :])
    yp = _combine(dest_p, x1p, g4p, normf_g[None, :], ys_slots).reshape(bp, sp, D_MODEL)
    ys = _combine(dest_s, x1s, g4s, normf_g[None, :], ys_slots).reshape(bs, ss, D_MODEL)
    return (yp, ys, lat_p[None], kr_p[None], st_p[None], lat_s[None], kr_s[None], st_s[None])
```

```python
import functools

import jax
import jax.numpy as jnp
import numpy as np
from jax import lax
from jax.experimental import pallas as pl
from jax.experimental.pallas import tpu as pltpu

F32 = jnp.float32
BF16 = jnp.bfloat16

D_MODEL = 1024
CHUNK = 64
CHUNK_SHIFT = 6
LOG2_E = 1.4426950408889634
NORM_EPS = 1e-6
MLA_HEADS = 8
MLA_NOPE = 128
MLA_ROPE = 64
MLA_V = 128
Q_LORA = 512
KV_LORA = 256
ROPE_THETA = 10000.0
MLA_SCALE = (MLA_NOPE + MLA_ROPE) ** -0.5
NEG_INF = -1e30
GLA_HEADS = 4
GLA_DK = D_MODEL // 2
GLA_DV = D_MODEL
GLA_DK_HEAD = GLA_DK // GLA_HEADS
GLA_DV_HEAD = GLA_DV // GLA_HEADS
GLA_GATE_RANK = 16
GLA_GATE_TEMP = 16.0
GLA_SUB = 16
FLASH_HEADS = 2
POST_MIX_PARTS = 2
N_EXPERTS = 32
TOP_K = 4
D_EXPERT = D_MODEL
SWIGLU_LIMIT = 7.0
SWIGLU_ALPHA = 1.702
MOE_ROWS = 512

LANES = 128
HEAD_PAD = 256

COL_CQ = 0
COL_CKV = 512
COL_KR = 768
COL_GA = 896
COL_GV = 1024
COL_GG = 2048
COL_GQ = 3072
COL_GK = 3584
COL_MG = 4096
D_IN_PAD = 6144

V7X_VMEM_BYTES = 64 * 1024 * 1024
VMEM_LIMIT = V7X_VMEM_BYTES - 8 * 1024 * 1024


def _cparams(*sem):
    return pltpu.CompilerParams(dimension_semantics=sem, vmem_limit_bytes=VMEM_LIMIT)


def _tile(n, pref, mult=16):
    t = min(pref, n)
    t -= t % mult
    while n % t:
        t -= mult
    return t


def _rms(x, g):
    return x * lax.rsqrt(jnp.mean(x * x, axis=-1, keepdims=True) + NORM_EPS) * g


def _in_proj_kernel(x_ref, g_ref, w_ref, o_ref):
    h = _rms(x_ref[...], g_ref[...]).astype(BF16)
    tn = 1536
    for j in range(D_IN_PAD // tn):
        o_ref[:, j * tn:(j + 1) * tn] = jnp.dot(h, w_ref[:, j * tn:(j + 1) * tn], preferred_element_type=F32)


def _in_proj(x2d, g, w_all):
    n = x2d.shape[0]
    tm = min(256, n)
    return pl.pallas_call(
        _in_proj_kernel,
        grid=(n // tm,),
        in_specs=[
            pl.BlockSpec((tm, D_MODEL), lambda i: (i, 0)),
            pl.BlockSpec((1, D_MODEL), lambda i: (0, 0)),
            pl.BlockSpec((D_MODEL, D_IN_PAD), lambda i: (0, 0)),
        ],
        out_specs=pl.BlockSpec((tm, D_IN_PAD), lambda i: (i, 0)),
        out_shape=jax.ShapeDtypeStruct((n, D_IN_PAD), F32),
        compiler_params=_cparams("parallel"),
        name="in_proj",
    )(x2d, g, w_all)


def _rope_pair(blk, cs):
    t = blk * cs
    return t + pltpu.roll(t, MLA_ROPE, axis=1)


def _mla_q_kernel(h_ref, cs_ref, qg_ref, kvg_ref, wuq_ref, q_ref, lat_ref, kr_ref):
    cs = cs_ref[...]
    qn = _rms(h_ref[:, COL_CQ:COL_CQ + Q_LORA], qg_ref[...]).astype(BF16)
    q = jnp.dot(qn, wuq_ref[...], preferred_element_type=F32)
    for hh in range(MLA_HEADS):
        c0 = hh * HEAD_PAD
        q_ref[:, c0:c0 + MLA_NOPE] = q[:, c0:c0 + MLA_NOPE].astype(BF16)
        q_ref[:, c0 + MLA_NOPE:c0 + HEAD_PAD] = _rope_pair(q[:, c0 + MLA_NOPE:c0 + HEAD_PAD], cs).astype(BF16)
    lat_ref[...] = _rms(h_ref[:, COL_CKV:COL_CKV + KV_LORA], kvg_ref[...])
    r = _rope_pair(h_ref[:, COL_KR:COL_KR + LANES], cs)
    lane = lax.broadcasted_iota(jnp.int32, r.shape, 1)
    kr_ref[...] = jnp.where(lane < MLA_ROPE, r, 0.0)


def _mla_q(hin, cs, qg, kvg, wuq):
    n = hin.shape[0]
    tm = min(512, n)
    n_cs = cs.shape[0] // tm
    return pl.pallas_call(
        _mla_q_kernel,
        grid=(n // tm,),
        in_specs=[
            pl.BlockSpec((tm, COL_GA), lambda i: (i, 0)),
            pl.BlockSpec((tm, LANES), lambda i: (i % n_cs, 0)),
            pl.BlockSpec((1, Q_LORA), lambda i: (0, 0)),
            pl.BlockSpec((1, KV_LORA), lambda i: (0, 0)),
            pl.BlockSpec((Q_LORA, MLA_HEADS * HEAD_PAD), lambda i: (0, 0)),
        ],
        out_specs=[
            pl.BlockSpec((tm, MLA_HEADS * HEAD_PAD), lambda i: (i, 0)),
            pl.BlockSpec((tm, KV_LORA), lambda i: (i, 0)),
            pl.BlockSpec((tm, LANES), lambda i: (i, 0)),
        ],
        out_shape=[
            jax.ShapeDtypeStruct((n, MLA_HEADS * HEAD_PAD), BF16),
            jax.ShapeDtypeStruct((n, KV_LORA), F32),
            jax.ShapeDtypeStruct((n, LANES), F32),
        ],
        compiler_params=_cparams("parallel"),
        name="mla_q",
    )(hin, cs, qg, kvg, wuq)


def _kv_up_kernel(lat_ref, kr_ref, w_ref, k_ref, v_ref):
    kv = jnp.dot(lat_ref[...].astype(BF16), w_ref[...], preferred_element_type=F32)
    krb = kr_ref[...].astype(BF16)
    for hh in range(MLA_HEADS):
        c0 = hh * HEAD_PAD
        k_ref[:, c0:c0 + MLA_NOPE] = kv[:, hh * MLA_NOPE:(hh + 1) * MLA_NOPE].astype(BF16)
        k_ref[:, c0 + MLA_NOPE:c0 + HEAD_PAD] = krb
    v_ref[...] = kv[:, MLA_HEADS * MLA_NOPE:].astype(BF16)


def _kv_up(lat, krp, wukv):
    n = lat.shape[0]
    tm = _tile(n, 512)
    return pl.pallas_call(
        _kv_up_kernel,
        grid=(n // tm,),
        in_specs=[
            pl.BlockSpec((tm, KV_LORA), lambda i: (i, 0)),
            pl.BlockSpec((tm, LANES), lambda i: (i, 0)),
            pl.BlockSpec((KV_LORA, MLA_HEADS * (MLA_NOPE + MLA_V)), lambda i: (0, 0)),
        ],
        out_specs=[
            pl.BlockSpec((tm, MLA_HEADS * HEAD_PAD), lambda i: (i, 0)),
            pl.BlockSpec((tm, MLA_HEADS * MLA_V), lambda i: (i, 0)),
        ],
        out_shape=[
            jax.ShapeDtypeStruct((n, MLA_HEADS * HEAD_PAD), BF16),
            jax.ShapeDtypeStruct((n, MLA_HEADS * MLA_V), BF16),
        ],
        compiler_params=_cparams("parallel"),
        name="kv_up",
    )(lat, krp, wukv)


def _flash_kernel(q_ref, k_ref, v_ref, o_ref, *, tq, tk, past_len):
    qi = pl.program_id(2)
    q_pos0 = past_len + qi * tq
    q_chunk = (q_pos0 + lax.broadcasted_iota(jnp.int32, (tq, 1), 0)) >> CHUNK_SHIFT
    n_blocks = (q_pos0 + tq + tk - 1) // tk
    n_full = ((q_pos0 >> CHUNK_SHIFT) << CHUNK_SHIFT) // tk
    qs = [q_ref[0, :, hh * HEAD_PAD:(hh + 1) * HEAD_PAD] for hh in range(FLASH_HEADS)]

    def step(j, carry, masked):
        k0 = pl.multiple_of(j * tk, tk)
        if masked:
            k_chunk = (k0 + lax.broadcasted_iota(jnp.int32, (1, tk), 1)) >> CHUNK_SHIFT
        out = []
        for hh in range(FLASH_HEADS):
            m, l, acc = carry[hh]
            k = k_ref[0, pl.ds(k0, tk), hh * HEAD_PAD:(hh + 1) * HEAD_PAD]
            v = v_ref[0, pl.ds(k0, tk), hh * MLA_V:(hh + 1) * MLA_V]
            s = lax.dot_general(qs[hh], k, (((1,), (1,)), ((), ())), preferred_element_type=F32)
            s = s * (MLA_SCALE * LOG2_E)
            if masked:
                s = jnp.where(k_chunk <= q_chunk, s, NEG_INF)
            m_new = jnp.maximum(m, jnp.max(s, axis=-1, keepdims=True))
            alpha = jnp.exp2(m - m_new)
            p = jnp.exp2(s - m_new)
            l = alpha * l + jnp.sum(p, axis=-1, keepdims=True)
            acc = alpha * acc + jnp.dot(p.astype(BF16), v, preferred_element_type=F32)
            out.append((m_new, l, acc))
        return tuple(out)

    init = tuple((jnp.full((tq, 1), NEG_INF, F32), jnp.zeros((tq, 1), F32), jnp.zeros((tq, MLA_V), F32))
                 for _ in range(FLASH_HEADS))
    carry = lax.fori_loop(0, n_full, functools.partial(step, masked=False), init)
    carry = lax.fori_loop(n_full, n_blocks, functools.partial(step, masked=True), carry)
    for hh in range(FLASH_HEADS):
        _, l, acc = carry[hh]
        o_ref[0, :, hh * MLA_V:(hh + 1) * MLA_V] = (acc / l).astype(BF16)


def _flash(q, k, v, past_len, tq, tk):
    b, sq, _ = q.shape
    sk = k.shape[1]
    fh = FLASH_HEADS
    return pl.pallas_call(
        functools.partial(_flash_kernel, tq=tq, tk=tk, past_len=past_len),
        grid=(b, MLA_HEADS // fh, sq // tq),
        in_specs=[
            pl.BlockSpec((1, tq, fh * HEAD_PAD), lambda bi, h, i: (bi, i, h)),
            pl.BlockSpec((1, sk, fh * HEAD_PAD), lambda bi, h, i: (bi, 0, h)),
            pl.BlockSpec((1, sk, fh * MLA_V), lambda bi, h, i: (bi, 0, h)),
        ],
        out_specs=pl.BlockSpec((1, tq, fh * MLA_V), lambda bi, h, i: (bi, i, h)),
        out_shape=jax.ShapeDtypeStruct((b, sq, MLA_HEADS * MLA_V), BF16),
        compiler_params=_cparams("parallel", "parallel", "arbitrary"),
        name="flash",
    )(q, k, v)


def _mla_cached_kernel(q_ref, clat_ref, ckr_ref, nlat_ref, nkr_ref, wuk_ref, wuv_ref, o_ref, *, past_len):
    seq = q_ref.shape[0]
    nt = (((1,), (1,)), ((), ()))
    q_abs = jnp.concatenate(
        [jnp.dot(q_ref[:, hh * HEAD_PAD:hh * HEAD_PAD + MLA_NOPE], wuk_ref[hh], preferred_element_type=F32)
         for hh in range(MLA_HEADS)], axis=0).astype(BF16)
    q_rot = jnp.concatenate([q_ref[:, hh * HEAD_PAD + MLA_NOPE:(hh + 1) * HEAD_PAD] for hh in range(MLA_HEADS)],
                            axis=0)
    pos_q = past_len + lax.broadcasted_iota(jnp.int32, (seq, 1), 0)
    q_chunk = jnp.concatenate([pos_q] * MLA_HEADS, axis=0) >> CHUNK_SHIFT

    def scores(lat, kr, k_pos0):
        n_k = lat.shape[0]
        kr_pad = jnp.concatenate([kr, jnp.zeros_like(kr)], axis=1)
        s = (lax.dot_general(q_abs, lat, nt, preferred_element_type=F32)
             + lax.dot_general(q_rot, kr_pad, nt, preferred_element_type=F32)) * (MLA_SCALE * LOG2_E)
        k_chunk = (k_pos0 + lax.broadcasted_iota(jnp.int32, (1, n_k), 1)) >> CHUNK_SHIFT
        return jnp.where(k_chunk <= q_chunk, s, NEG_INF)

    lat_c = clat_ref[0].astype(BF16)
    lat_n = nlat_ref[...].astype(BF16)
    s_c = scores(lat_c, ckr_ref[0].astype(BF16), 0)
    s_n = scores(lat_n, nkr_ref[:, :MLA_ROPE].astype(BF16), past_len)
    m = jnp.maximum(jnp.max(s_c, axis=-1, keepdims=True), jnp.max(s_n, axis=-1, keepdims=True))
    p_c = jnp.exp2(s_c - m)
    p_n = jnp.exp2(s_n - m)
    l = jnp.sum(p_c, axis=-1, keepdims=True) + jnp.sum(p_n, axis=-1, keepdims=True)
    o_lat = (jnp.dot(p_c.astype(BF16), lat_c, preferred_element_type=F32)
             + jnp.dot(p_n.astype(BF16), lat_n, preferred_element_type=F32)) / l
    o_lat = o_lat.astype(BF16)
    for hh in range(MLA_HEADS):
        o_ref[:, hh * MLA_V:(hh + 1) * MLA_V] = jnp.dot(o_lat[hh * seq:(hh + 1) * seq], wuv_ref[hh],
                                                         preferred_element_type=F32).astype(BF16)


def _mla_cached(q, cache_lat, cache_kr, lat_new, krp_new, w_uk, w_uv, bsz, seq):
    past_len = cache_lat.shape[1]
    return pl.pallas_call(
        functools.partial(_mla_cached_kernel, past_len=past_len),
        grid=(bsz,),
        in_specs=[
            pl.BlockSpec((seq, MLA_HEADS * HEAD_PAD), lambda b: (b, 0)),
            pl.BlockSpec((1, past_len, KV_LORA), lambda b: (b, 0, 0)),
            pl.BlockSpec((1, past_len, MLA_ROPE), lambda b: (b, 0, 0)),
            pl.BlockSpec((seq, KV_LORA), lambda b: (b, 0)),
            pl.BlockSpec((seq, LANES), lambda b: (b, 0)),
            pl.BlockSpec((MLA_HEADS, MLA_NOPE, KV_LORA), lambda b: (0, 0, 0)),
            pl.BlockSpec((MLA_HEADS, KV_LORA, MLA_V), lambda b: (0, 0, 0)),
        ],
        out_specs=pl.BlockSpec((seq, MLA_HEADS * MLA_V), lambda b: (b, 0)),
        out_shape=jax.ShapeDtypeStruct((bsz * seq, MLA_HEADS * MLA_V), BF16),
        compiler_params=_cparams("parallel"),
        name="mla_cached",
    )(q, cache_lat, cache_kr, lat_new, krp_new, w_uk, w_uv)


def _cumsum_rows(x):
    n = x.shape[0]
    row = lax.broadcasted_iota(jnp.int32, x.shape, 0)
    s = 1
    while s < n:
        x = x + jnp.where(row >= s, pltpu.roll(x, s, axis=0), 0.0)
        s *= 2
    return x


def _gla_kernel(gv_ref, gg_ref, gq_ref, gk_ref, ga_ref, wa_ref, ba_ref, ng_ref, s0_ref, og_ref, sn_ref, st_scr,
                *, csize, n_chunks):
    t = pl.program_id(1)

    @pl.when(t == 0)
    def _():
        for hh in range(GLA_HEADS):
            st_scr[hh] = s0_ref[0, hh].T

    wa = wa_ref[...]
    ba = ba_ref[...]
    ng = ng_ref[...]
    n_sub = csize // GLA_SUB
    col_id = lax.broadcasted_iota(jnp.int32, (GLA_SUB, csize), 1)
    row_in_sub = lax.broadcasted_iota(jnp.int32, (GLA_SUB, csize), 0)

    def chunk(c, carry):
        r0 = pl.multiple_of(c * csize, csize)
        rows = pl.ds(r0, csize)
        x = jnp.dot(ga_ref[rows, :].astype(BF16), wa, preferred_element_type=F32) + ba
        la = jax.nn.log_sigmoid(x) / GLA_GATE_TEMP
        b_all = _cumsum_rows(la) * LOG2_E
        for hh in range(GLA_HEADS):
            head(hh, rows, b_all[:, hh * GLA_DK_HEAD:(hh + 1) * GLA_DK_HEAD])
        return carry

    def head(hh, rows, b):
        kcols = slice(hh * GLA_DK_HEAD, (hh + 1) * GLA_DK_HEAD)
        vcols = slice(hh * GLA_DV_HEAD, (hh + 1) * GLA_DV_HEAD)
        q = gq_ref[rows, kcols] * (GLA_DK_HEAD ** -0.5)
        k = gk_ref[rows, kcols]
        v = gv_ref[rows, vcols]

        a_rows = []
        for i in range(n_sub):
            lo = i * GLA_SUB
            bi = b[lo:lo + GLA_SUB]
            qi = q[lo:lo + GLA_SUB]
            ki = k[lo:lo + GLA_SUB]
            if i == 0:
                a_i = jnp.zeros((GLA_SUB, csize), F32)
            else:
                ri = b[lo - 1:lo]
                qs = qi * jnp.exp2(bi - ri)
                ks = k * jnp.exp2(jnp.minimum(ri - b, 0.0))
                a_i = lax.dot_general(qs.astype(BF16), ks.astype(BF16), (((1,), (1,)), ((), ())),
                                      preferred_element_type=F32)
            for s in range(GLA_SUB):
                col = jnp.sum(qi * ki[s:s + 1] * jnp.exp2(bi - bi[s:s + 1]), axis=-1, keepdims=True)
                a_i = jnp.where(col_id == lo + s, col, a_i)
            a_rows.append(jnp.where(col_id <= lo + row_in_sub, a_i, 0.0))
        a = jnp.concatenate(a_rows, axis=0)

        st = st_scr[hh]
        vb = v.astype(BF16)
        o = jnp.dot(a.astype(BF16), vb, preferred_element_type=F32)
        o = o + lax.dot_general((q * jnp.exp2(b)).astype(BF16), st.astype(BF16), (((1,), (1,)), ((), ())),
                                preferred_element_type=F32)
        b_end = b[csize - 1:csize]
        kd = (k * jnp.exp2(b_end - b)).astype(BF16)
        st_scr[hh] = jnp.exp2(b_end) * st + jnp.dot(v.T.astype(BF16), kd, preferred_element_type=F32)
        on = _rms(o, ng)
        og_ref[rows, vcols] = (on * jax.nn.silu(gg_ref[rows, vcols])).astype(BF16)

    lax.fori_loop(0, n_chunks, chunk, 0, unroll=2 if n_chunks % 2 == 0 else 1)

    @pl.when(t == pl.num_programs(1) - 1)
    def _():
        for hh in range(GLA_HEADS):
            sn_ref[0, hh] = st_scr[hh].T


def _gla(hin, wa, ba, ng, s0, bsz, seq):
    csize = min(CHUNK, seq)
    tt = min(1024, seq)
    n_t = seq // tt
    n = bsz * seq
    kern = functools.partial(_gla_kernel, csize=csize, n_chunks=tt // csize)
    state = pl.BlockSpec((1, GLA_HEADS, GLA_DK_HEAD, GLA_DV_HEAD), lambda b, t: (b, 0, 0, 0))
    return pl.pallas_call(
        kern,
        grid=(bsz, n_t),
        in_specs=[
            pl.BlockSpec((tt, GLA_DV), lambda b, t: (b * n_t + t, COL_GV // GLA_DV)),
            pl.BlockSpec((tt, GLA_DV), lambda b, t: (b * n_t + t, COL_GG // GLA_DV)),
            pl.BlockSpec((tt, GLA_DK), lambda b, t: (b * n_t + t, COL_GQ // GLA_DK)),
            pl.BlockSpec((tt, GLA_DK), lambda b, t: (b * n_t + t, COL_GK // GLA_DK)),
            pl.BlockSpec((tt, LANES), lambda b, t: (b * n_t + t, COL_GA // LANES)),
            pl.BlockSpec((LANES, GLA_DK), lambda b, t: (0, 0)),
            pl.BlockSpec((1, GLA_DK), lambda b, t: (0, 0)),
            pl.BlockSpec((1, GLA_DV_HEAD), lambda b, t: (0, 0)),
            state,
        ],
        out_specs=[pl.BlockSpec((tt, GLA_DV), lambda b, t: (b * n_t + t, 0)), state],
        out_shape=[
            jax.ShapeDtypeStruct((n, GLA_DV), BF16),
            jax.ShapeDtypeStruct((bsz, GLA_HEADS, GLA_DK_HEAD, GLA_DV_HEAD), F32),
        ],
        scratch_shapes=[pltpu.VMEM((GLA_HEADS, GLA_DV_HEAD, GLA_DK_HEAD), F32)],
        compiler_params=_cparams("parallel", "arbitrary"),
        name="gla",
    )(hin, hin, hin, hin, hin, wa, ba, ng, s0)


def _post_mix_kernel(at_ref, og_ref, mg1_ref, mg2_ref, x_ref, wom_ref, wog_ref, wout_ref, n2_ref, wr_ref, br_ref,
                     x1_ref, hm_ref, sel_ref, g4_ref, cnt_ref):
    @pl.when(pl.program_id(0) == 0)
    def _():
        cnt_ref[...] = jnp.zeros_like(cnt_ref)

    tm = x_ref.shape[0]
    half = tm // POST_MIX_PARTS
    for part in range(POST_MIX_PARTS):
        rows = slice(part * half, (part + 1) * half)
        y_mla = jnp.dot(at_ref[rows, :], wom_ref[...], preferred_element_type=F32)
        y_gla = jnp.dot(og_ref[rows, :], wog_ref[...], preferred_element_type=F32)
        m = jax.nn.sigmoid(mg1_ref[rows, :]) * y_mla + jax.nn.sigmoid(mg2_ref[rows, :]) * y_gla
        x1 = x_ref[rows, :] + jnp.dot(m.astype(BF16), wout_ref[...], preferred_element_type=F32)
        x1_ref[rows, :] = x1
        hm = _rms(x1, n2_ref[...])
        hm_ref[rows, :] = hm
        logits = jnp.dot(hm.astype(BF16), wr_ref[...], preferred_element_type=F32) + br_ref[...]

        lane = lax.broadcasted_iota(jnp.int32, logits.shape, 1)
        work = jnp.where(lane < N_EXPERTS, logits, -jnp.inf)
        sel = jnp.zeros(logits.shape, F32)
        vals = []
        for k in range(TOP_K):
            mk = jnp.max(work, axis=-1, keepdims=True)
            ik = jnp.min(jnp.where(work == mk, lane, LANES), axis=-1, keepdims=True)
            hit = lane == ik
            sel = jnp.where(hit, float(k + 1), sel)
            work = jnp.where(hit, -jnp.inf, work)
            vals.append(mk)
        sel_ref[rows, :] = sel
        ex = [jnp.exp(v - vals[0]) for v in vals]
        den = ex[0]
        for e in ex[1:]:
            den = den + e
        g4 = jnp.zeros(logits.shape, F32)
        for k in range(TOP_K):
            g4 = jnp.where(lane == k, ex[k] / den, g4)
        g4_ref[rows, :] = g4[:, :TOP_K]
        cnt_ref[...] += jnp.sum(jnp.where(sel > 0.0, 1.0, 0.0), axis=0, keepdims=True)


def _post_mix(attn, og, hin, x2d, wom, wog, wout, n2, wr, br):
    n = x2d.shape[0]
    tm = min(256 * POST_MIX_PARTS, n)
    row = lambda i: (i, 0)
    const = lambda i: (0, 0)
    return pl.pallas_call(
        _post_mix_kernel,
        grid=(n // tm,),
        in_specs=[
            pl.BlockSpec((tm, D_MODEL), row),
            pl.BlockSpec((tm, D_MODEL), row),
            pl.BlockSpec((tm, D_MODEL), lambda i: (i, COL_MG // D_MODEL)),
            pl.BlockSpec((tm, D_MODEL), lambda i: (i, COL_MG // D_MODEL + 1)),
            pl.BlockSpec((tm, D_MODEL), row),
            pl.BlockSpec((D_MODEL, D_MODEL), const),
            pl.BlockSpec((D_MODEL, D_MODEL), const),
            pl.BlockSpec((D_MODEL, D_MODEL), const),
            pl.BlockSpec((1, D_MODEL), const),
            pl.BlockSpec((D_MODEL, LANES), const),
            pl.BlockSpec((1, LANES), const),
        ],
        out_specs=[
            pl.BlockSpec((tm, D_MODEL), row),
            pl.BlockSpec((tm, D_MODEL), row),
            pl.BlockSpec((tm, LANES), row),
            pl.BlockSpec((tm, TOP_K), row),
            pl.BlockSpec((1, LANES), const),
        ],
        out_shape=[
            jax.ShapeDtypeStruct((n, D_MODEL), F32),
            jax.ShapeDtypeStruct((n, D_MODEL), F32),
            jax.ShapeDtypeStruct((n, LANES), F32),
            jax.ShapeDtypeStruct((n, TOP_K), F32),
            jax.ShapeDtypeStruct((1, LANES), F32),
        ],
        compiler_params=_cparams("arbitrary"),
        name="post_mix",
    )(attn, og, hin, hin, x2d, wom, wog, wout, n2, wr, br)


def _dest_kernel(sel_ref, ps_ref, dest_ref, run_scr):
    @pl.when(pl.program_id(0) == 0)
    def _():
        run_scr[...] = ps_ref[...]

    sel = sel_ref[...]
    tt = sel.shape[0]
    oh = jnp.where(sel > 0.0, 1.0, 0.0)
    r = lax.broadcasted_iota(jnp.int32, (tt, tt), 0)
    c = lax.broadcasted_iota(jnp.int32, (tt, tt), 1)
    ltri = jnp.where(r > c, 1.0, 0.0).astype(BF16)
    slot = jnp.dot(ltri, oh.astype(BF16), preferred_element_type=F32) + run_scr[...]
    run_scr[...] += jnp.sum(oh, axis=0, keepdims=True)
    lane = lax.broadcasted_iota(jnp.int32, sel.shape, 1)
    d = jnp.zeros(sel.shape, F32)
    for k in range(TOP_K):
        col = jnp.sum(jnp.where(sel == float(k + 1), slot, 0.0), axis=-1, keepdims=True)
        d = jnp.where(lane == k, col, d)
    dest_ref[...] = d[:, :TOP_K].astype(jnp.int32)


def _dest(sel, pad_start):
    n = sel.shape[0]
    tt = _tile(n, 512)
    return pl.pallas_call(
        _dest_kernel,
        grid=(n // tt,),
        in_specs=[pl.BlockSpec((tt, LANES), lambda i: (i, 0)), pl.BlockSpec((1, LANES), lambda i: (0, 0))],
        out_specs=pl.BlockSpec((tt, TOP_K), lambda i: (i, 0)),
        out_shape=jax.ShapeDtypeStruct((n, TOP_K), jnp.int32),
        scratch_shapes=[pltpu.VMEM((1, LANES), F32)],
        compiler_params=_cparams("arbitrary"),
        name="route_dest",
    )(sel, pad_start)


def _row_copy(src, src_row, dst, dst_row, sem):
    return pltpu.make_async_copy(src.at[pl.ds(src_row, 1)], dst.at[pl.ds(dst_row, 1)], sem)


def _scatter_rows(dest_ref, hm_ref, xs_ref, sem):
    tt = hm_ref.shape[0]

    def issue(t, carry):
        for k in range(TOP_K):
            _row_copy(hm_ref, t, xs_ref, dest_ref[t * TOP_K + k], sem).start()
        return carry

    lax.fori_loop(0, tt, issue, 0)
    for k in range(TOP_K):
        pltpu.make_async_copy(hm_ref, xs_ref.at[pl.ds(0, tt)], sem).wait()


def _scatter_init_kernel(zf_ref, zl_ref, dest_ref, hm_hbm, xs_ref, hbuf, zero_scr, lsems, ssems, zsem, *, n, tt):
    i = pl.program_id(0)

    def load(b, slot):
        return pltpu.make_async_copy(hm_hbm.at[pl.ds(pl.multiple_of(b * tt, tt), tt)], hbuf.at[slot], lsems.at[slot])

    def wait_rows(slot):
        for _ in range(TOP_K):
            pltpu.make_async_copy(hbuf.at[slot], xs_ref.at[pl.ds(0, tt)], ssems.at[slot]).wait()

    @pl.when(i == 0)
    def _():
        load(0, 0).start()
        zero_scr[...] = jnp.zeros_like(zero_scr)

        def zero_block(blk):
            return pltpu.make_async_copy(zero_scr,
                                         xs_ref.at[pl.ds(pl.multiple_of(blk * MOE_ROWS, MOE_ROWS), MOE_ROWS)], zsem)

        def start_block(blk, carry):
            zero_block(blk).start()
            return carry

        def wait_block(blk, carry):
            zero_block(blk).wait()
            return carry

        def start_range(e, carry):
            return lax.fori_loop(zf_ref[e], zl_ref[e], start_block, carry)

        def wait_range(e, carry):
            return lax.fori_loop(zf_ref[e], zl_ref[e], wait_block, carry)

        lax.fori_loop(0, N_EXPERTS + 1, start_range, 0)
        lax.fori_loop(0, N_EXPERTS + 1, wait_range, 0)

    for slot in range(3):
        @pl.when(i % 3 == slot)
        def _():
            @pl.when(i + 1 < n)
            def _():
                load(i + 1, (slot + 1) % 3).start()

            load(i, slot).wait()

            def issue(t, carry):
                for k in range(TOP_K):
                    _row_copy(hbuf.at[slot], t, xs_ref, dest_ref[t * TOP_K + k], ssems.at[slot]).start(priority=k % 2)
                return carry

            lax.fori_loop(0, tt, issue, 0)

            @pl.when(i >= 1)
            def _():
                wait_rows((slot + 2) % 3)

            @pl.when(i == n - 1)
            def _():
                wait_rows(slot)


def _scatter_more_kernel(dest_ref, hm_ref, xs_in, xs_ref, sem):
    del xs_in
    _scatter_rows(dest_ref, hm_ref, xs_ref, sem)


def _scatter_init(zero_first, zero_last, dest_flat, hm, n_slot):
    n = hm.shape[0]
    tt = _tile(n, 512)
    n_t = n // tt
    grid_spec = pltpu.PrefetchScalarGridSpec(
        num_scalar_prefetch=2,
        grid=(n_t,),
        in_specs=[
            pl.BlockSpec((tt * TOP_K,), lambda i, zf, zl: (i,), memory_space=pltpu.SMEM),
            pl.BlockSpec(memory_space=pl.ANY),
        ],
        out_specs=pl.BlockSpec(memory_space=pl.ANY),
        scratch_shapes=[pltpu.VMEM((3, tt, D_MODEL), F32), pltpu.VMEM((MOE_ROWS, D_MODEL), F32),
                        pltpu.SemaphoreType.DMA((3,)), pltpu.SemaphoreType.DMA((3,)), pltpu.SemaphoreType.DMA(())],
    )
    return pl.pallas_call(
        functools.partial(_scatter_init_kernel, n=n_t, tt=tt),
        grid_spec=grid_spec,
        out_shape=jax.ShapeDtypeStruct((n_slot, D_MODEL), F32),
        compiler_params=_cparams("arbitrary"),
        name="moe_scatter_init",
    )(zero_first, zero_last, dest_flat, hm)


def _scatter_more(dest_flat, hm, xs):
    n = hm.shape[0]
    tt = _tile(n, 256)
    return pl.pallas_call(
        _scatter_more_kernel,
        grid=(n // tt,),
        in_specs=[
            pl.BlockSpec((tt * TOP_K,), lambda i: (i,), memory_space=pltpu.SMEM),
            pl.BlockSpec((tt, D_MODEL), lambda i: (i, 0)),
            pl.BlockSpec(memory_space=pl.ANY),
        ],
        out_specs=pl.BlockSpec(memory_space=pl.ANY),
        out_shape=jax.ShapeDtypeStruct(xs.shape, xs.dtype),
        scratch_shapes=[pltpu.SemaphoreType.DMA(())],
        input_output_aliases={2: 0},
        compiler_params=_cparams("arbitrary"),
        name="moe_scatter",
    )(dest_flat, hm, xs)


def _w1_split_kernel(w_ref, g_ref, l_ref, t_scr):
    n_slab = w_ref.shape[1] // LANES
    for c in range(n_slab):
        t_scr[c] = w_ref[0, c * LANES:(c + 1) * LANES, :].T
    for c in range(n_slab):
        g_ref[0, :, c * LANES:(c + 1) * LANES] = t_scr[c, pl.ds(0, D_EXPERT, stride=2), :].astype(BF16)
        l_ref[0, :, c * LANES:(c + 1) * LANES] = t_scr[c, pl.ds(1, D_EXPERT, stride=2), :].astype(BF16)


def _w1_split(w1):
    ks = 512
    n_slab = ks // LANES
    out = jax.ShapeDtypeStruct((N_EXPERTS, D_EXPERT, D_MODEL), BF16)
    return pl.pallas_call(
        _w1_split_kernel,
        grid=(N_EXPERTS, D_MODEL // ks),
        in_specs=[pl.BlockSpec((1, ks, 2 * D_EXPERT), lambda e, j: (e, j, 0))],
        out_specs=[pl.BlockSpec((1, D_EXPERT, ks), lambda e, j: (e, 0, j)),
                   pl.BlockSpec((1, D_EXPERT, ks), lambda e, j: (e, 0, j))],
        out_shape=[out, out],
        scratch_shapes=[pltpu.VMEM((n_slab, 2 * D_EXPERT, LANES), F32)],
        compiler_params=_cparams("parallel", "parallel"),
        name="w1_split",
    )(w1)


def _expert_kernel(be_ref, nb_ref, x_ref, w1g_ref, w1l_ref, b1g_ref, b1l_ref, w2_ref, b2_ref, o_ref, w2b_scr):
    i = pl.program_id(0)
    used = i < nb_ref[0]

    @pl.when(jnp.logical_not(used))
    def _():
        o_ref[...] = jnp.zeros_like(o_ref)

    @pl.when(jnp.logical_or(i == 0, be_ref[i] != be_ref[jnp.maximum(i - 1, 0)]))
    def _():
        w2b_scr[...] = w2_ref[0].astype(BF16)

    @pl.when(used)
    def _():
        x = x_ref[...].astype(BF16)
        nt = (((1,), (1,)), ((), ()))
        hg = lax.dot_general(x, w1g_ref[0], nt, preferred_element_type=F32) + b1g_ref[0]
        hl = lax.dot_general(x, w1l_ref[0], nt, preferred_element_type=F32) + b1l_ref[0]
        glu = jnp.minimum(hg, SWIGLU_LIMIT)
        lin = jnp.clip(hl, -SWIGLU_LIMIT, SWIGLU_LIMIT)
        act = glu * jax.nn.sigmoid(SWIGLU_ALPHA * glu) * (lin + 1.0)
        o_ref[...] = jnp.dot(act.astype(BF16), w2b_scr[...], preferred_element_type=F32) + b2_ref[0]


def _experts(blk_exp, n_used, xs, w1g_t, w1l_t, b1g, b1l, w2, b2):
    n_slot = xs.shape[0]
    n_blk = n_slot // MOE_ROWS
    row_in = lambda i, be, nb: (jnp.minimum(i, nb[0] - 1), 0)
    row = lambda i, be, nb: (i, 0)
    wsel = lambda i, be, nb: (be[i], 0, 0)
    grid_spec = pltpu.PrefetchScalarGridSpec(
        num_scalar_prefetch=2,
        grid=(n_blk,),
        in_specs=[
            pl.BlockSpec((MOE_ROWS, D_MODEL), row_in),
            pl.BlockSpec((1, D_EXPERT, D_MODEL), wsel),
            pl.BlockSpec((1, D_EXPERT, D_MODEL), wsel),
            pl.BlockSpec((1, 1, D_EXPERT), wsel),
            pl.BlockSpec((1, 1, D_EXPERT), wsel),
            pl.BlockSpec((1, D_EXPERT, D_MODEL), wsel),
            pl.BlockSpec((1, 1, D_MODEL), wsel),
        ],
        out_specs=pl.BlockSpec((MOE_ROWS, D_MODEL), row),
        scratch_shapes=[pltpu.VMEM((D_EXPERT, D_MODEL), BF16)],
    )
    return pl.pallas_call(
        _expert_kernel,
        grid_spec=grid_spec,
        out_shape=jax.ShapeDtypeStruct((n_slot, D_MODEL), F32),
        compiler_params=_cparams("arbitrary"),
        name="experts",
    )(blk_exp, n_used, xs, w1g_t, w1l_t, b1g, b1l, w2, b2)


def _combine_kernel(dest_ref, dnext_ref, x1_ref, g4_ref, nf_ref, ys_ref, o_ref, buf, sems, *, n):
    tt = x1_ref.shape[0]
    i = pl.program_id(0)

    def issue(d_ref, slot):
        def body(t, carry):
            for k in range(TOP_K):
                _row_copy(ys_ref, d_ref[t * TOP_K + k], buf.at[slot, k], t, sems.at[slot]).start(priority=k % 2)
            return carry

        lax.fori_loop(0, tt, body, 0, unroll=8)

    def finish(slot):
        for k in range(TOP_K):
            pltpu.make_async_copy(ys_ref.at[pl.ds(0, tt)], buf.at[slot, k], sems.at[slot]).wait()
        g4 = g4_ref[...]
        moe = g4[:, 0:1] * buf[slot, 0]
        for k in range(1, TOP_K):
            moe = moe + g4[:, k:k + 1] * buf[slot, k]
        o_ref[...] = _rms(x1_ref[...] + moe, nf_ref[...])

    @pl.when(i == 0)
    def _():
        issue(dest_ref, 0)

    for parity in range(2):
        @pl.when(i % 2 == parity)
        def _():
            @pl.when(i + 1 < n)
            def _():
                issue(dnext_ref, 1 - parity)

            finish(parity)


def _combine(dest_flat, x1, g4, nf, ys):
    n = x1.shape[0]
    tt = _tile(n, 512)
    n_t = n // tt
    row = lambda i: (i, 0)
    return pl.pallas_call(
        functools.partial(_combine_kernel, n=n_t),
        grid=(n_t,),
        in_specs=[
            pl.BlockSpec((tt * TOP_K,), lambda i: (i,), memory_space=pltpu.SMEM),
            pl.BlockSpec((tt * TOP_K,), lambda i: (jnp.minimum(i + 1, n_t - 1),), memory_space=pltpu.SMEM),
            pl.BlockSpec((tt, D_MODEL), row),
            pl.BlockSpec((tt, TOP_K), row),
            pl.BlockSpec((1, D_MODEL), lambda i: (0, 0)),
            pl.BlockSpec(memory_space=pl.ANY),
        ],
        out_specs=pl.BlockSpec((tt, D_MODEL), row),
        out_shape=jax.ShapeDtypeStruct((n, D_MODEL), F32),
        scratch_shapes=[pltpu.VMEM((2, TOP_K, tt, D_MODEL), F32), pltpu.SemaphoreType.DMA((2,))],
        compiler_params=_cparams("arbitrary"),
        name="moe_combine",
    )(dest_flat, dest_flat, x1, g4, nf, ys)


def _rope_table(pos):
    half = MLA_ROPE // 2
    inv_freq = ROPE_THETA ** (-jnp.arange(half, dtype=F32) / half)
    ang = pos.astype(F32)[:, None] * inv_freq[None, :]
    cos, sin = jnp.cos(ang), jnp.sin(ang)
    return jnp.concatenate([cos, cos, -sin, sin], axis=-1)


def _regroup_w_in(w_in):
    o = np.cumsum((Q_LORA, KV_LORA, MLA_ROPE, GLA_DK, GLA_DK, GLA_DV, GLA_GATE_RANK, GLA_DV, 2 * D_MODEL))
    c_q, c_kv, k_r = w_in[:, :o[0]], w_in[:, o[0]:o[1]], w_in[:, o[1]:o[2]]
    gq, gk, gv = w_in[:, o[2]:o[3]], w_in[:, o[3]:o[4]], w_in[:, o[4]:o[5]]
    ga, gg, mg = w_in[:, o[5]:o[6]], w_in[:, o[6]:o[7]], w_in[:, o[7]:o[8]]
    half = MLA_ROPE // 2
    k_r_sw = jnp.concatenate([k_r[:, half:], k_r[:, :half]], axis=1)
    ga_pad = jnp.zeros((D_MODEL, LANES - GLA_GATE_RANK), w_in.dtype)
    return jnp.concatenate([c_q, c_kv, k_r, k_r_sw, ga, ga_pad, gv, gg, gq, gk, mg], axis=1).astype(BF16)


def _regroup_w_uq(w_uq):
    w = w_uq.reshape(Q_LORA, MLA_HEADS, MLA_NOPE + MLA_ROPE)
    half = MLA_ROPE // 2
    nope, rope = w[..., :MLA_NOPE], w[..., MLA_NOPE:]
    rope_sw = jnp.concatenate([rope[..., half:], rope[..., :half]], axis=-1)
    return jnp.concatenate([nope, rope, rope_sw], axis=-1).reshape(Q_LORA, MLA_HEADS * HEAD_PAD).astype(BF16)


def _regroup_w_ukv(w_ukv):
    w = w_ukv.reshape(KV_LORA, MLA_HEADS, MLA_NOPE + MLA_V)
    k = w[..., :MLA_NOPE].reshape(KV_LORA, MLA_HEADS * MLA_NOPE)
    v = w[..., MLA_NOPE:].reshape(KV_LORA, MLA_HEADS * MLA_V)
    return jnp.concatenate([k, v], axis=1).astype(BF16)


def _per_head_w_ukv(w_ukv):
    w = w_ukv.reshape(KV_LORA, MLA_HEADS, MLA_NOPE + MLA_V)
    w_uk = jnp.transpose(w[..., :MLA_NOPE], (1, 2, 0)).astype(BF16)
    w_uv = jnp.transpose(w[..., MLA_NOPE:], (1, 0, 2)).astype(BF16)
    return w_uk, w_uv


def _slot_plan(counts_first, counts_rest, n_blk):
    counts_first = counts_first.astype(jnp.int32)
    counts = counts_first + counts_rest.astype(jnp.int32)
    padded = (counts + MOE_ROWS - 1) // MOE_ROWS * MOE_ROWS
    pad_end = jnp.cumsum(padded)
    pad_start = pad_end - padded
    n_used = (pad_end[-1] // MOE_ROWS).reshape(1)
    blk_start = jnp.arange(n_blk, dtype=jnp.int32) * MOE_ROWS
    blk_exp = jnp.sum(blk_start[:, None] >= pad_end[None, :], axis=1).astype(jnp.int32)
    last_exp = blk_exp[jnp.maximum(n_used[0] - 1, 0)]
    blk_exp = jnp.where(jnp.arange(n_blk) < n_used[0], blk_exp, last_exp)
    ps_row = jnp.pad(pad_start.astype(F32), (0, LANES - N_EXPERTS))[None, :]
    zero_first = jnp.concatenate([(pad_start + counts_first) // MOE_ROWS, n_used]).astype(jnp.int32)
    zero_last = jnp.concatenate([pad_end // MOE_ROWS, jnp.full((1,), n_blk)]).astype(jnp.int32)
    return ps_row, zero_first, zero_last, blk_exp, n_used


def _mixers(x, past_lat, past_kr, s0, past_len, p):
    bsz, seq, _ = x.shape
    n = bsz * seq
    x2d = x.reshape(n, D_MODEL)
    hin = _in_proj(x2d, p["norm1_g"], p["w_all"])
    tm = min(512, n)
    cs = _rope_table(past_len + jnp.arange(seq))
    if seq < tm:
        cs = jnp.tile(cs, (tm // seq, 1))
    q, lat_new, krp_new = _mla_q(hin, cs, p["q_norm_g"], p["kv_norm_g"], p["w_uq"])
    if past_len:
        attn = _mla_cached(q, past_lat, past_kr, lat_new, krp_new, p["w_uk"], p["w_uv"], bsz, seq)
    else:
        kcat, v = _kv_up(lat_new, krp_new, p["w_ukv"])
        tq = min(1024, seq)
        tk = 1024 if seq % 1024 == 0 else seq
        attn = _flash(q.reshape(bsz, seq, -1), kcat.reshape(bsz, seq, -1), v.reshape(bsz, seq, -1), 0, tq, tk)
    og, s_new = _gla(hin, p["w_alpha"], p["b_alpha"], p["gla_norm_g"], s0, bsz, seq)
    x1, hm, sel, g4, cnt = _post_mix(attn.reshape(n, -1), og, hin, x2d, p["w_o_mla"], p["w_o_gla"], p["w_out"],
                                     p["norm2_g"], p["w_router"], p["b_router"])
    new_state = (lat_new.reshape(bsz, seq, KV_LORA), krp_new[:, :MLA_ROPE].reshape(bsz, seq, MLA_ROPE), s_new)
    return x1, hm, sel, g4, cnt, new_state


def kernel(x_prompt, x_sample, cache_mla_latent, cache_mla_krope, state_gla, norm1_g, w_in, q_norm_g, kv_norm_g,
           w_uq, w_ukv, w_o_mla, w_alpha2, b_alpha, gla_norm_g, w_o_gla, w_out, norm2_g, w_router, b_router,
           w1, b1, w2, b2, normf_g):
    depth = w_in.shape[0]
    assert depth == 1
    l = 0
    bp, sp, _ = x_prompt.shape
    bs, ss, _ = x_sample.shape
    past_len = cache_mla_latent.shape[2]
    p = {
        "norm1_g": norm1_g[l][None, :],
        "w_all": _regroup_w_in(w_in[l]),
        "q_norm_g": q_norm_g[l][None, :],
        "kv_norm_g": kv_norm_g[l][None, :],
        "w_uq": _regroup_w_uq(w_uq[l]),
        "w_ukv": _regroup_w_ukv(w_ukv[l]),
        "w_uk": _per_head_w_ukv(w_ukv[l])[0],
        "w_uv": _per_head_w_ukv(w_ukv[l])[1],
        "w_o_mla": w_o_mla[l].astype(BF16),
        "w_alpha": jnp.pad(w_alpha2[l], ((0, LANES - GLA_GATE_RANK), (0, 0))).astype(BF16),
        "b_alpha": b_alpha[l][None, :],
        "gla_norm_g": gla_norm_g[l][None, :],
        "w_o_gla": w_o_gla[l].astype(BF16),
        "w_out": w_out[l].astype(BF16),
        "norm2_g": norm2_g[l][None, :],
        "w_router": jnp.pad(w_router[l], ((0, 0), (0, LANES - N_EXPERTS))).astype(BF16),
        "b_router": jnp.pad(b_router[l], (0, LANES - N_EXPERTS))[None, :],
    }
    zeros_state = jnp.zeros((bp, GLA_HEADS, GLA_DK_HEAD, GLA_DV_HEAD), F32)
    x1p, hmp, selp, g4p, cntp, (lat_p, kr_p, st_p) = _mixers(x_prompt, None, None, zeros_state, 0, p)
    x1s, hms, sels, g4s, cnts, (lat_s, kr_s, st_s) = _mixers(x_sample, cache_mla_latent[l], cache_mla_krope[l],
                                                              state_gla[l], past_len, p)

    n_p, n_s = bp * sp, bs * ss
    n_asg = (n_p + n_s) * TOP_K
    n_slot = -(-n_asg // MOE_ROWS) * MOE_ROWS + N_EXPERTS * MOE_ROWS
    ps_row, zero_first, zero_last, blk_exp, n_used = _slot_plan(cntp[0, :N_EXPERTS], cnts[0, :N_EXPERTS],
                                                                n_slot // MOE_ROWS)
    dest_p = _dest(selp, ps_row).reshape(n_p * TOP_K)
    dest_s = _dest(sels, ps_row + cntp).reshape(n_s * TOP_K)
    xs = _scatter_init(zero_first, zero_last, dest_p, hmp, n_slot)
    xs = _scatter_more(dest_s, hms, xs)
    w1g_t, w1l_t = _w1_split(w1[l])
    b1g = b1[l][:, None, 0::2]
    b1lin = b1[l][:, None, 1::2]
    ys_slots = _experts(blk_exp, n_used, xs, w1g_t, w1l_t, b1g, b1lin, w2[l], b2[l][:, None, :])
    yp = _combine(dest_p, x1p, g4p, normf_g[None, :], ys_slots).reshape(bp, sp, D_MODEL)
    ys = _combine(dest_s, x1s, g4s, normf_g[None, :], ys_slots).reshape(bs, ss, D_MODEL)
    return (yp, ys, lat_p[None], kr_p[None], st_p[None], lat_s[None], kr_s[None], st_s[None])
```

```python
import functools

import jax
import jax.numpy as jnp
import numpy as np
from jax import lax
from jax.experimental import pallas as pl
from jax.experimental.pallas import tpu as pltpu

F32 = jnp.float32
BF16 = jnp.bfloat16

D_MODEL = 1024
CHUNK = 64
CHUNK_SHIFT = 6
LOG2_E = 1.4426950408889634
NORM_EPS = 1e-6
MLA_HEADS = 8
MLA_NOPE = 128
MLA_ROPE = 64
MLA_V = 128
Q_LORA = 512
KV_LORA = 256
ROPE_THETA = 10000.0
MLA_SCALE = (MLA_NOPE + MLA_ROPE) ** -0.5
NEG_INF = -1e30
GLA_HEADS = 4
GLA_DK = D_MODEL // 2
GLA_DV = D_MODEL
GLA_DK_HEAD = GLA_DK // GLA_HEADS
GLA_DV_HEAD = GLA_DV // GLA_HEADS
GLA_GATE_RANK = 16
GLA_GATE_TEMP = 16.0
GLA_SUB = 16
FLASH_HEADS = 2
POST_MIX_PARTS = 2
N_EXPERTS = 32
TOP_K = 4
D_EXPERT = D_MODEL
SWIGLU_LIMIT = 7.0
SWIGLU_ALPHA = 1.702
MOE_ROWS = 512

LANES = 128
HEAD_PAD = 256

COL_CQ = 0
COL_CKV = 512
COL_KR = 768
COL_GA = 896
COL_GV = 1024
COL_GG = 2048
COL_GQ = 3072
COL_GK = 3584
COL_MG = 4096
D_IN_PAD = 6144

V7X_VMEM_BYTES = 64 * 1024 * 1024
VMEM_LIMIT = V7X_VMEM_BYTES - 8 * 1024 * 1024


def _cparams(*sem):
    return pltpu.CompilerParams(dimension_semantics=sem, vmem_limit_bytes=VMEM_LIMIT)


def _tile(n, pref, mult=16):
    t = min(pref, n)
    t -= t % mult
    while n % t:
        t -= mult
    return t


def _rms(x, g):
    return x * lax.rsqrt(jnp.mean(x * x, axis=-1, keepdims=True) + NORM_EPS) * g


def _in_proj_kernel(x_ref, g_ref, w_ref, o_ref):
    h = _rms(x_ref[...], g_ref[...]).astype(BF16)
    tn = 1536
    for j in range(D_IN_PAD // tn):
        o_ref[:, j * tn:(j + 1) * tn] = jnp.dot(h, w_ref[:, j * tn:(j + 1) * tn], preferred_element_type=F32)


def _in_proj(x2d, g, w_all):
    n = x2d.shape[0]
    tm = min(256, n)
    return pl.pallas_call(
        _in_proj_kernel,
        grid=(n // tm,),
        in_specs=[
            pl.BlockSpec((tm, D_MODEL), lambda i: (i, 0)),
            pl.BlockSpec((1, D_MODEL), lambda i: (0, 0)),
            pl.BlockSpec((D_MODEL, D_IN_PAD), lambda i: (0, 0)),
        ],
        out_specs=pl.BlockSpec((tm, D_IN_PAD), lambda i: (i, 0)),
        out_shape=jax.ShapeDtypeStruct((n, D_IN_PAD), F32),
        compiler_params=_cparams("parallel"),
        name="in_proj",
    )(x2d, g, w_all)


def _rope_pair(blk, cs):
    t = blk * cs
    return t + pltpu.roll(t, MLA_ROPE, axis=1)


def _mla_q_kernel(h_ref, cs_ref, qg_ref, kvg_ref, wuq_ref, q_ref, lat_ref, kr_ref):
    cs = cs_ref[...]
    qn = _rms(h_ref[:, COL_CQ:COL_CQ + Q_LORA], qg_ref[...]).astype(BF16)
    q = jnp.dot(qn, wuq_ref[...], preferred_element_type=F32)
    for hh in range(MLA_HEADS):
        c0 = hh * HEAD_PAD
        q_ref[:, c0:c0 + MLA_NOPE] = q[:, c0:c0 + MLA_NOPE].astype(BF16)
        q_ref[:, c0 + MLA_NOPE:c0 + HEAD_PAD] = _rope_pair(q[:, c0 + MLA_NOPE:c0 + HEAD_PAD], cs).astype(BF16)
    lat_ref[...] = _rms(h_ref[:, COL_CKV:COL_CKV + KV_LORA], kvg_ref[...])
    r = _rope_pair(h_ref[:, COL_KR:COL_KR + LANES], cs)
    lane = lax.broadcasted_iota(jnp.int32, r.shape, 1)
    kr_ref[...] = jnp.where(lane < MLA_ROPE, r, 0.0)


def _mla_qkv_kernel(h_ref, cs_ref, qg_ref, kvg_ref, wuq_ref, wukv_ref, q_ref, lat_ref, kr_ref, k_ref, v_ref):
    _mla_q_kernel(h_ref, cs_ref, qg_ref, kvg_ref, wuq_ref, q_ref, lat_ref, kr_ref)
    _kv_up_kernel(lat_ref, kr_ref, wukv_ref, k_ref, v_ref)


def _mla_qkv(hin, cs, qg, kvg, wuq, wukv):
    n = hin.shape[0]
    tm = min(512, n)
    n_cs = cs.shape[0] // tm
    row = lambda i: (i, 0)
    const = lambda i: (0, 0)
    return pl.pallas_call(
        _mla_qkv_kernel,
        grid=(n // tm,),
        in_specs=[
            pl.BlockSpec((tm, COL_GA), row),
            pl.BlockSpec((tm, LANES), lambda i: (i % n_cs, 0)),
            pl.BlockSpec((1, Q_LORA), const),
            pl.BlockSpec((1, KV_LORA), const),
            pl.BlockSpec((Q_LORA, MLA_HEADS * HEAD_PAD), const),
            pl.BlockSpec((KV_LORA, MLA_HEADS * (MLA_NOPE + MLA_V)), const),
        ],
        out_specs=[
            pl.BlockSpec((tm, MLA_HEADS * HEAD_PAD), row),
            pl.BlockSpec((tm, KV_LORA), row),
            pl.BlockSpec((tm, LANES), row),
            pl.BlockSpec((tm, MLA_HEADS * HEAD_PAD), row),
            pl.BlockSpec((tm, MLA_HEADS * MLA_V), row),
        ],
        out_shape=[
            jax.ShapeDtypeStruct((n, MLA_HEADS * HEAD_PAD), BF16),
            jax.ShapeDtypeStruct((n, KV_LORA), F32),
            jax.ShapeDtypeStruct((n, LANES), F32),
            jax.ShapeDtypeStruct((n, MLA_HEADS * HEAD_PAD), BF16),
            jax.ShapeDtypeStruct((n, MLA_HEADS * MLA_V), BF16),
        ],
        compiler_params=_cparams("parallel"),
        name="mla_qkv",
    )(hin, cs, qg, kvg, wuq, wukv)


def _mla_q(hin, cs, qg, kvg, wuq):
    n = hin.shape[0]
    tm = min(512, n)
    n_cs = cs.shape[0] // tm
    return pl.pallas_call(
        _mla_q_kernel,
        grid=(n // tm,),
        in_specs=[
            pl.BlockSpec((tm, COL_GA), lambda i: (i, 0)),
            pl.BlockSpec((tm, LANES), lambda i: (i % n_cs, 0)),
            pl.BlockSpec((1, Q_LORA), lambda i: (0, 0)),
            pl.BlockSpec((1, KV_LORA), lambda i: (0, 0)),
            pl.BlockSpec((Q_LORA, MLA_HEADS * HEAD_PAD), lambda i: (0, 0)),
        ],
        out_specs=[
            pl.BlockSpec((tm, MLA_HEADS * HEAD_PAD), lambda i: (i, 0)),
            pl.BlockSpec((tm, KV_LORA), lambda i: (i, 0)),
            pl.BlockSpec((tm, LANES), lambda i: (i, 0)),
        ],
        out_shape=[
            jax.ShapeDtypeStruct((n, MLA_HEADS * HEAD_PAD), BF16),
            jax.ShapeDtypeStruct((n, KV_LORA), F32),
            jax.ShapeDtypeStruct((n, LANES), F32),
        ],
        compiler_params=_cparams("parallel"),
        name="mla_q",
    )(hin, cs, qg, kvg, wuq)


def _kv_up_kernel(lat_ref, kr_ref, w_ref, k_ref, v_ref):
    kv = jnp.dot(lat_ref[...].astype(BF16), w_ref[...], preferred_element_type=F32)
    krb = kr_ref[...].astype(BF16)
    for hh in range(MLA_HEADS):
        c0 = hh * HEAD_PAD
        k_ref[:, c0:c0 + MLA_NOPE] = kv[:, hh * MLA_NOPE:(hh + 1) * MLA_NOPE].astype(BF16)
        k_ref[:, c0 + MLA_NOPE:c0 + HEAD_PAD] = krb
    v_ref[...] = kv[:, MLA_HEADS * MLA_NOPE:].astype(BF16)


def _kv_up(lat, krp, wukv):
    n = lat.shape[0]
    tm = _tile(n, 512)
    return pl.pallas_call(
        _kv_up_kernel,
        grid=(n // tm,),
        in_specs=[
            pl.BlockSpec((tm, KV_LORA), lambda i: (i, 0)),
            pl.BlockSpec((tm, LANES), lambda i: (i, 0)),
            pl.BlockSpec((KV_LORA, MLA_HEADS * (MLA_NOPE + MLA_V)), lambda i: (0, 0)),
        ],
        out_specs=[
            pl.BlockSpec((tm, MLA_HEADS * HEAD_PAD), lambda i: (i, 0)),
            pl.BlockSpec((tm, MLA_HEADS * MLA_V), lambda i: (i, 0)),
        ],
        out_shape=[
            jax.ShapeDtypeStruct((n, MLA_HEADS * HEAD_PAD), BF16),
            jax.ShapeDtypeStruct((n, MLA_HEADS * MLA_V), BF16),
        ],
        compiler_params=_cparams("parallel"),
        name="kv_up",
    )(lat, krp, wukv)


def _flash_kernel(q_ref, k_ref, v_ref, o_ref, *, tq, tk, past_len):
    qi = pl.program_id(2)
    q_pos0 = past_len + qi * tq
    q_chunk = (q_pos0 + lax.broadcasted_iota(jnp.int32, (tq, 1), 0)) >> CHUNK_SHIFT
    n_blocks = (q_pos0 + tq + tk - 1) // tk
    n_full = ((q_pos0 >> CHUNK_SHIFT) << CHUNK_SHIFT) // tk
    qs = [q_ref[0, :, hh * HEAD_PAD:(hh + 1) * HEAD_PAD] for hh in range(FLASH_HEADS)]

    def step(j, carry, masked):
        k0 = pl.multiple_of(j * tk, tk)
        if masked:
            k_chunk = (k0 + lax.broadcasted_iota(jnp.int32, (1, tk), 1)) >> CHUNK_SHIFT
        out = []
        for hh in range(FLASH_HEADS):
            m, l, acc = carry[hh]
            k = k_ref[0, pl.ds(k0, tk), hh * HEAD_PAD:(hh + 1) * HEAD_PAD]
            v = v_ref[0, pl.ds(k0, tk), hh * MLA_V:(hh + 1) * MLA_V]
            s = lax.dot_general(qs[hh], k, (((1,), (1,)), ((), ())), preferred_element_type=F32)
            s = s * (MLA_SCALE * LOG2_E)
            if masked:
                s = jnp.where(k_chunk <= q_chunk, s, NEG_INF)
            m_new = jnp.maximum(m, jnp.max(s, axis=-1, keepdims=True))
            alpha = jnp.exp2(m - m_new)
            p = jnp.exp2(s - m_new)
            l = alpha * l + jnp.sum(p, axis=-1, keepdims=True)
            acc = alpha * acc + jnp.dot(p.astype(BF16), v, preferred_element_type=F32)
            out.append((m_new, l, acc))
        return tuple(out)

    init = tuple((jnp.full((tq, 1), NEG_INF, F32), jnp.zeros((tq, 1), F32), jnp.zeros((tq, MLA_V), F32))
                 for _ in range(FLASH_HEADS))
    carry = lax.fori_loop(0, n_full, functools.partial(step, masked=False), init)
    carry = lax.fori_loop(n_full, n_blocks, functools.partial(step, masked=True), carry)
    for hh in range(FLASH_HEADS):
        _, l, acc = carry[hh]
        o_ref[0, :, hh * MLA_V:(hh + 1) * MLA_V] = (acc / l).astype(BF16)


def _flash(q, k, v, past_len, tq, tk):
    b, sq, _ = q.shape
    sk = k.shape[1]
    fh = FLASH_HEADS
    return pl.pallas_call(
        functools.partial(_flash_kernel, tq=tq, tk=tk, past_len=past_len),
        grid=(b, MLA_HEADS // fh, sq // tq),
        in_specs=[
            pl.BlockSpec((1, tq, fh * HEAD_PAD), lambda bi, h, i: (bi, i, h)),
            pl.BlockSpec((1, sk, fh * HEAD_PAD), lambda bi, h, i: (bi, 0, h)),
            pl.BlockSpec((1, sk, fh * MLA_V), lambda bi, h, i: (bi, 0, h)),
        ],
        out_specs=pl.BlockSpec((1, tq, fh * MLA_V), lambda bi, h, i: (bi, i, h)),
        out_shape=jax.ShapeDtypeStruct((b, sq, MLA_HEADS * MLA_V), BF16),
        compiler_params=_cparams("parallel", "parallel", "arbitrary"),
        name="flash",
    )(q, k, v)


def _mla_cached_kernel(q_ref, clat_ref, ckr_ref, nlat_ref, nkr_ref, wuk_ref, wuv_ref, o_ref, *, past_len):
    seq = q_ref.shape[0]
    nt = (((1,), (1,)), ((), ()))
    q_abs = jnp.concatenate(
        [jnp.dot(q_ref[:, hh * HEAD_PAD:hh * HEAD_PAD + MLA_NOPE], wuk_ref[hh], preferred_element_type=F32)
         for hh in range(MLA_HEADS)], axis=0).astype(BF16)
    q_rot = jnp.concatenate([q_ref[:, hh * HEAD_PAD + MLA_NOPE:(hh + 1) * HEAD_PAD] for hh in range(MLA_HEADS)],
                            axis=0)
    pos_q = past_len + lax.broadcasted_iota(jnp.int32, (seq, 1), 0)
    q_chunk = jnp.concatenate([pos_q] * MLA_HEADS, axis=0) >> CHUNK_SHIFT

    def scores(lat, kr, k_pos0):
        n_k = lat.shape[0]
        kr_pad = jnp.concatenate([kr, jnp.zeros_like(kr)], axis=1)
        s = (lax.dot_general(q_abs, lat, nt, preferred_element_type=F32)
             + lax.dot_general(q_rot, kr_pad, nt, preferred_element_type=F32)) * (MLA_SCALE * LOG2_E)
        k_chunk = (k_pos0 + lax.broadcasted_iota(jnp.int32, (1, n_k), 1)) >> CHUNK_SHIFT
        return jnp.where(k_chunk <= q_chunk, s, NEG_INF)

    lat_c = clat_ref[0].astype(BF16)
    lat_n = nlat_ref[...].astype(BF16)
    s_c = scores(lat_c, ckr_ref[0].astype(BF16), 0)
    s_n = scores(lat_n, nkr_ref[:, :MLA_ROPE].astype(BF16), past_len)
    m = jnp.maximum(jnp.max(s_c, axis=-1, keepdims=True), jnp.max(s_n, axis=-1, keepdims=True))
    p_c = jnp.exp2(s_c - m)
    p_n = jnp.exp2(s_n - m)
    l = jnp.sum(p_c, axis=-1, keepdims=True) + jnp.sum(p_n, axis=-1, keepdims=True)
    o_lat = (jnp.dot(p_c.astype(BF16), lat_c, preferred_element_type=F32)
             + jnp.dot(p_n.astype(BF16), lat_n, preferred_element_type=F32)) / l
    o_lat = o_lat.astype(BF16)
    for hh in range(MLA_HEADS):
        o_ref[:, hh * MLA_V:(hh + 1) * MLA_V] = jnp.dot(o_lat[hh * seq:(hh + 1) * seq], wuv_ref[hh],
                                                         preferred_element_type=F32).astype(BF16)


def _mla_cached(q, cache_lat, cache_kr, lat_new, krp_new, w_uk, w_uv, bsz, seq):
    past_len = cache_lat.shape[1]
    return pl.pallas_call(
        functools.partial(_mla_cached_kernel, past_len=past_len),
        grid=(bsz,),
        in_specs=[
            pl.BlockSpec((seq, MLA_HEADS * HEAD_PAD), lambda b: (b, 0)),
            pl.BlockSpec((1, past_len, KV_LORA), lambda b: (b, 0, 0)),
            pl.BlockSpec((1, past_len, MLA_ROPE), lambda b: (b, 0, 0)),
            pl.BlockSpec((seq, KV_LORA), lambda b: (b, 0)),
            pl.BlockSpec((seq, LANES), lambda b: (b, 0)),
            pl.BlockSpec((MLA_HEADS, MLA_NOPE, KV_LORA), lambda b: (0, 0, 0)),
            pl.BlockSpec((MLA_HEADS, KV_LORA, MLA_V), lambda b: (0, 0, 0)),
        ],
        out_specs=pl.BlockSpec((seq, MLA_HEADS * MLA_V), lambda b: (b, 0)),
        out_shape=jax.ShapeDtypeStruct((bsz * seq, MLA_HEADS * MLA_V), BF16),
        compiler_params=_cparams("parallel"),
        name="mla_cached",
    )(q, cache_lat, cache_kr, lat_new, krp_new, w_uk, w_uv)


def _cumsum_rows(x):
    n = x.shape[0]
    row = lax.broadcasted_iota(jnp.int32, x.shape, 0)
    s = 1
    while s < n:
        x = x + jnp.where(row >= s, pltpu.roll(x, s, axis=0), 0.0)
        s *= 2
    return x


def _gla_kernel(gv_ref, gg_ref, gq_ref, gk_ref, ga_ref, wa_ref, ba_ref, ng_ref, s0_ref, og_ref, sn_ref, st_scr,
                *, csize, n_chunks):
    t = pl.program_id(1)

    @pl.when(t == 0)
    def _():
        for hh in range(GLA_HEADS):
            st_scr[hh] = s0_ref[0, hh].T

    wa = wa_ref[...]
    ba = ba_ref[...]
    ng = ng_ref[...]
    n_sub = csize // GLA_SUB
    col_id = lax.broadcasted_iota(jnp.int32, (GLA_SUB, csize), 1)
    row_in_sub = lax.broadcasted_iota(jnp.int32, (GLA_SUB, csize), 0)

    def chunk(c, carry):
        r0 = pl.multiple_of(c * csize, csize)
        rows = pl.ds(r0, csize)
        x = jnp.dot(ga_ref[rows, :].astype(BF16), wa, preferred_element_type=F32) + ba
        la = jax.nn.log_sigmoid(x) / GLA_GATE_TEMP
        b_all = _cumsum_rows(la) * LOG2_E
        for hh in range(GLA_HEADS):
            head(hh, rows, b_all[:, hh * GLA_DK_HEAD:(hh + 1) * GLA_DK_HEAD])
        return carry

    def head(hh, rows, b):
        kcols = slice(hh * GLA_DK_HEAD, (hh + 1) * GLA_DK_HEAD)
        vcols = slice(hh * GLA_DV_HEAD, (hh + 1) * GLA_DV_HEAD)
        q = gq_ref[rows, kcols] * (GLA_DK_HEAD ** -0.5)
        k = gk_ref[rows, kcols]
        v = gv_ref[rows, vcols]

        a_rows = []
        for i in range(n_sub):
            lo = i * GLA_SUB
            bi = b[lo:lo + GLA_SUB]
            qi = q[lo:lo + GLA_SUB]
            ki = k[lo:lo + GLA_SUB]
            if i == 0:
                a_i = jnp.zeros((GLA_SUB, csize), F32)
            else:
                ri = b[lo - 1:lo]
                qs = qi * jnp.exp2(bi - ri)
                ks = k * jnp.exp2(jnp.minimum(ri - b, 0.0))
                a_i = lax.dot_general(qs.astype(BF16), ks.astype(BF16), (((1,), (1,)), ((), ())),
                                      preferred_element_type=F32)
            for s in range(GLA_SUB):
                col = jnp.sum(qi * ki[s:s + 1] * jnp.exp2(bi - bi[s:s + 1]), axis=-1, keepdims=True)
                a_i = jnp.where(col_id == lo + s, col, a_i)
            a_rows.append(jnp.where(col_id <= lo + row_in_sub, a_i, 0.0))
        a = jnp.concatenate(a_rows, axis=0)

        st = st_scr[hh]
        vb = v.astype(BF16)
        o = jnp.dot(a.astype(BF16), vb, preferred_element_type=F32)
        o = o + lax.dot_general((q * jnp.exp2(b)).astype(BF16), st.astype(BF16), (((1,), (1,)), ((), ())),
                                preferred_element_type=F32)
        b_end = b[csize - 1:csize]
        kd = (k * jnp.exp2(b_end - b)).astype(BF16)
        st_scr[hh] = jnp.exp2(b_end) * st + jnp.dot(v.T.astype(BF16), kd, preferred_element_type=F32)
        on = _rms(o, ng)
        og_ref[rows, vcols] = (on * jax.nn.silu(gg_ref[rows, vcols])).astype(BF16)

    lax.fori_loop(0, n_chunks, chunk, 0, unroll=2 if n_chunks % 2 == 0 else 1)

    @pl.when(t == pl.num_programs(1) - 1)
    def _():
        for hh in range(GLA_HEADS):
            sn_ref[0, hh] = st_scr[hh].T


def _gla(hin, wa, ba, ng, s0, bsz, seq):
    csize = min(CHUNK, seq)
    tt = min(1024, seq)
    n_t = seq // tt
    n = bsz * seq
    kern = functools.partial(_gla_kernel, csize=csize, n_chunks=tt // csize)
    state = pl.BlockSpec((1, GLA_HEADS, GLA_DK_HEAD, GLA_DV_HEAD), lambda b, t: (b, 0, 0, 0))
    return pl.pallas_call(
        kern,
        grid=(bsz, n_t),
        in_specs=[
            pl.BlockSpec((tt, GLA_DV), lambda b, t: (b * n_t + t, COL_GV // GLA_DV)),
            pl.BlockSpec((tt, GLA_DV), lambda b, t: (b * n_t + t, COL_GG // GLA_DV)),
            pl.BlockSpec((tt, GLA_DK), lambda b, t: (b * n_t + t, COL_GQ // GLA_DK)),
            pl.BlockSpec((tt, GLA_DK), lambda b, t: (b * n_t + t, COL_GK // GLA_DK)),
            pl.BlockSpec((tt, LANES), lambda b, t: (b * n_t + t, COL_GA // LANES)),
            pl.BlockSpec((LANES, GLA_DK), lambda b, t: (0, 0)),
            pl.BlockSpec((1, GLA_DK), lambda b, t: (0, 0)),
            pl.BlockSpec((1, GLA_DV_HEAD), lambda b, t: (0, 0)),
            state,
        ],
        out_specs=[pl.BlockSpec((tt, GLA_DV), lambda b, t: (b * n_t + t, 0)), state],
        out_shape=[
            jax.ShapeDtypeStruct((n, GLA_DV), BF16),
            jax.ShapeDtypeStruct((bsz, GLA_HEADS, GLA_DK_HEAD, GLA_DV_HEAD), F32),
        ],
        scratch_shapes=[pltpu.VMEM((GLA_HEADS, GLA_DV_HEAD, GLA_DK_HEAD), F32)],
        compiler_params=_cparams("parallel", "arbitrary"),
        name="gla",
    )(hin, hin, hin, hin, hin, wa, ba, ng, s0)


def _post_mix_kernel(at_ref, og_ref, mg1_ref, mg2_ref, x_ref, wom_ref, wog_ref, wout_ref, n2_ref, wr_ref, br_ref,
                     x1_ref, hm_ref, sel_ref, g4_ref, cnt_ref):
    @pl.when(pl.program_id(0) == 0)
    def _():
        cnt_ref[...] = jnp.zeros_like(cnt_ref)

    tm = x_ref.shape[0]
    half = tm // POST_MIX_PARTS
    for part in range(POST_MIX_PARTS):
        rows = slice(part * half, (part + 1) * half)
        y_mla = jnp.dot(at_ref[rows, :], wom_ref[...], preferred_element_type=F32)
        y_gla = jnp.dot(og_ref[rows, :], wog_ref[...], preferred_element_type=F32)
        m = jax.nn.sigmoid(mg1_ref[rows, :]) * y_mla + jax.nn.sigmoid(mg2_ref[rows, :]) * y_gla
        x1 = x_ref[rows, :] + jnp.dot(m.astype(BF16), wout_ref[...], preferred_element_type=F32)
        x1_ref[rows, :] = x1
        hm = _rms(x1, n2_ref[...])
        hm_ref[rows, :] = hm
        logits = jnp.dot(hm.astype(BF16), wr_ref[...], preferred_element_type=F32) + br_ref[...]

        lane = lax.broadcasted_iota(jnp.int32, logits.shape, 1)
        work = jnp.where(lane < N_EXPERTS, logits, -jnp.inf)
        sel = jnp.zeros(logits.shape, F32)
        vals = []
        for k in range(TOP_K):
            mk = jnp.max(work, axis=-1, keepdims=True)
            ik = jnp.min(jnp.where(work == mk, lane, LANES), axis=-1, keepdims=True)
            hit = lane == ik
            sel = jnp.where(hit, float(k + 1), sel)
            work = jnp.where(hit, -jnp.inf, work)
            vals.append(mk)
        sel_ref[rows, :] = sel
        ex = [jnp.exp(v - vals[0]) for v in vals]
        den = ex[0]
        for e in ex[1:]:
            den = den + e
        g4 = jnp.zeros(logits.shape, F32)
        for k in range(TOP_K):
            g4 = jnp.where(lane == k, ex[k] / den, g4)
        g4_ref[rows, :] = g4[:, :TOP_K]
        cnt_ref[...] += jnp.sum(jnp.where(sel > 0.0, 1.0, 0.0), axis=0, keepdims=True)


def _post_mix(attn, og, hin, x2d, wom, wog, wout, n2, wr, br):
    n = x2d.shape[0]
    tm = min(256 * POST_MIX_PARTS, n)
    row = lambda i: (i, 0)
    const = lambda i: (0, 0)
    return pl.pallas_call(
        _post_mix_kernel,
        grid=(n // tm,),
        in_specs=[
            pl.BlockSpec((tm, D_MODEL), row),
            pl.BlockSpec((tm, D_MODEL), row),
            pl.BlockSpec((tm, D_MODEL), lambda i: (i, COL_MG // D_MODEL)),
            pl.BlockSpec((tm, D_MODEL), lambda i: (i, COL_MG // D_MODEL + 1)),
            pl.BlockSpec((tm, D_MODEL), row),
            pl.BlockSpec((D_MODEL, D_MODEL), const),
            pl.BlockSpec((D_MODEL, D_MODEL), const),
            pl.BlockSpec((D_MODEL, D_MODEL), const),
            pl.BlockSpec((1, D_MODEL), const),
            pl.BlockSpec((D_MODEL, LANES), const),
            pl.BlockSpec((1, LANES), const),
        ],
        out_specs=[
            pl.BlockSpec((tm, D_MODEL), row),
            pl.BlockSpec((tm, D_MODEL), row),
            pl.BlockSpec((tm, LANES), row),
            pl.BlockSpec((tm, TOP_K), row),
            pl.BlockSpec((1, LANES), const),
        ],
        out_shape=[
            jax.ShapeDtypeStruct((n, D_MODEL), F32),
            jax.ShapeDtypeStruct((n, D_MODEL), F32),
            jax.ShapeDtypeStruct((n, LANES), F32),
            jax.ShapeDtypeStruct((n, TOP_K), F32),
            jax.ShapeDtypeStruct((1, LANES), F32),
        ],
        compiler_params=_cparams("arbitrary"),
        name="post_mix",
    )(attn, og, hin, hin, x2d, wom, wog, wout, n2, wr, br)


def _dest_kernel(sel_ref, ps_ref, dest_ref, run_scr):
    @pl.when(pl.program_id(0) == 0)
    def _():
        run_scr[...] = ps_ref[...]

    sel = sel_ref[...]
    tt = sel.shape[0]
    oh = jnp.where(sel > 0.0, 1.0, 0.0)
    r = lax.broadcasted_iota(jnp.int32, (tt, tt), 0)
    c = lax.broadcasted_iota(jnp.int32, (tt, tt), 1)
    ltri = jnp.where(r > c, 1.0, 0.0).astype(BF16)
    slot = jnp.dot(ltri, oh.astype(BF16), preferred_element_type=F32) + run_scr[...]
    run_scr[...] += jnp.sum(oh, axis=0, keepdims=True)
    lane = lax.broadcasted_iota(jnp.int32, sel.shape, 1)
    d = jnp.zeros(sel.shape, F32)
    for k in range(TOP_K):
        col = jnp.sum(jnp.where(sel == float(k + 1), slot, 0.0), axis=-1, keepdims=True)
        d = jnp.where(lane == k, col, d)
    dest_ref[...] = d[:, :TOP_K].astype(jnp.int32)


def _dest(sel, pad_start):
    n = sel.shape[0]
    tt = _tile(n, 512)
    return pl.pallas_call(
        _dest_kernel,
        grid=(n // tt,),
        in_specs=[pl.BlockSpec((tt, LANES), lambda i: (i, 0)), pl.BlockSpec((1, LANES), lambda i: (0, 0))],
        out_specs=pl.BlockSpec((tt, TOP_K), lambda i: (i, 0)),
        out_shape=jax.ShapeDtypeStruct((n, TOP_K), jnp.int32),
        scratch_shapes=[pltpu.VMEM((1, LANES), F32)],
        compiler_params=_cparams("arbitrary"),
        name="route_dest",
    )(sel, pad_start)


def _row_copy(src, src_row, dst, dst_row, sem):
    return pltpu.make_async_copy(src.at[pl.ds(src_row, 1)], dst.at[pl.ds(dst_row, 1)], sem)


def _scatter_rows(dest_ref, hm_ref, xs_ref, sem):
    tt = hm_ref.shape[0]

    def issue(t, carry):
        for k in range(TOP_K):
            _row_copy(hm_ref, t, xs_ref, dest_ref[t * TOP_K + k], sem).start()
        return carry

    lax.fori_loop(0, tt, issue, 0)
    for k in range(TOP_K):
        pltpu.make_async_copy(hm_ref, xs_ref.at[pl.ds(0, tt)], sem).wait()


def _scatter_init_kernel(zf_ref, zl_ref, dest_ref, hm_hbm, xs_ref, hbuf, zero_scr, lsems, ssems, zsem, *, n, tt):
    i = pl.program_id(0)

    def load(b, slot):
        return pltpu.make_async_copy(hm_hbm.at[pl.ds(pl.multiple_of(b * tt, tt), tt)], hbuf.at[slot], lsems.at[slot])

    def wait_rows(slot):
        for _ in range(TOP_K):
            pltpu.make_async_copy(hbuf.at[slot], xs_ref.at[pl.ds(0, tt)], ssems.at[slot]).wait()

    @pl.when(i == 0)
    def _():
        load(0, 0).start()
        zero_scr[...] = jnp.zeros_like(zero_scr)

        def zero_block(blk):
            return pltpu.make_async_copy(zero_scr,
                                         xs_ref.at[pl.ds(pl.multiple_of(blk * MOE_ROWS, MOE_ROWS), MOE_ROWS)], zsem)

        def start_block(blk, carry):
            zero_block(blk).start()
            return carry

        def wait_block(blk, carry):
            zero_block(blk).wait()
            return carry

        def start_range(e, carry):
            return lax.fori_loop(zf_ref[e], zl_ref[e], start_block, carry)

        def wait_range(e, carry):
            return lax.fori_loop(zf_ref[e], zl_ref[e], wait_block, carry)

        lax.fori_loop(0, N_EXPERTS + 1, start_range, 0)
        lax.fori_loop(0, N_EXPERTS + 1, wait_range, 0)

    for slot in range(3):
        @pl.when(i % 3 == slot)
        def _():
            @pl.when(i + 1 < n)
            def _():
                load(i + 1, (slot + 1) % 3).start()

            load(i, slot).wait()

            def issue(t, carry):
                for k in range(TOP_K):
                    _row_copy(hbuf.at[slot], t, xs_ref, dest_ref[t * TOP_K + k], ssems.at[slot]).start(priority=k % 2)
                return carry

            lax.fori_loop(0, tt, issue, 0)

            @pl.when(i >= 1)
            def _():
                wait_rows((slot + 2) % 3)

            @pl.when(i == n - 1)
            def _():
                wait_rows(slot)


def _scatter_more_kernel(dest_ref, hm_ref, xs_in, xs_ref, sem):
    del xs_in
    _scatter_rows(dest_ref, hm_ref, xs_ref, sem)


def _scatter_init(zero_first, zero_last, dest_flat, hm, n_slot):
    n = hm.shape[0]
    tt = _tile(n, 512)
    n_t = n // tt
    grid_spec = pltpu.PrefetchScalarGridSpec(
        num_scalar_prefetch=2,
        grid=(n_t,),
        in_specs=[
            pl.BlockSpec((tt * TOP_K,), lambda i, zf, zl: (i,), memory_space=pltpu.SMEM),
            pl.BlockSpec(memory_space=pl.ANY),
        ],
        out_specs=pl.BlockSpec(memory_space=pl.ANY),
        scratch_shapes=[pltpu.VMEM((3, tt, D_MODEL), F32), pltpu.VMEM((MOE_ROWS, D_MODEL), F32),
                        pltpu.SemaphoreType.DMA((3,)), pltpu.SemaphoreType.DMA((3,)), pltpu.SemaphoreType.DMA(())],
    )
    return pl.pallas_call(
        functools.partial(_scatter_init_kernel, n=n_t, tt=tt),
        grid_spec=grid_spec,
        out_shape=jax.ShapeDtypeStruct((n_slot, D_MODEL), F32),
        compiler_params=_cparams("arbitrary"),
        name="moe_scatter_init",
    )(zero_first, zero_last, dest_flat, hm)


def _scatter_more(dest_flat, hm, xs):
    n = hm.shape[0]
    tt = _tile(n, 256)
    return pl.pallas_call(
        _scatter_more_kernel,
        grid=(n // tt,),
        in_specs=[
            pl.BlockSpec((tt * TOP_K,), lambda i: (i,), memory_space=pltpu.SMEM),
            pl.BlockSpec((tt, D_MODEL), lambda i: (i, 0)),
            pl.BlockSpec(memory_space=pl.ANY),
        ],
        out_specs=pl.BlockSpec(memory_space=pl.ANY),
        out_shape=jax.ShapeDtypeStruct(xs.shape, xs.dtype),
        scratch_shapes=[pltpu.SemaphoreType.DMA(())],
        input_output_aliases={2: 0},
        compiler_params=_cparams("arbitrary"),
        name="moe_scatter",
    )(dest_flat, hm, xs)


def _w1_split_kernel(w_ref, g_ref, l_ref, t_scr):
    n_slab = w_ref.shape[1] // LANES
    for c in range(n_slab):
        t_scr[c] = w_ref[0, c * LANES:(c + 1) * LANES, :].T
    for c in range(n_slab):
        g_ref[0, :, c * LANES:(c + 1) * LANES] = t_scr[c, pl.ds(0, D_EXPERT, stride=2), :].astype(BF16)
        l_ref[0, :, c * LANES:(c + 1) * LANES] = t_scr[c, pl.ds(1, D_EXPERT, stride=2), :].astype(BF16)


def _w1_split(w1):
    ks = 512
    n_slab = ks // LANES
    out = jax.ShapeDtypeStruct((N_EXPERTS, D_EXPERT, D_MODEL), BF16)
    return pl.pallas_call(
        _w1_split_kernel,
        grid=(N_EXPERTS, D_MODEL // ks),
        in_specs=[pl.BlockSpec((1, ks, 2 * D_EXPERT), lambda e, j: (e, j, 0))],
        out_specs=[pl.BlockSpec((1, D_EXPERT, ks), lambda e, j: (e, 0, j)),
                   pl.BlockSpec((1, D_EXPERT, ks), lambda e, j: (e, 0, j))],
        out_shape=[out, out],
        scratch_shapes=[pltpu.VMEM((n_slab, 2 * D_EXPERT, LANES), F32)],
        compiler_params=_cparams("parallel", "parallel"),
        name="w1_split",
    )(w1)


def _expert_kernel(be_ref, nb_ref, x_ref, w1g_ref, w1l_ref, b1g_ref, b1l_ref, w2_ref, b2_ref, o_ref, w2b_scr):
    i = pl.program_id(0)
    used = i < nb_ref[0]

    @pl.when(jnp.logical_not(used))
    def _():
        o_ref[...] = jnp.zeros_like(o_ref)

    @pl.when(jnp.logical_or(i == 0, be_ref[i] != be_ref[jnp.maximum(i - 1, 0)]))
    def _():
        w2b_scr[...] = w2_ref[0].astype(BF16)

    @pl.when(used)
    def _():
        x = x_ref[...].astype(BF16)
        nt = (((1,), (1,)), ((), ()))
        hg = lax.dot_general(x, w1g_ref[0], nt, preferred_element_type=F32) + b1g_ref[0]
        hl = lax.dot_general(x, w1l_ref[0], nt, preferred_element_type=F32) + b1l_ref[0]
        glu = jnp.minimum(hg, SWIGLU_LIMIT)
        lin = jnp.clip(hl, -SWIGLU_LIMIT, SWIGLU_LIMIT)
        act = glu * jax.nn.sigmoid(SWIGLU_ALPHA * glu) * (lin + 1.0)
        o_ref[...] = jnp.dot(act.astype(BF16), w2b_scr[...], preferred_element_type=F32) + b2_ref[0]


def _experts(blk_exp, n_used, xs, w1g_t, w1l_t, b1g, b1l, w2, b2):
    n_slot = xs.shape[0]
    n_blk = n_slot // MOE_ROWS
    row_in = lambda i, be, nb: (jnp.minimum(i, nb[0] - 1), 0)
    row = lambda i, be, nb: (i, 0)
    wsel = lambda i, be, nb: (be[i], 0, 0)
    grid_spec = pltpu.PrefetchScalarGridSpec(
        num_scalar_prefetch=2,
        grid=(n_blk,),
        in_specs=[
            pl.BlockSpec((MOE_ROWS, D_MODEL), row_in),
            pl.BlockSpec((1, D_EXPERT, D_MODEL), wsel),
            pl.BlockSpec((1, D_EXPERT, D_MODEL), wsel),
            pl.BlockSpec((1, 1, D_EXPERT), wsel),
            pl.BlockSpec((1, 1, D_EXPERT), wsel),
            pl.BlockSpec((1, D_EXPERT, D_MODEL), wsel),
            pl.BlockSpec((1, 1, D_MODEL), wsel),
        ],
        out_specs=pl.BlockSpec((MOE_ROWS, D_MODEL), row),
        scratch_shapes=[pltpu.VMEM((D_EXPERT, D_MODEL), BF16)],
    )
    return pl.pallas_call(
        _expert_kernel,
        grid_spec=grid_spec,
        out_shape=jax.ShapeDtypeStruct((n_slot, D_MODEL), F32),
        compiler_params=_cparams("arbitrary"),
        name="experts",
    )(blk_exp, n_used, xs, w1g_t, w1l_t, b1g, b1l, w2, b2)


def _combine_kernel(dest_ref, dnext_ref, x1_ref, g4_ref, nf_ref, ys_ref, o_ref, buf, sems, *, n):
    tt = x1_ref.shape[0]
    i = pl.program_id(0)

    def issue(d_ref, slot):
        def body(t, carry):
            for k in range(TOP_K):
                _row_copy(ys_ref, d_ref[t * TOP_K + k], buf.at[slot, k], t, sems.at[slot]).start(priority=k % 2)
            return carry

        lax.fori_loop(0, tt, body, 0, unroll=8)

    def finish(slot):
        for k in range(TOP_K):
            pltpu.make_async_copy(ys_ref.at[pl.ds(0, tt)], buf.at[slot, k], sems.at[slot]).wait()
        g4 = g4_ref[...]
        moe = g4[:, 0:1] * buf[slot, 0]
        for k in range(1, TOP_K):
            moe = moe + g4[:, k:k + 1] * buf[slot, k]
        o_ref[...] = _rms(x1_ref[...] + moe, nf_ref[...])

    @pl.when(i == 0)
    def _():
        issue(dest_ref, 0)

    for parity in range(2):
        @pl.when(i % 2 == parity)
        def _():
            @pl.when(i + 1 < n)
            def _():
                issue(dnext_ref, 1 - parity)

            finish(parity)


def _combine(dest_flat, x1, g4, nf, ys):
    n = x1.shape[0]
    tt = _tile(n, 512)
    n_t = n // tt
    row = lambda i: (i, 0)
    return pl.pallas_call(
        functools.partial(_combine_kernel, n=n_t),
        grid=(n_t,),
        in_specs=[
            pl.BlockSpec((tt * TOP_K,), lambda i: (i,), memory_space=pltpu.SMEM),
            pl.BlockSpec((tt * TOP_K,), lambda i: (jnp.minimum(i + 1, n_t - 1),), memory_space=pltpu.SMEM),
            pl.BlockSpec((tt, D_MODEL), row),
            pl.BlockSpec((tt, TOP_K), row),
            pl.BlockSpec((1, D_MODEL), lambda i: (0, 0)),
            pl.BlockSpec(memory_space=pl.ANY),
        ],
        out_specs=pl.BlockSpec((tt, D_MODEL), row),
        out_shape=jax.ShapeDtypeStruct((n, D_MODEL), F32),
        scratch_shapes=[pltpu.VMEM((2, TOP_K, tt, D_MODEL), F32), pltpu.SemaphoreType.DMA((2,))],
        compiler_params=_cparams("arbitrary"),
        name="moe_combine",
    )(dest_flat, dest_flat, x1, g4, nf, ys)


def _rope_table(pos):
    half = MLA_ROPE // 2
    inv_freq = ROPE_THETA ** (-jnp.arange(half, dtype=F32) / half)
    ang = pos.astype(F32)[:, None] * inv_freq[None, :]
    cos, sin = jnp.cos(ang), jnp.sin(ang)
    return jnp.concatenate([cos, cos, -sin, sin], axis=-1)


def _regroup_w_in(w_in):
    o = np.cumsum((Q_LORA, KV_LORA, MLA_ROPE, GLA_DK, GLA_DK, GLA_DV, GLA_GATE_RANK, GLA_DV, 2 * D_MODEL))
    c_q, c_kv, k_r = w_in[:, :o[0]], w_in[:, o[0]:o[1]], w_in[:, o[1]:o[2]]
    gq, gk, gv = w_in[:, o[2]:o[3]], w_in[:, o[3]:o[4]], w_in[:, o[4]:o[5]]
    ga, gg, mg = w_in[:, o[5]:o[6]], w_in[:, o[6]:o[7]], w_in[:, o[7]:o[8]]
    half = MLA_ROPE // 2
    k_r_sw = jnp.concatenate([k_r[:, half:], k_r[:, :half]], axis=1)
    ga_pad = jnp.zeros((D_MODEL, LANES - GLA_GATE_RANK), w_in.dtype)
    return jnp.concatenate([c_q, c_kv, k_r, k_r_sw, ga, ga_pad, gv, gg, gq, gk, mg], axis=1).astype(BF16)


def _regroup_w_uq(w_uq):
    w = w_uq.reshape(Q_LORA, MLA_HEADS, MLA_NOPE + MLA_ROPE)
    half = MLA_ROPE // 2
    nope, rope = w[..., :MLA_NOPE], w[..., MLA_NOPE:]
    rope_sw = jnp.concatenate([rope[..., half:], rope[..., :half]], axis=-1)
    return jnp.concatenate([nope, rope, rope_sw], axis=-1).reshape(Q_LORA, MLA_HEADS * HEAD_PAD).astype(BF16)


def _regroup_w_ukv(w_ukv):
    w = w_ukv.reshape(KV_LORA, MLA_HEADS, MLA_NOPE + MLA_V)
    k = w[..., :MLA_NOPE].reshape(KV_LORA, MLA_HEADS * MLA_NOPE)
    v = w[..., MLA_NOPE:].reshape(KV_LORA, MLA_HEADS * MLA_V)
    return jnp.concatenate([k, v], axis=1).astype(BF16)


def _per_head_w_ukv(w_ukv):
    w = w_ukv.reshape(KV_LORA, MLA_HEADS, MLA_NOPE + MLA_V)
    w_uk = jnp.transpose(w[..., :MLA_NOPE], (1, 2, 0)).astype(BF16)
    w_uv = jnp.transpose(w[..., MLA_NOPE:], (1, 0, 2)).astype(BF16)
    return w_uk, w_uv


def _slot_plan(counts_first, counts_rest, n_blk):
    counts_first = counts_first.astype(jnp.int32)
    counts = counts_first + counts_rest.astype(jnp.int32)
    padded = (counts + MOE_ROWS - 1) // MOE_ROWS * MOE_ROWS
    pad_end = jnp.cumsum(padded)
    pad_start = pad_end - padded
    n_used = (pad_end[-1] // MOE_ROWS).reshape(1)
    blk_start = jnp.arange(n_blk, dtype=jnp.int32) * MOE_ROWS
    blk_exp = jnp.sum(blk_start[:, None] >= pad_end[None, :], axis=1).astype(jnp.int32)
    last_exp = blk_exp[jnp.maximum(n_used[0] - 1, 0)]
    blk_exp = jnp.where(jnp.arange(n_blk) < n_used[0], blk_exp, last_exp)
    ps_row = jnp.pad(pad_start.astype(F32), (0, LANES - N_EXPERTS))[None, :]
    zero_first = jnp.concatenate([(pad_start + counts_first) // MOE_ROWS, n_used]).astype(jnp.int32)
    zero_last = jnp.concatenate([pad_end // MOE_ROWS, jnp.full((1,), n_blk)]).astype(jnp.int32)
    return ps_row, zero_first, zero_last, blk_exp, n_used


def _mixers(x, past_lat, past_kr, s0, past_len, p):
    bsz, seq, _ = x.shape
    n = bsz * seq
    x2d = x.reshape(n, D_MODEL)
    hin = _in_proj(x2d, p["norm1_g"], p["w_all"])
    tm = min(512, n)
    cs = _rope_table(past_len + jnp.arange(seq))
    if seq < tm:
        cs = jnp.tile(cs, (tm // seq, 1))
    if past_len:
        q, lat_new, krp_new = _mla_q(hin, cs, p["q_norm_g"], p["kv_norm_g"], p["w_uq"])
        attn = _mla_cached(q, past_lat, past_kr, lat_new, krp_new, p["w_uk"], p["w_uv"], bsz, seq)
    else:
        q, lat_new, krp_new, kcat, v = _mla_qkv(hin, cs, p["q_norm_g"], p["kv_norm_g"], p["w_uq"], p["w_ukv"])
        tq = min(1024, seq)
        tk = 1024 if seq % 1024 == 0 else seq
        attn = _flash(q.reshape(bsz, seq, -1), kcat.reshape(bsz, seq, -1), v.reshape(bsz, seq, -1), 0, tq, tk)
    og, s_new = _gla(hin, p["w_alpha"], p["b_alpha"], p["gla_norm_g"], s0, bsz, seq)
    x1, hm, sel, g4, cnt = _post_mix(attn.reshape(n, -1), og, hin, x2d, p["w_o_mla"], p["w_o_gla"], p["w_out"],
                                     p["norm2_g"], p["w_router"], p["b_router"])
    new_state = (lat_new.reshape(bsz, seq, KV_LORA), krp_new[:, :MLA_ROPE].reshape(bsz, seq, MLA_ROPE), s_new)
    return x1, hm, sel, g4, cnt, new_state


def kernel(x_prompt, x_sample, cache_mla_latent, cache_mla_krope, state_gla, norm1_g, w_in, q_norm_g, kv_norm_g,
           w_uq, w_ukv, w_o_mla, w_alpha2, b_alpha, gla_norm_g, w_o_gla, w_out, norm2_g, w_router, b_router,
           w1, b1, w2, b2, normf_g):
    depth = w_in.shape[0]
    assert depth == 1
    l = 0
    bp, sp, _ = x_prompt.shape
    bs, ss, _ = x_sample.shape
    past_len = cache_mla_latent.shape[2]
    p = {
        "norm1_g": norm1_g[l][None, :],
        "w_all": _regroup_w_in(w_in[l]),
        "q_norm_g": q_norm_g[l][None, :],
        "kv_norm_g": kv_norm_g[l][None, :],
        "w_uq": _regroup_w_uq(w_uq[l]),
        "w_ukv": _regroup_w_ukv(w_ukv[l]),
        "w_uk": _per_head_w_ukv(w_ukv[l])[0],
        "w_uv": _per_head_w_ukv(w_ukv[l])[1],
        "w_o_mla": w_o_mla[l].astype(BF16),
        "w_alpha": jnp.pad(w_alpha2[l], ((0, LANES - GLA_GATE_RANK), (0, 0))).astype(BF16),
        "b_alpha": b_alpha[l][None, :],
        "gla_norm_g": gla_norm_g[l][None, :],
        "w_o_gla": w_o_gla[l].astype(BF16),
        "w_out": w_out[l].astype(BF16),
        "norm2_g": norm2_g[l][None, :],
        "w_router": jnp.pad(w_router[l], ((0, 0), (0, LANES - N_EXPERTS))).astype(BF16),
        "b_router": jnp.pad(b_router[l], (0, LANES - N_EXPERTS))[None, :],
    }
    zeros_state = jnp.zeros((bp, GLA_HEADS, GLA_DK_HEAD, GLA_DV_HEAD), F32)
    x1p, hmp, selp, g4p, cntp, (lat_p, kr_p, st_p) = _mixers(x_prompt, None, None, zeros_state, 0, p)
    x1s, hms, sels, g4s, cnts, (lat_s, kr_s, st_s) = _mixers(x_sample, cache_mla_latent[l], cache_mla_krope[l],
                                                              state_gla[l], past_len, p)

    n_p, n_s = bp * sp, bs * ss
    n_asg = (n_p + n_s) * TOP_K
    n_slot = -(-n_asg // MOE_ROWS) * MOE_ROWS + N_EXPERTS * MOE_ROWS
    ps_row, zero_first, zero_last, blk_exp, n_used = _slot_plan(cntp[0, :N_EXPERTS], cnts[0, :N_EXPERTS],
                                                                n_slot // MOE_ROWS)
    dest_p = _dest(selp, ps_row).reshape(n_p * TOP_K)
    dest_s = _dest(sels, ps_row + cntp).reshape(n_s * TOP_K)
    xs = _scatter_init(zero_first, zero_last, dest_p, hmp, n_slot)
    xs = _scatter_more(dest_s, hms, xs)
    w1g_t, w1l_t = _w1_split(w1[l])
    b1g = b1[l][:, None, 0::2]
    b1lin = b1[l][:, None, 1::2]
    ys_slots = _experts(blk_exp, n_used, xs, w1g_t, w1l_t, b1g, b1lin, w2[l], b2[l][:, None, :])
    yp = _combine(dest_p, x1p, g4p, normf_g[None, :], ys_slots).reshape(bp, sp, D_MODEL)
    ys = _combine(dest_s, x1s, g4s, normf_g[None, :], ys_slots).reshape(bs, ss, D_MODEL)
    return (yp, ys, lat_p[None], kr_p[None], st_p[None], lat_s[None], kr_s[None], st_s[None])
```
